```python
import jax
import jax.numpy as jnp
from jax import lax
import numpy as np

D_MODEL = 2048
BATCH = 4
SEQ = 4096
DEPTH = 2
DEC_BATCH = 8
DEC_SEQ = 16
PAST_LEN = 1024

CHUNK = 64
Q_BLOCK = 128
D_MIX = D_MODEL
D_A = D_MIX // 2
H_A = 4
DV_A = D_A // H_A
DK_A = DV_A // 2
D_B = D_MIX - D_A
H_B = 8
DH_B = D_B // H_B
D_FF = 4 * D_MODEL
N_GATES = 2 * H_A + H_B
ALPHA = (2 * DEPTH) ** 0.25
BETA = (8 * DEPTH) ** -0.25
LN_EPS = 1e-5
HEAD_NORM_EPS = 1e-6
IN_WIDTHS = (H_A * DK_A, H_A * DK_A, D_A, D_A, H_A, H_A, D_B, D_B, D_B, H_B)
N_IN = sum(IN_WIDTHS)
SPLIT_POINTS = tuple(int(s) for s in np.cumsum(IN_WIDTHS)[:-1])

kernel_name = 'hybrid_mlstm_fox_stream_step'


def layer_norm(x, g, b):
    xf = x.astype(jnp.float32)
    mu = jnp.mean(xf, axis=-1, keepdims=True)
    var = jnp.mean(jnp.square(xf - mu), axis=-1, keepdims=True)
    y = (xf - mu) * lax.rsqrt(var + LN_EPS) * g.astype(jnp.float32) + b.astype(jnp.float32)
    return y.astype(x.dtype)


def heads_first(t, n_heads):
    b, l, _ = t.shape
    return t.reshape(b, l, n_heads, -1).transpose(0, 2, 1, 3).astype(jnp.float32)


def mlstm_chunk(carry, inp):
    c, n, m = carry
    q, k, v, ig, lf = inp
    cl = q.shape[2]
    b = jnp.cumsum(lf, axis=-1)
    causal = jnp.tril(jnp.ones((cl, cl), dtype=bool))
    d = jnp.where(causal, b[..., :, None] - b[..., None, :] + ig[..., None, :], -jnp.inf)
    g = b + m[..., None]
    m_t = jnp.maximum(g, jnp.max(d, axis=-1))
    w_inter = jnp.exp(g - m_t)
    s = jnp.einsum('bhtk,bhsk->bhts', q, k) * jnp.exp(d - m_t[..., None])
    num = w_inter[..., None] * jnp.einsum('bhvk,bhtk->bhtv', c, q) + jnp.einsum('bhts,bhsv->bhtv', s, v)
    den = w_inter * jnp.einsum('bhk,bhtk->bht', n, q) + jnp.sum(s, axis=-1)
    h = num / jnp.maximum(jnp.abs(den), jnp.exp(-m_t))[..., None]
    m_new = m_t[..., -1]
    a = jnp.exp(b[..., -1:] - b + ig - m_new[..., None])
    decay = jnp.exp(b[..., -1] + m - m_new)
    c_new = decay[..., None, None] * c + jnp.einsum('bhs,bhsv,bhsk->bhvk', a, v, k)
    n_new = decay[..., None] * n + jnp.einsum('bhs,bhsk->bhk', a, k)
    return (c_new, n_new, m_new), h


def mlstm_sequence(q, k, v, ig, lf, c0, n0, m0):
    bsz, nh, l, _ = q.shape
    cl = min(CHUNK, l)
    nc = l // cl

    def to_chunks(t):
        return jnp.moveaxis(t.reshape(t.shape[:2] + (nc, cl) + t.shape[3:]), 2, 0)

    carry0 = (c0.astype(jnp.float32), n0.astype(jnp.float32), m0.astype(jnp.float32))
    xs = (to_chunks(q), to_chunks(k), to_chunks(v), to_chunks(ig), to_chunks(lf))
    (c, n, m), h = lax.scan(mlstm_chunk, carry0, xs)
    h = jnp.moveaxis(h, 0, 2).reshape(bsz, nh, l, -1)
    return h, c, n, m


def mlstm_mixer(qa, ka, va, oa, ig, lf, g_head, c0, n0, m0):
    bsz, l, _ = qa.shape
    q = heads_first(qa, H_A) * (DK_A ** -0.5)
    k = heads_first(ka, H_A)
    v = heads_first(va, H_A)
    h, c, n, m = mlstm_sequence(q, k, v, ig.transpose(0, 2, 1), lf.transpose(0, 2, 1), c0, n0, m0)
    h = h * lax.rsqrt(jnp.mean(jnp.square(h), axis=-1, keepdims=True) + HEAD_NORM_EPS)
    h = h.transpose(0, 2, 1, 3).reshape(bsz, l, D_A) * g_head.astype(jnp.float32)
    h = h * jax.nn.sigmoid(oa.astype(jnp.float32))
    return h.astype(qa.dtype), c, n, m


def fox_attend(q, fq, pos_q, k, v, fk):
    logits = jnp.einsum('bthd,bshd->bhts', q, k, preferred_element_type=jnp.float32) * (DH_B ** -0.5)
    logits = logits + fq.transpose(0, 2, 1)[..., :, None] - fk.transpose(0, 2, 1)[..., None, :]
    visible = jnp.arange(k.shape[1])[None, :] <= pos_q[:, None]
    p = jax.nn.softmax(jnp.where(visible, logits, -jnp.inf), axis=-1)
    return jnp.einsum('bhts,bshd->bthd', p.astype(v.dtype), v)


def fox_prompt(q, k, v, lf):
    bsz, l, nh, dh = q.shape
    f = jnp.cumsum(lf, axis=1)

    def block(i):
        start = i * Q_BLOCK
        qb = lax.dynamic_slice_in_dim(q, start, Q_BLOCK, axis=1)
        fb = lax.dynamic_slice_in_dim(f, start, Q_BLOCK, axis=1)
        return fox_attend(qb, fb, start + jnp.arange(Q_BLOCK), k, v, f)

    out = lax.map(block, jnp.arange(l // Q_BLOCK))
    return jnp.moveaxis(out, 0, 1).reshape(bsz, l, nh * dh)


def fox_sample(q, k, v, lf, ck, cv, clf):
    bsz, l, nh, dh = q.shape
    past = ck.shape[1]
    k_all = jnp.concatenate([ck.astype(k.dtype), k], axis=1)
    v_all = jnp.concatenate([cv.astype(v.dtype), v], axis=1)
    f = jnp.cumsum(jnp.concatenate([clf.astype(jnp.float32), lf], axis=1), axis=1)
    out = fox_attend(q, f[:, past:], past + jnp.arange(l), k_all, v_all, f)
    return out.reshape(bsz, l, nh * dh)


def trunk_layer(x, past, w_in, b_gates, g_mlstm, w_out, ln1_g, ln1_b, w_up, w_down, ln2_g, ln2_b):
    bsz, l, _ = x.shape
    qa, ka, va, oa, ia, fa, qb, kb, vb, fb = jnp.split(x @ w_in, SPLIT_POINTS, axis=-1)
    gates = jnp.concatenate([ia, fa, fb], axis=-1).astype(jnp.float32) + b_gates.astype(jnp.float32)
    ig = gates[..., :H_A]
    lfa = jax.nn.log_sigmoid(gates[..., H_A:2 * H_A])
    lfb = jax.nn.log_sigmoid(gates[..., 2 * H_A:])
    qb = qb.reshape(bsz, l, H_B, DH_B)
    kb = kb.reshape(bsz, l, H_B, DH_B)
    vb = vb.reshape(bsz, l, H_B, DH_B)
    if past is None:
        c0 = jnp.zeros((bsz, H_A, DV_A, DK_A), jnp.float32)
        n0 = jnp.zeros((bsz, H_A, DK_A), jnp.float32)
        m0 = jnp.zeros((bsz, H_A), jnp.float32)
        h_b = fox_prompt(qb, kb, vb, lfb)
    else:
        ck, cv, clf, c0, n0, m0 = past
        h_b = fox_sample(qb, kb, vb, lfb, ck, cv, clf)
    h_a, c, n, m = mlstm_mixer(qa, ka, va, oa, ig, lfa, g_mlstm, c0, n0, m0)
    mix = jnp.concatenate([h_a, h_b.astype(h_a.dtype)], axis=-1) @ w_out
    x1 = layer_norm(ALPHA * x + mix, ln1_g, ln1_b)
    ff = jnp.square(jax.nn.relu(x1 @ w_up)) @ w_down
    y = layer_norm(ALPHA * x1 + ff, ln2_g, ln2_b)
    dt = x.dtype
    return y, (kb, vb, lfb.astype(dt), c.astype(dt), n.astype(dt), m.astype(dt))


def stack_field(states, i):
    return jnp.stack([s[i] for s in states], axis=0)


def setup_inputs(seed: int = 0) -> dict:
    key = jax.random.key(seed)
    ks = jax.random.split(key, 20)
    nrm = jax.random.normal
    f32 = jnp.float32
    x_prompt = nrm(ks[0], (BATCH, SEQ, D_MODEL), f32)
    x_sample = nrm(ks[1], (DEC_BATCH, DEC_SEQ, D_MODEL), f32)
    cache_fox_k = nrm(ks[2], (DEPTH, DEC_BATCH, PAST_LEN, H_B, DH_B), f32)
    cache_fox_v = nrm(ks[3], (DEPTH, DEC_BATCH, PAST_LEN, H_B, DH_B), f32)
    cache_fox_logf = jax.nn.log_sigmoid(3.0 + nrm(ks[4], (DEPTH, DEC_BATCH, PAST_LEN, H_B), f32))
    state_mlstm_c = 0.1 * nrm(ks[5], (DEPTH, DEC_BATCH, H_A, DV_A, DK_A), f32)
    state_mlstm_n = 0.1 * nrm(ks[6], (DEPTH, DEC_BATCH, H_A, DK_A), f32)
    state_mlstm_m = 1.0 + 0.5 * nrm(ks[7], (DEPTH, DEC_BATCH, H_A), f32)
    w_in = nrm(ks[8], (DEPTH, D_MODEL, N_IN), f32) * (D_MODEL ** -0.5)
    gate_base = jnp.concatenate([jnp.full((H_A,), -1.0, f32),
                                 jnp.linspace(3.0, 6.0, H_A, dtype=f32),
                                 jnp.linspace(1.0, 5.0, H_B, dtype=f32)])
    b_gates = gate_base[None, :] + 0.1 * nrm(ks[9], (DEPTH, N_GATES), f32)
    g_mlstm = 1.0 + 0.01 * nrm(ks[10], (DEPTH, D_A), f32)
    w_out = nrm(ks[11], (DEPTH, D_MIX, D_MODEL), f32) * (D_MIX ** -0.5 * BETA)
    ln1_g = 1.0 + 0.01 * nrm(ks[12], (DEPTH, D_MODEL), f32)
    ln1_b = 0.01 * nrm(ks[13], (DEPTH, D_MODEL), f32)
    w_up = nrm(ks[14], (DEPTH, D_MODEL, D_FF), f32) * (D_MODEL ** -0.5)
    w_down = nrm(ks[15], (DEPTH, D_FF, D_MODEL), f32) * (D_FF ** -0.5 * BETA)
    ln2_g = 1.0 + 0.01 * nrm(ks[16], (DEPTH, D_MODEL), f32)
    ln2_b = 0.01 * nrm(ks[17], (DEPTH, D_MODEL), f32)
    return {'x_prompt': x_prompt, 'x_sample': x_sample,
            'cache_fox_k': cache_fox_k, 'cache_fox_v': cache_fox_v, 'cache_fox_logf': cache_fox_logf,
            'state_mlstm_c': state_mlstm_c, 'state_mlstm_n': state_mlstm_n, 'state_mlstm_m': state_mlstm_m,
            'w_in': w_in, 'b_gates': b_gates, 'g_mlstm': g_mlstm, 'w_out': w_out,
            'ln1_g': ln1_g, 'ln1_b': ln1_b, 'w_up': w_up, 'w_down': w_down,
            'ln2_g': ln2_g, 'ln2_b': ln2_b}


def reference(x_prompt, x_sample, cache_fox_k, cache_fox_v, cache_fox_logf, state_mlstm_c,
              state_mlstm_n, state_mlstm_m, w_in, b_gates, g_mlstm, w_out, ln1_g, ln1_b,
              w_up, w_down, ln2_g, ln2_b):
    yp = x_prompt
    ys = x_sample
    new_p = []
    new_s = []
    for layer in range(DEPTH):
        weights = (w_in[layer], b_gates[layer], g_mlstm[layer], w_out[layer], ln1_g[layer],
                   ln1_b[layer], w_up[layer], w_down[layer], ln2_g[layer], ln2_b[layer])
        yp, st_p = trunk_layer(yp, None, *weights)
        past = (cache_fox_k[layer], cache_fox_v[layer], cache_fox_logf[layer],
                state_mlstm_c[layer], state_mlstm_n[layer], state_mlstm_m[layer])
        ys, st_s = trunk_layer(ys, past, *weights)
        new_p.append(st_p)
        new_s.append(st_s)
    fox_k_p = stack_field(new_p, 0)
    fox_v_p = stack_field(new_p, 1)
    fox_logf_p = stack_field(new_p, 2)
    mlstm_c_p = stack_field(new_p, 3)
    mlstm_n_p = stack_field(new_p, 4)
    mlstm_m_p = stack_field(new_p, 5)
    fox_k_s = stack_field(new_s, 0)
    fox_v_s = stack_field(new_s, 1)
    fox_logf_s = stack_field(new_s, 2)
    mlstm_c_s = stack_field(new_s, 3)
    mlstm_n_s = stack_field(new_s, 4)
    mlstm_m_s = stack_field(new_s, 5)
    return (yp, ys, fox_k_p, fox_v_p, fox_logf_p, mlstm_c_p, mlstm_n_p, mlstm_m_p,
            fox_k_s, fox_v_s, fox_logf_s, mlstm_c_s, mlstm_n_s, mlstm_m_s)
```

```python
import functools

import jax
import jax.numpy as jnp
from jax import lax
from jax.experimental import pallas as pl
from jax.experimental.pallas import tpu as pltpu

F32 = jnp.float32
BF16 = jnp.bfloat16

LN_EPS = 1e-5
HEAD_NORM_EPS = 1e-6
LANES = 128
GATE_COLS = LANES
VMEM_LIMIT = 56 * 1024 * 1024


def _cparams(sem):
    return pltpu.CompilerParams(dimension_semantics=sem, vmem_limit_bytes=VMEM_LIMIT)


def _tile(n, pref):
    t = min(n, pref)
    assert n % t == 0, (n, pref)
    return t


def _log_sigmoid(g):
    return jnp.minimum(g, 0.0) - jnp.log1p(jnp.exp(-jnp.abs(g)))


def _dot_f32(a, b):
    a_hi = a.astype(BF16)
    b_hi = b.astype(BF16)
    a_lo = (a - a_hi.astype(F32)).astype(BF16)
    b_lo = (b - b_hi.astype(F32)).astype(BF16)
    dot = functools.partial(jnp.dot, preferred_element_type=F32)
    return dot(a_hi, b_hi) + dot(a_lo, b_hi) + dot(a_hi, b_lo)


def _dot_nt(a, b):
    return lax.dot_general(a, b, (((1,), (1,)), ((), ())), preferred_element_type=F32)


def _dot_tn(a, b):
    return lax.dot_general(a, b, (((0,), (0,)), ((), ())), preferred_element_type=F32)


def _layer_norm(z, g, b):
    mu = jnp.mean(z, axis=-1, keepdims=True)
    zc = z - mu
    var = jnp.mean(zc * zc, axis=-1, keepdims=True)
    return zc * lax.rsqrt(var + LN_EPS) * g + b


def _inproj_kernel(x_ref, w_ref, slab_ref, k_ref, v_ref, xb_ref, *, jk0, jv0):
    j = pl.program_id(1)

    @pl.when(j == 0)
    def _():
        xb_ref[...] = x_ref[...].astype(BF16)

    acc = jnp.dot(xb_ref[...], w_ref[...], preferred_element_type=F32)
    slab_ref[...] = acc.astype(BF16)

    @pl.when((j >= jk0) & (j < jv0))
    def _():
        k_ref[...] = acc

    @pl.when(j >= jv0)
    def _():
        v_ref[...] = acc


def _inproj(x2, w_main, d_b, tm_pref):
    m, d = x2.shape
    n = w_main.shape[1]
    tm = _tile(m, tm_pref)
    tn = _tile(d_b, 512)
    nj = n // tn
    nkv = d_b // tn
    jk0 = nj - 2 * nkv
    jv0 = nj - nkv
    return pl.pallas_call(
        functools.partial(_inproj_kernel, jk0=jk0, jv0=jv0),
        grid=(m // tm, nj),
        in_specs=[pl.BlockSpec((tm, d), lambda i, j: (i, 0)),
                  pl.BlockSpec((d, tn), lambda i, j: (0, j))],
        out_specs=[pl.BlockSpec((tm, tn), lambda i, j: (i, j)),
                   pl.BlockSpec((tm, tn), lambda i, j: (i, jnp.clip(j - jk0, 0, nkv - 1))),
                   pl.BlockSpec((tm, tn), lambda i, j: (i, jnp.clip(j - jv0, 0, nkv - 1)))],
        out_shape=[jax.ShapeDtypeStruct((m, n), BF16),
                   jax.ShapeDtypeStruct((m, d_b), F32),
                   jax.ShapeDtypeStruct((m, d_b), F32)],
        scratch_shapes=[pltpu.VMEM((tm, d), BF16)],
        compiler_params=_cparams(("parallel", "arbitrary")),
        name="inproj",
    )(x2, w_main)


def _gates_kernel(x_ref, wg_ref, bg_ref, lf_ref, gc_ref, gr_ref, carry_ref, *, n_raw, n_rows):
    t = pl.program_id(1)

    @pl.when(t == 0)
    def _():
        carry_ref[...] = jnp.zeros_like(carry_ref)

    g = _dot_f32(x_ref[...], wg_ref[...]) + bg_ref[...]
    tl = g.shape[0]
    lane = lax.broadcasted_iota(jnp.int32, g.shape, 1)
    row = lax.broadcasted_iota(jnp.int32, g.shape, 0)
    raw = lane < n_raw
    val = jnp.where(raw, g, _log_sigmoid(g))
    lf_ref[...] = val
    c = val
    s = 1
    while s < tl:
        c = c + jnp.where(row >= s, pltpu.roll(c, s, 0), 0.0)
        s *= 2
    c = c + carry_ref[...]
    carry_ref[...] = c[tl - 1:tl, :]
    out = jnp.where(raw, g, c)
    gc_ref[...] = out
    gr_ref[...] = out.T[:n_rows, :]


def _gates(x3, w_gate, b_gate, n_raw, n_rows):
    b, l, d = x3.shape
    tl = _tile(l, 512)
    return pl.pallas_call(
        functools.partial(_gates_kernel, n_raw=n_raw, n_rows=n_rows),
        grid=(b, l // tl),
        in_specs=[pl.BlockSpec((None, tl, d), lambda i, t: (i, t, 0)),
                  pl.BlockSpec((d, GATE_COLS), lambda i, t: (0, 0)),
                  pl.BlockSpec((1, GATE_COLS), lambda i, t: (0, 0))],
        out_specs=[pl.BlockSpec((None, tl, GATE_COLS), lambda i, t: (i, t, 0)),
                   pl.BlockSpec((None, tl, GATE_COLS), lambda i, t: (i, t, 0)),
                   pl.BlockSpec((None, n_rows, tl), lambda i, t: (i, 0, t))],
        out_shape=[jax.ShapeDtypeStruct((b, l, GATE_COLS), F32),
                   jax.ShapeDtypeStruct((b, l, GATE_COLS), F32),
                   jax.ShapeDtypeStruct((b, n_rows, l), F32)],
        scratch_shapes=[pltpu.VMEM((1, GATE_COLS), F32)],
        compiler_params=_cparams(("parallel", "arbitrary")),
        name="gates",
    )(x3, w_gate, b_gate)


def _mlstm_kernel(q_ref, k_ref, v_ref, o_ref, gc_ref, gr_ref, gh_ref, c0_ref, n0_ref, m0_ref,
                  h_ref, c_ref, n_ref, m_ref, ct_s, n_s, m_s, fp_s, *, n_heads, dk, dv):
    c_idx = pl.program_id(1)
    n_chunks = pl.num_programs(1)
    lc = q_ref.shape[0]
    scale = dk ** -0.5

    @pl.when(c_idx == 0)
    def _():
        for h in range(n_heads):
            ct_s[h] = c0_ref[h].T
        n_s[...] = n0_ref[...]
        m_s[...] = m0_ref[...]
        fp_s[...] = jnp.zeros_like(fp_s)

    row = lax.broadcasted_iota(jnp.int32, (lc, lc), 0)
    col = lax.broadcasted_iota(jnp.int32, (lc, lc), 1)
    causal = col <= row
    gc = gc_ref[...]
    gr = gr_ref[...]
    fp = fp_s[...]
    m_all = m_s[...]
    for h in range(n_heads):
        q = q_ref[:, h * dk:(h + 1) * dk]
        k = k_ref[:, h * dk:(h + 1) * dk]
        v = v_ref[:, h * dv:(h + 1) * dv]
        f0 = fp[:, n_heads + h:n_heads + h + 1]
        ig_row = gr[h:h + 1, :]
        ig_col = gc[:, h:h + 1]
        b_row = gr[n_heads + h:n_heads + h + 1, :] - f0
        b_col = gc[:, n_heads + h:n_heads + h + 1] - f0
        m_prev = m_all[:, h:h + 1]
        ct = ct_s[h]
        nvec = n_s[h:h + 1, :]

        d = jnp.where(causal, b_col - b_row + ig_row, -jnp.inf)
        g = b_col + m_prev
        m_t = jnp.maximum(g, jnp.max(d, axis=1, keepdims=True))
        w_inter = jnp.exp(g - m_t)
        s = _dot_nt(q, k) * scale * jnp.exp(d - m_t)
        inter = jnp.dot(q, ct.astype(BF16), preferred_element_type=F32) * scale
        num = w_inter * inter + jnp.dot(s.astype(BF16), v, preferred_element_type=F32)
        qn = jnp.sum(q.astype(F32) * nvec, axis=1, keepdims=True) * scale
        den = w_inter * qn + jnp.sum(s, axis=1, keepdims=True)
        hh = num / jnp.maximum(jnp.abs(den), jnp.exp(-m_t))

        m_new = m_t[lc - 1:lc, :]
        b_last = b_col[lc - 1:lc, :]
        a_col = jnp.exp(b_last - b_col + ig_col - m_new)
        decay = jnp.exp(b_last + m_prev - m_new)
        ka = k.astype(F32) * a_col
        ct_s[h] = decay * ct + _dot_tn(ka.astype(BF16), v)
        n_s[h:h + 1, :] = decay * nvec + jnp.sum(ka, axis=0, keepdims=True)
        m_s[:, h:h + 1] = m_new

        hn = hh * lax.rsqrt(jnp.mean(hh * hh, axis=1, keepdims=True) + HEAD_NORM_EPS)
        og = o_ref[:, h * dv:(h + 1) * dv].astype(F32)
        hn = hn * gh_ref[:, h * dv:(h + 1) * dv] * jax.nn.sigmoid(og)
        h_ref[:, h * dv:(h + 1) * dv] = hn.astype(h_ref.dtype)
    fp_s[...] = gc[lc - 1:lc, :]

    @pl.when(c_idx == n_chunks - 1)
    def _():
        for h in range(n_heads):
            c_ref[h] = ct_s[h].T
        n_ref[...] = n_s[...]
        m_ref[...] = m_s[...]


def _mlstm(slab3, gc, gr, g_head, c0, n0, m0, n_heads, dk, dv, lc_pref):
    b, l, _ = slab3.shape
    lc = _tile(l, lc_pref)
    dq = n_heads * dk
    da = n_heads * dv
    assert da == 2 * dq
    m0r = m0.reshape(b, 1, n_heads)
    bmap = lambda i, c: (i, 0, 0)
    outs = pl.pallas_call(
        functools.partial(_mlstm_kernel, n_heads=n_heads, dk=dk, dv=dv),
        grid=(b, l // lc),
        in_specs=[pl.BlockSpec((None, lc, dq), lambda i, c: (i, c, 0)),
                  pl.BlockSpec((None, lc, dq), lambda i, c: (i, c, 1)),
                  pl.BlockSpec((None, lc, da), lambda i, c: (i, c, 1)),
                  pl.BlockSpec((None, lc, da), lambda i, c: (i, c, 2)),
                  pl.BlockSpec((None, lc, GATE_COLS), lambda i, c: (i, c, 0)),
                  pl.BlockSpec((None, gr.shape[1], lc), lambda i, c: (i, 0, c)),
                  pl.BlockSpec((1, da), lambda i, c: (0, 0)),
                  pl.BlockSpec((None, n_heads, dv, dk), lambda i, c: (i, 0, 0, 0)),
                  pl.BlockSpec((None, n_heads, dk), bmap),
                  pl.BlockSpec((None, 1, n_heads), bmap)],
        out_specs=[pl.BlockSpec((None, lc, da), lambda i, c: (i, c, 0)),
                   pl.BlockSpec((None, n_heads, dv, dk), lambda i, c: (i, 0, 0, 0)),
                   pl.BlockSpec((None, n_heads, dk), bmap),
                   pl.BlockSpec((None, 1, n_heads), bmap)],
        out_shape=[jax.ShapeDtypeStruct((b, l, da), BF16),
                   jax.ShapeDtypeStruct((b, n_heads, dv, dk), F32),
                   jax.ShapeDtypeStruct((b, n_heads, dk), F32),
                   jax.ShapeDtypeStruct((b, 1, n_heads), F32)],
        scratch_shapes=[pltpu.VMEM((n_heads, dk, dv), F32),
                        pltpu.VMEM((n_heads, dk), F32),
                        pltpu.VMEM((1, n_heads), F32),
                        pltpu.VMEM((1, GATE_COLS), F32)],
        compiler_params=_cparams(("parallel", "arbitrary")),
        name="mlstm",
    )(slab3, slab3, slab3, slab3, gc, gr, g_head, c0, n0, m0r)
    h_a, c, n, m = outs
    return h_a, c, n, m.reshape(b, n_heads)


def _fox_prompt_kernel(q_ref, k_ref, v_ref, gc_ref, fk_ref, o_ref, *, lane0, tq, scale):
    h = pl.program_id(1)
    i = pl.program_id(2)
    q = q_ref[...]
    lane = lax.broadcasted_iota(jnp.int32, gc_ref.shape, 1)
    fq = jnp.sum(jnp.where(lane == lane0 + h, gc_ref[...], 0.0), axis=1, keepdims=True)

    def step(j, carry, diagonal):
        m, l, acc = carry
        start = pl.multiple_of(j * tq, tq)
        ks = k_ref[pl.ds(start, tq), :]
        vs = v_ref[pl.ds(start, tq), :]
        fk = fk_ref[:, pl.ds(start, tq)]
        s = _dot_nt(q, ks) * scale + fq - fk
        if diagonal:
            row = lax.broadcasted_iota(jnp.int32, s.shape, 0)
            col = lax.broadcasted_iota(jnp.int32, s.shape, 1)
            s = jnp.where(col <= row, s, -jnp.inf)
        m_new = jnp.maximum(m, jnp.max(s, axis=1, keepdims=True))
        alpha = jnp.exp(m - m_new)
        p = jnp.exp(s - m_new)
        l = alpha * l + jnp.sum(p, axis=1, keepdims=True)
        acc = alpha * acc + jnp.dot(p.astype(BF16), vs, preferred_element_type=F32)
        return m_new, l, acc

    init = (jnp.full((tq, 1), -jnp.inf, F32), jnp.zeros((tq, 1), F32), jnp.zeros(q.shape, F32))
    carry = lax.fori_loop(0, i, lambda j, c: step(j, c, False), init)
    _, l, acc = step(i, carry, True)
    o_ref[...] = (acc / l).astype(o_ref.dtype)


def _fox_prompt(slab3, gc, gr4, n_heads, dh, q_col, lane0, tq_pref):
    b, l, _ = slab3.shape
    tq = _tile(l, tq_pref)
    return pl.pallas_call(
        functools.partial(_fox_prompt_kernel, lane0=lane0, tq=tq, scale=dh ** -0.5),
        grid=(b, n_heads, l // tq),
        in_specs=[pl.BlockSpec((None, tq, dh), lambda bi, h, i: (bi, i, q_col + h)),
                  pl.BlockSpec((None, l, dh), lambda bi, h, i: (bi, 0, q_col + n_heads + h)),
                  pl.BlockSpec((None, l, dh), lambda bi, h, i: (bi, 0, q_col + 2 * n_heads + h)),
                  pl.BlockSpec((None, tq, GATE_COLS), lambda bi, h, i: (bi, i, 0)),
                  pl.BlockSpec((None, None, 1, l), lambda bi, h, i: (bi, lane0 + h, 0, 0))],
        out_specs=pl.BlockSpec((None, tq, dh), lambda bi, h, i: (bi, i, h)),
        out_shape=jax.ShapeDtypeStruct((b, l, n_heads * dh), BF16),
        compiler_params=_cparams(("parallel", "parallel", "arbitrary")),
        name="fox_prompt",
    )(slab3, slab3, slab3, gc, gr4)


def _fox_sample_kernel(q_ref, k_ref, v_ref, ck_ref, cv_ref, clf_ref, gc_ref, fk_ref, o_ref, *, lane0, scale):
    h = pl.program_id(1)
    q = q_ref[...]
    l = q.shape[0]
    past = clf_ref.shape[1]
    lane = lax.broadcasted_iota(jnp.int32, gc_ref.shape, 1)
    f_new_col = jnp.sum(jnp.where(lane == lane0 + h, gc_ref[...], 0.0), axis=1, keepdims=True)
    f_new_row = fk_ref[...]
    cum = clf_ref[...]
    pos = lax.broadcasted_iota(jnp.int32, cum.shape, 1)
    s = 1
    while s < past:
        cum = cum + jnp.where(pos >= s, pltpu.roll(cum, s, 1), 0.0)
        s *= 2
    total = cum[:, past - 1:past]
    fq = total + f_new_col
    s_old = _dot_nt(q, ck_ref[...].astype(BF16)) * scale + fq - cum
    s_new = _dot_nt(q, k_ref[...]) * scale + fq - (total + f_new_row)
    row = lax.broadcasted_iota(jnp.int32, s_new.shape, 0)
    col = lax.broadcasted_iota(jnp.int32, s_new.shape, 1)
    s_new = jnp.where(col <= row, s_new, -jnp.inf)
    m = jnp.maximum(jnp.max(s_old, axis=1, keepdims=True), jnp.max(s_new, axis=1, keepdims=True))
    p_old = jnp.exp(s_old - m)
    p_new = jnp.exp(s_new - m)
    den = jnp.sum(p_old, axis=1, keepdims=True) + jnp.sum(p_new, axis=1, keepdims=True)
    acc = (jnp.dot(p_old.astype(BF16), cv_ref[...].astype(BF16), preferred_element_type=F32)
           + jnp.dot(p_new.astype(BF16), v_ref[...], preferred_element_type=F32))
    o_ref[...] = (acc / den).astype(o_ref.dtype)


def _fox_sample(slab3, ck, cv, clf4, gc, gr4, n_heads, dh, q_col, lane0):
    b, l, _ = slab3.shape
    past = ck.shape[1]
    return pl.pallas_call(
        functools.partial(_fox_sample_kernel, lane0=lane0, scale=dh ** -0.5),
        grid=(b, n_heads),
        in_specs=[pl.BlockSpec((None, l, dh), lambda bi, h: (bi, 0, q_col + h)),
                  pl.BlockSpec((None, l, dh), lambda bi, h: (bi, 0, q_col + n_heads + h)),
                  pl.BlockSpec((None, l, dh), lambda bi, h: (bi, 0, q_col + 2 * n_heads + h)),
                  pl.BlockSpec((None, past, dh), lambda bi, h: (bi, 0, h)),
                  pl.BlockSpec((None, past, dh), lambda bi, h: (bi, 0, h)),
                  pl.BlockSpec((None, None, 1, past), lambda bi, h: (bi, h, 0, 0)),
                  pl.BlockSpec((None, l, GATE_COLS), lambda bi, h: (bi, 0, 0)),
                  pl.BlockSpec((None, None, 1, l), lambda bi, h: (bi, lane0 + h, 0, 0))],
        out_specs=pl.BlockSpec((None, l, dh), lambda bi, h: (bi, 0, h)),
        out_shape=jax.ShapeDtypeStruct((b, l, n_heads * dh), BF16),
        compiler_params=_cparams(("parallel", "parallel")),
        name="fox_sample",
    )(slab3, slab3, slab3, ck, cv, clf4, gc, gr4)


def _outproj_kernel(ha_ref, hb_ref, x_ref, wa_ref, wb_ref, g_ref, b_ref, y_ref, *, alpha):
    mix = (jnp.dot(ha_ref[...], wa_ref[...], preferred_element_type=F32)
           + jnp.dot(hb_ref[...], wb_ref[...], preferred_element_type=F32))
    y_ref[...] = _layer_norm(alpha * x_ref[...] + mix, g_ref[...], b_ref[...])


def _outproj(ha2, hb2, x2, w_a, w_b, ln_g, ln_b, alpha, tm_pref):
    m, d = x2.shape
    da = ha2.shape[1]
    db = hb2.shape[1]
    tm = _tile(m, tm_pref)
    const = lambda i: (0, 0)
    return pl.pallas_call(
        functools.partial(_outproj_kernel, alpha=alpha),
        grid=(m // tm,),
        in_specs=[pl.BlockSpec((tm, da), lambda i: (i, 0)),
                  pl.BlockSpec((tm, db), lambda i: (i, 0)),
                  pl.BlockSpec((tm, d), lambda i: (i, 0)),
                  pl.BlockSpec((da, d), const),
                  pl.BlockSpec((db, d), const),
                  pl.BlockSpec((1, d), const),
                  pl.BlockSpec((1, d), const)],
        out_specs=pl.BlockSpec((tm, d), lambda i: (i, 0)),
        out_shape=jax.ShapeDtypeStruct((m, d), F32),
        compiler_params=_cparams(("parallel",)),
        name="outproj_ln",
    )(ha2, hb2, x2, w_a, w_b, ln_g, ln_b)


def _ffn_kernel(x_ref, wu_ref, wd_ref, g_ref, b_ref, y_ref, xb_ref, acc_ref, *, alpha):
    j = pl.program_id(1)

    @pl.when(j == 0)
    def _():
        xb_ref[...] = x_ref[...].astype(BF16)
        acc_ref[...] = jnp.zeros_like(acc_ref)

    u = jnp.dot(xb_ref[...], wu_ref[...], preferred_element_type=F32)
    u = jnp.maximum(u, 0.0)
    acc_ref[...] += jnp.dot((u * u).astype(BF16), wd_ref[...], preferred_element_type=F32)

    @pl.when(j == pl.num_programs(1) - 1)
    def _():
        y_ref[...] = _layer_norm(alpha * x_ref[...] + acc_ref[...], g_ref[...], b_ref[...])


def _ffn(x2, w_up, w_down, ln_g, ln_b, alpha, tm_pref):
    m, d = x2.shape
    f = w_up.shape[1]
    tm = _tile(m, tm_pref)
    tf = _tile(f, 512)
    const = lambda i, j: (0, 0)
    return pl.pallas_call(
        functools.partial(_ffn_kernel, alpha=alpha),
        grid=(m // tm, f // tf),
        in_specs=[pl.BlockSpec((tm, d), lambda i, j: (i, 0)),
                  pl.BlockSpec((d, tf), lambda i, j: (0, j)),
                  pl.BlockSpec((tf, d), lambda i, j: (j, 0)),
                  pl.BlockSpec((1, d), const),
                  pl.BlockSpec((1, d), const)],
        out_specs=pl.BlockSpec((tm, d), lambda i, j: (i, 0)),
        out_shape=jax.ShapeDtypeStruct((m, d), F32),
        scratch_shapes=[pltpu.VMEM((tm, d), BF16), pltpu.VMEM((tm, d), F32)],
        compiler_params=_cparams(("parallel", "arbitrary")),
        name="ffn_ln",
    )(x2, w_up, w_down, ln_g, ln_b)


def _layer(x, past, wts, dims, tiles):
    h_a, dk, dv, h_b, dh = dims
    w_main, w_gate, b_gate, g_head, w_out_a, w_out_b, ln1_g, ln1_b, w_up, w_down, ln2_g, ln2_b, alpha = wts
    b, l, d = x.shape
    m = b * l
    d_b = h_b * dh
    x2 = x.reshape(m, d)

    slab, k_new, v_new = _inproj(x2, w_main, d_b, tiles["tm_in"])
    slab3 = slab.reshape(b, l, slab.shape[1])
    n_rows = 2 * h_a + h_b
    lf, gc, gr = _gates(x, w_gate, b_gate, h_a, n_rows)
    gr4 = gr.reshape(b, n_rows, 1, l)
    q_col = (2 * h_a * dk + 2 * h_a * dv) // dh

    if past is None:
        c0 = jnp.zeros((b, h_a, dv, dk), F32)
        n0 = jnp.zeros((b, h_a, dk), F32)
        m0 = jnp.zeros((b, h_a), F32)
        hb = _fox_prompt(slab3, gc, gr4, h_b, dh, q_col, 2 * h_a, tiles["tq"])
    else:
        ck, cv, clf, c0, n0, m0 = past
        p = ck.shape[1]
        clf4 = jnp.transpose(clf, (0, 2, 1)).reshape(b, h_b, 1, p)
        hb = _fox_sample(slab3, ck.reshape(b, p, d_b), cv.reshape(b, p, d_b), clf4, gc, gr4,
                         h_b, dh, q_col, 2 * h_a)
    ha, c, n, mm = _mlstm(slab3, gc, gr, g_head, c0, n0, m0, h_a, dk, dv, tiles["lc"])

    x1 = _outproj(ha.reshape(m, -1), hb.reshape(m, -1), x2, w_out_a, w_out_b, ln1_g, ln1_b, alpha, tiles["tm_out"])
    y = _ffn(x1, w_up, w_down, ln2_g, ln2_b, alpha, tiles["tm_ffn"])
    lfb = lf[:, :, 2 * h_a:n_rows]
    state = (k_new.reshape(b, l, h_b, dh), v_new.reshape(b, l, h_b, dh), lfb, c, n, mm)
    return y.reshape(b, l, d), state


def kernel(x_prompt, x_sample, cache_fox_k, cache_fox_v, cache_fox_logf, state_mlstm_c, state_mlstm_n,
           state_mlstm_m, w_in, b_gates, g_mlstm, w_out, ln1_g, ln1_b, w_up, w_down, ln2_g, ln2_b):
    depth, d_model, _ = w_in.shape
    _, _, _, h_b, dh = cache_fox_k.shape
    _, _, h_a, dv, dk = state_mlstm_c.shape
    d_a = h_a * dv
    d_b = h_b * dh
    dims = (h_a, dk, dv, h_b, dh)
    alpha = (2 * depth) ** 0.25
    widths = (h_a * dk, h_a * dk, d_a, d_a, h_a, h_a, d_b, d_b, d_b, h_b)
    offs = [0]
    for w in widths:
        offs.append(offs[-1] + w)
    n_gates = 2 * h_a + h_b

    tiles_p = dict(tm_in=1024, tq=512, lc=256, tm_out=512, tm_ffn=512)
    tiles_s = dict(tm_in=1024, tq=512, lc=256, tm_out=512, tm_ffn=512)

    yp, ys = x_prompt, x_sample
    new_p, new_s = [], []
    for layer in range(depth):
        wl = w_in[layer]
        w_main = jnp.concatenate([wl[:, offs[0]:offs[4]], wl[:, offs[6]:offs[9]]], axis=1).astype(BF16)
        w_gate = jnp.concatenate([wl[:, offs[4]:offs[6]], wl[:, offs[9]:offs[10]]], axis=1)
        w_gate = jnp.pad(w_gate, ((0, 0), (0, GATE_COLS - n_gates)))
        b_gate = jnp.pad(b_gates[layer], (0, GATE_COLS - n_gates)).reshape(1, GATE_COLS)
        wo = w_out[layer].astype(BF16)
        wts = (w_main, w_gate, b_gate, g_mlstm[layer].reshape(1, d_a), wo[:d_a], wo[d_a:],
               ln1_g[layer].reshape(1, d_model), ln1_b[layer].reshape(1, d_model),
               w_up[layer].astype(BF16), w_down[layer].astype(BF16),
               ln2_g[layer].reshape(1, d_model), ln2_b[layer].reshape(1, d_model), alpha)
        yp, st_p = _layer(yp, None, wts, dims, tiles_p)
        past = (cache_fox_k[layer], cache_fox_v[layer], cache_fox_logf[layer],
                state_mlstm_c[layer], state_mlstm_n[layer], state_mlstm_m[layer])
        ys, st_s = _layer(ys, past, wts, dims, tiles_s)
        new_p.append(st_p)
        new_s.append(st_s)
    stack = lambda states, i: jnp.stack([s[i] for s in states], axis=0)
    return (yp, ys) + tuple(stack(new_p, i) for i in range(6)) + tuple(stack(new_s, i) for i in range(6))
```

```python
import functools
import math

import jax
import jax.numpy as jnp
from jax import lax
from jax.experimental import pallas as pl
from jax.experimental.pallas import tpu as pltpu

F32 = jnp.float32
BF16 = jnp.bfloat16

LN_EPS = 1e-5
HEAD_NORM_EPS = 1e-6
LOG2E = math.log2(math.e)
LANES = 128
GATE_COLS = LANES
VMEM_LIMIT = 56 * 1024 * 1024


def _cparams(sem):
    return pltpu.CompilerParams(dimension_semantics=sem, vmem_limit_bytes=VMEM_LIMIT)


def _tile(n, pref):
    t = min(n, pref)
    assert n % t == 0, (n, pref)
    return t


def _log_sigmoid(g):
    return jnp.minimum(g, 0.0) - jnp.log1p(jnp.exp(-jnp.abs(g)))


def _dot_nt(a, b):
    return lax.dot_general(a, b, (((1,), (1,)), ((), ())), preferred_element_type=F32)


def _dot_tn(a, b):
    return lax.dot_general(a, b, (((0,), (0,)), ((), ())), preferred_element_type=F32)


def _layer_norm(z, g, b):
    mu = jnp.mean(z, axis=-1, keepdims=True)
    zc = z - mu
    var = jnp.mean(zc * zc, axis=-1, keepdims=True)
    return zc * lax.rsqrt(var + LN_EPS) * g + b


def _prefix_sum_rows(v):
    row = lax.broadcasted_iota(jnp.int32, v.shape, 0)
    s = 1
    while s < v.shape[0]:
        v = v + jnp.where(row >= s, pltpu.roll(v, s, 0), 0.0)
        s *= 2
    return v


def _split3(f):
    hi = f.astype(BF16).astype(F32)
    r = f - hi
    mid = r.astype(BF16).astype(F32)
    lo = (r - mid).astype(BF16).astype(F32)
    return hi, mid, lo


def _bias_lanes(f, query_side):
    hi, mid, lo = _split3(f)
    lane = lax.broadcasted_iota(jnp.int32, (f.shape[0], LANES), 1)
    if query_side:
        a = jnp.where(lane == 0, hi, jnp.where(lane == 1, mid, jnp.where(lane == 2, lo, jnp.where(lane < 6, 1.0, 0.0))))
    else:
        a = jnp.where(lane == 3, -hi, jnp.where(lane == 4, -mid, jnp.where(lane == 5, -lo, jnp.where(lane < 3, 1.0, 0.0))))
    return a.astype(BF16)


def _pick_lane(x, idx):
    lane = lax.broadcasted_iota(jnp.int32, x.shape, 1)
    return jnp.sum(jnp.where(lane == idx, x, 0.0), axis=1, keepdims=True)


def _inproj_kernel(*refs, jk0, jv0, aliased):
    x_ref, w_ref, wg_ref, cs_ref = refs[:4]
    slab_ref, k_ref, v_ref, g_ref, xb_ref = refs[-5:]
    del aliased
    j = pl.program_id(1)

    @pl.when(j == 0)
    def _():
        xb = x_ref[...].astype(BF16)
        xb_ref[...] = xb
        g_ref[...] = jnp.dot(xb, wg_ref[...], preferred_element_type=F32)

    acc = jnp.dot(xb_ref[...], w_ref[...], preferred_element_type=F32)
    slab_ref[...] = (acc * cs_ref[...]).astype(BF16)

    @pl.when((j >= jk0) & (j < jv0))
    def _():
        k_ref[...] = acc

    @pl.when(j >= jv0)
    def _():
        v_ref[...] = acc


def _inproj(x2, w_main, w_gate, col_scale, layer, d_b, kv_prev, tm_pref):
    m, d = x2.shape
    depth, _, n = w_main.shape
    tm = _tile(m, tm_pref)
    tn = _tile(d_b, 512)
    nj = n // tn
    nkv = d_b // tn
    jk0 = nj - 2 * nkv
    jv0 = nj - nkv
    aliased = kv_prev is not None
    in_specs = [pl.BlockSpec((tm, d), lambda i, j: (i, 0)),
                pl.BlockSpec((None, d, tn), lambda i, j: (layer, 0, j)),
                pl.BlockSpec((None, d, GATE_COLS), lambda i, j: (layer, 0, 0)),
                pl.BlockSpec((1, tn), lambda i, j: (0, j))]
    args = [x2, w_main, w_gate, col_scale]
    aliases = {}
    if aliased:
        in_specs += [pl.BlockSpec(memory_space=pl.ANY), pl.BlockSpec(memory_space=pl.ANY)]
        args += list(kv_prev)
        aliases = {4: 1, 5: 2}
    return pl.pallas_call(
        functools.partial(_inproj_kernel, jk0=jk0, jv0=jv0, aliased=aliased),
        grid=(m // tm, nj),
        in_specs=in_specs,
        out_specs=[pl.BlockSpec((tm, tn), lambda i, j: (i, j)),
                   pl.BlockSpec((None, tm, tn), lambda i, j: (layer, i, jnp.clip(j - jk0, 0, nkv - 1))),
                   pl.BlockSpec((None, tm, tn), lambda i, j: (layer, i, jnp.clip(j - jv0, 0, nkv - 1))),
                   pl.BlockSpec((tm, GATE_COLS), lambda i, j: (i, 0))],
        out_shape=[jax.ShapeDtypeStruct((m, n), BF16),
                   jax.ShapeDtypeStruct((depth, m, d_b), F32),
                   jax.ShapeDtypeStruct((depth, m, d_b), F32),
                   jax.ShapeDtypeStruct((m, GATE_COLS), F32)],
        scratch_shapes=[pltpu.VMEM((tm, d), BF16)],
        input_output_aliases=aliases,
        compiler_params=_cparams(("parallel", "arbitrary")),
        name="inproj",
    )(*args)


def _gates_kernel(gp_ref, bg_ref, lf_ref, gc_ref, gr_ref, carry_ref, *, n_raw, n_rows):
    t = pl.program_id(1)

    @pl.when(t == 0)
    def _():
        carry_ref[...] = jnp.zeros_like(carry_ref)

    g = gp_ref[...] + bg_ref[...]
    tl = g.shape[0]
    lane = lax.broadcasted_iota(jnp.int32, g.shape, 1)
    raw = lane < n_raw
    val = jnp.where(raw, g, _log_sigmoid(g))
    lf_ref[...] = val
    c = _prefix_sum_rows(val) + carry_ref[...]
    carry_ref[...] = c[tl - 1:tl, :]
    out = jnp.where(raw, g, c)
    gc_ref[...] = out
    gr_ref[...] = out.T[:n_rows, :]


def _gates(gpre3, b_gate, layer, n_raw, n_rows):
    b, l, _ = gpre3.shape
    tl = _tile(l, 512)
    return pl.pallas_call(
        functools.partial(_gates_kernel, n_raw=n_raw, n_rows=n_rows),
        grid=(b, l // tl),
        in_specs=[pl.BlockSpec((None, tl, GATE_COLS), lambda i, t: (i, t, 0)),
                  pl.BlockSpec((None, 1, GATE_COLS), lambda i, t: (layer, 0, 0))],
        out_specs=[pl.BlockSpec((None, tl, GATE_COLS), lambda i, t: (i, t, 0)),
                   pl.BlockSpec((None, tl, GATE_COLS), lambda i, t: (i, t, 0)),
                   pl.BlockSpec((None, n_rows, tl), lambda i, t: (i, 0, t))],
        out_shape=[jax.ShapeDtypeStruct((b, l, GATE_COLS), F32),
                   jax.ShapeDtypeStruct((b, l, GATE_COLS), F32),
                   jax.ShapeDtypeStruct((b, n_rows, l), F32)],
        scratch_shapes=[pltpu.VMEM((1, GATE_COLS), F32)],
        compiler_params=_cparams(("parallel", "arbitrary")),
        name="gates",
    )(gpre3, b_gate)


def _mlstm_kernel(q_ref, k_ref, v_ref, o_ref, gc_ref, gr_ref, gh_ref, c0_ref, n0_ref, m0_ref,
                  h_ref, c_ref, n_ref, m_ref, ct_s, n_s, m_s, fp_s, *, n_heads, dk, dv):
    c_idx = pl.program_id(1)
    n_chunks = pl.num_programs(1)
    lc = q_ref.shape[0]

    @pl.when(c_idx == 0)
    def _():
        for h in range(n_heads):
            ct_s[h] = c0_ref[h].T
        n_s[...] = n0_ref[...]
        m_s[...] = m0_ref[...]
        fp_s[...] = jnp.zeros_like(fp_s)

    row = lax.broadcasted_iota(jnp.int32, (lc, lc), 0)
    col = lax.broadcasted_iota(jnp.int32, (lc, lc), 1)
    causal = col <= row
    gc = gc_ref[...]
    gr = gr_ref[...]
    fp = fp_s[...]
    m_all = m_s[...]
    for h in range(n_heads):
        q = q_ref[:, h * dk:(h + 1) * dk]
        k = k_ref[:, h * dk:(h + 1) * dk]
        v = v_ref[:, h * dv:(h + 1) * dv]
        f0 = fp[:, n_heads + h:n_heads + h + 1]
        ig_row = gr[h:h + 1, :]
        ig_col = gc[:, h:h + 1]
        b_row = gr[n_heads + h:n_heads + h + 1, :] - f0
        b_col = gc[:, n_heads + h:n_heads + h + 1] - f0
        m_prev = m_all[:, h:h + 1]
        ct = ct_s[h]
        nvec = n_s[h:h + 1, :]

        d = jnp.where(causal, b_col - b_row + ig_row, -jnp.inf)
        g = b_col + m_prev
        m_t = jnp.maximum(g, jnp.max(d, axis=1, keepdims=True))
        w_inter = jnp.exp(g - m_t)
        s = _dot_nt(q, k) * jnp.exp(d - m_t)
        inter = jnp.dot(q, ct.astype(BF16), preferred_element_type=F32)
        num = w_inter * inter + jnp.dot(s.astype(BF16), v, preferred_element_type=F32)
        qn = jnp.sum(q.astype(F32) * nvec, axis=1, keepdims=True)
        den = w_inter * qn + jnp.sum(s, axis=1, keepdims=True)
        hh = num / jnp.maximum(jnp.abs(den), jnp.exp(-m_t))

        m_new = m_t[lc - 1:lc, :]
        b_last = b_col[lc - 1:lc, :]
        a_col = jnp.exp(b_last - b_col + ig_col - m_new)
        decay = jnp.exp(b_last + m_prev - m_new)
        ka = k.astype(F32) * a_col
        ct_s[h] = decay * ct + _dot_tn(ka.astype(BF16), v)
        n_s[h:h + 1, :] = decay * nvec + jnp.sum(ka, axis=0, keepdims=True)
        m_s[:, h:h + 1] = m_new

        hn = hh * lax.rsqrt(jnp.mean(hh * hh, axis=1, keepdims=True) + HEAD_NORM_EPS)
        og = o_ref[:, h * dv:(h + 1) * dv].astype(F32)
        hn = hn * gh_ref[:, h * dv:(h + 1) * dv] * jax.nn.sigmoid(og)
        h_ref[:, h * dv:(h + 1) * dv] = hn.astype(h_ref.dtype)
    fp_s[...] = gc[lc - 1:lc, :]

    @pl.when(c_idx == n_chunks - 1)
    def _():
        for h in range(n_heads):
            c_ref[h] = ct_s[h].T
        n_ref[...] = n_s[...]
        m_ref[...] = m_s[...]


def _mlstm(slab3, gc, gr, g_head, layer, c0, n0, m0, n_heads, dk, dv, lc_pref):
    b, l, _ = slab3.shape
    lc = _tile(l, lc_pref)
    dq = n_heads * dk
    da = n_heads * dv
    assert da == 2 * dq
    m0r = m0.reshape(b, 1, n_heads)
    bmap = lambda i, c: (i, 0, 0)
    outs = pl.pallas_call(
        functools.partial(_mlstm_kernel, n_heads=n_heads, dk=dk, dv=dv),
        grid=(b, l // lc),
        in_specs=[pl.BlockSpec((None, lc, dq), lambda i, c: (i, c, 0)),
                  pl.BlockSpec((None, lc, dq), lambda i, c: (i, c, 1)),
                  pl.BlockSpec((None, lc, da), lambda i, c: (i, c, 1)),
                  pl.BlockSpec((None, lc, da), lambda i, c: (i, c, 2)),
                  pl.BlockSpec((None, lc, GATE_COLS), lambda i, c: (i, c, 0)),
                  pl.BlockSpec((None, gr.shape[1], lc), lambda i, c: (i, 0, c)),
                  pl.BlockSpec((None, 1, da), lambda i, c: (layer, 0, 0)),
                  pl.BlockSpec((None, n_heads, dv, dk), lambda i, c: (i, 0, 0, 0)),
                  pl.BlockSpec((None, n_heads, dk), bmap),
                  pl.BlockSpec((None, 1, n_heads), bmap)],
        out_specs=[pl.BlockSpec((None, lc, da), lambda i, c: (i, c, 0)),
                   pl.BlockSpec((None, n_heads, dv, dk), lambda i, c: (i, 0, 0, 0)),
                   pl.BlockSpec((None, n_heads, dk), bmap),
                   pl.BlockSpec((None, 1, n_heads), bmap)],
        out_shape=[jax.ShapeDtypeStruct((b, l, da), BF16),
                   jax.ShapeDtypeStruct((b, n_heads, dv, dk), F32),
                   jax.ShapeDtypeStruct((b, n_heads, dk), F32),
                   jax.ShapeDtypeStruct((b, 1, n_heads), F32)],
        scratch_shapes=[pltpu.VMEM((n_heads, dk, dv), F32),
                        pltpu.VMEM((n_heads, dk), F32),
                        pltpu.VMEM((1, n_heads), F32),
                        pltpu.VMEM((1, GATE_COLS), F32)],
        compiler_params=_cparams(("parallel", "arbitrary")),
        name="mlstm",
    )(slab3, slab3, slab3, slab3, gc, gr, g_head, c0, n0, m0r)
    h_a, c, n, m = outs
    return h_a, c, n, m.reshape(b, n_heads)


def _fox_prompt_kernel(q_ref, k_ref, v_ref, gcq_ref, gck_ref, o_ref, qa_s, ka_s, va_s, p_s, m_s, al_s, acc_s,
                       *, lane0, tq, hp):
    hg = pl.program_id(1)
    i = pl.program_id(2)
    dh = q_ref.shape[1] // hp

    @pl.when(i == 0)
    def _():
        gck = gck_ref[...]
        lane = lax.broadcasted_iota(jnp.int32, (k_ref.shape[0], dh), 1)
        ones_lane = jnp.where(lane == 0, 1.0, 0.0).astype(BF16)
        for u in range(hp):
            ka_s[u, :, :dh] = k_ref[:, u * dh:(u + 1) * dh]
            ka_s[u, :, dh:] = _bias_lanes(_pick_lane(gck, lane0 + hg * hp + u) * LOG2E, False)
            va_s[u, :, :dh] = v_ref[:, u * dh:(u + 1) * dh]
            va_s[u, :, dh:] = ones_lane

    gcq = gcq_ref[...]
    for u in range(hp):
        qa_s[u, :, :dh] = q_ref[:, u * dh:(u + 1) * dh]
        qa_s[u, :, dh:] = _bias_lanes(_pick_lane(gcq, lane0 + hg * hp + u) * LOG2E, True)

    def logits(u, j):
        start = pl.multiple_of(j * tq, tq)
        return _dot_nt(qa_s[u], ka_s[u, pl.ds(start, tq), :])

    def softmax(u, s, diagonal):
        if diagonal:
            row = lax.broadcasted_iota(jnp.int32, s.shape, 0)
            col = lax.broadcasted_iota(jnp.int32, s.shape, 1)
            s = jnp.where(col <= row, s, -jnp.inf)
            m_new = jnp.broadcast_to(jnp.max(s, axis=1, keepdims=True), (tq, LANES))
            al_s[u] = jnp.zeros_like(m_new)
        else:
            m = m_s[u]
            m_new = jnp.maximum(m, jnp.max(s, axis=1, keepdims=True))
            al_s[u] = jnp.exp2(m - m_new)
        m_s[u] = m_new
        p_s[u] = jnp.exp2(s - pltpu.repeat(m_new, tq // LANES, axis=1)).astype(BF16)

    def values(u, j):
        start = pl.multiple_of(j * tq, tq)
        pv = jnp.dot(p_s[u], va_s[u, pl.ds(start, tq), :], preferred_element_type=F32)
        return pltpu.repeat(al_s[u], 2 * dh // LANES, axis=1) * acc_s[u] + pv

    for u in range(hp):
        acc_s[u] = jnp.zeros(acc_s.shape[1:], F32)
        softmax(u, logits(u, i), True)

    def trip(j, _):
        prev = jnp.where(j == 0, i, j - 1)
        s = [logits(u, j) for u in range(hp)]
        for u in range(hp):
            acc_s[u] = values(u, prev)
        for u in range(hp):
            softmax(u, s[u], False)
        return 0

    lax.fori_loop(0, i, trip, 0)
    last = jnp.where(i == 0, i, i - 1)
    for u in range(hp):
        acc = values(u, last)
        o_ref[:, u * dh:(u + 1) * dh] = (acc[:, :dh] / acc[:, dh:dh + 1]).astype(o_ref.dtype)


def _fox_prompt(slab3, gc, n_heads, dh, q_col, lane0, tq_pref, hp):
    b, l, _ = slab3.shape
    tq = _tile(l, tq_pref)
    assert n_heads % hp == 0 and q_col % hp == 0
    qb, ng, w = q_col // hp, n_heads // hp, hp * dh
    return pl.pallas_call(
        functools.partial(_fox_prompt_kernel, lane0=lane0, tq=tq, hp=hp),
        grid=(b, ng, l // tq),
        in_specs=[pl.BlockSpec((None, tq, w), lambda bi, h, i: (bi, i, qb + h)),
                  pl.BlockSpec((None, l, w), lambda bi, h, i: (bi, 0, qb + ng + h), pipeline_mode=pl.Buffered(1)),
                  pl.BlockSpec((None, l, w), lambda bi, h, i: (bi, 0, qb + 2 * ng + h), pipeline_mode=pl.Buffered(1)),
                  pl.BlockSpec((None, tq, GATE_COLS), lambda bi, h, i: (bi, i, 0)),
                  pl.BlockSpec((None, l, GATE_COLS), lambda bi, h, i: (bi, 0, 0), pipeline_mode=pl.Buffered(1))],
        out_specs=pl.BlockSpec((None, tq, w), lambda bi, h, i: (bi, i, h)),
        out_shape=jax.ShapeDtypeStruct((b, l, n_heads * dh), BF16),
        scratch_shapes=[pltpu.VMEM((hp, tq, 2 * dh), BF16), pltpu.VMEM((hp, l, 2 * dh), BF16),
                        pltpu.VMEM((hp, l, 2 * dh), BF16), pltpu.VMEM((hp, tq, tq), BF16),
                        pltpu.VMEM((hp, tq, LANES), F32), pltpu.VMEM((hp, tq, LANES), F32),
                        pltpu.VMEM((hp, tq, 2 * dh), F32)],
        compiler_params=_cparams(("parallel", "parallel", "arbitrary")),
        name="fox_prompt",
    )(slab3, slab3, slab3, gc, gc)


def _fox_sample_kernel(q_ref, k_ref, v_ref, ck_ref, cv_ref, clf_ref, gc_ref, o_ref, *, n_heads, dh, lane0):
    l = q_ref.shape[0]
    cum = _prefix_sum_rows(clf_ref[...])
    past = cum.shape[0]
    gc = gc_ref[...]
    row = lax.broadcasted_iota(jnp.int32, (l, l), 0)
    col = lax.broadcasted_iota(jnp.int32, (l, l), 1)
    for h in range(n_heads):
        f_old = cum[:, h:h + 1]
        f_new = f_old[past - 1:past, :] + gc[:, lane0 + h:lane0 + h + 1]
        bias_new_k = _bias_lanes(f_new * LOG2E, False)
        qa = jnp.concatenate([q_ref[:, h * dh:(h + 1) * dh], _bias_lanes(f_new * LOG2E, True)], axis=1)
        ka_new = jnp.concatenate([k_ref[:, h * dh:(h + 1) * dh], bias_new_k], axis=1)
        ka_old = jnp.concatenate([ck_ref[:, h, :].astype(BF16), _bias_lanes(f_old * LOG2E, False)], axis=1)
        s_old = _dot_nt(qa, ka_old)
        s_new = jnp.where(col <= row, _dot_nt(qa, ka_new), -jnp.inf)
        m = jnp.maximum(jnp.max(s_old, axis=1, keepdims=True), jnp.max(s_new, axis=1, keepdims=True))
        p_old = jnp.exp2(s_old - m)
        p_new = jnp.exp2(s_new - m)
        den = jnp.sum(p_old, axis=1, keepdims=True) + jnp.sum(p_new, axis=1, keepdims=True)
        acc = (jnp.dot(p_old.astype(BF16), cv_ref[:, h, :].astype(BF16), preferred_element_type=F32)
               + jnp.dot(p_new.astype(BF16), v_ref[:, h * dh:(h + 1) * dh], preferred_element_type=F32))
        o_ref[:, h * dh:(h + 1) * dh] = (acc / den).astype(o_ref.dtype)


def _fox_sample(slab3, ck, cv, clf, layer, gc, n_heads, dh, q_col, lane0):
    b, l, _ = slab3.shape
    past = ck.shape[2]
    dd = n_heads * dh
    qb = q_col // n_heads
    return pl.pallas_call(
        functools.partial(_fox_sample_kernel, n_heads=n_heads, dh=dh, lane0=lane0),
        grid=(b,),
        in_specs=[pl.BlockSpec((None, l, dd), lambda bi: (bi, 0, qb)),
                  pl.BlockSpec((None, l, dd), lambda bi: (bi, 0, qb + 1)),
                  pl.BlockSpec((None, l, dd), lambda bi: (bi, 0, qb + 2)),
                  pl.BlockSpec((None, None, past, n_heads, dh), lambda bi: (layer, bi, 0, 0, 0)),
                  pl.BlockSpec((None, None, past, n_heads, dh), lambda bi: (layer, bi, 0, 0, 0)),
                  pl.BlockSpec((None, None, past, n_heads), lambda bi: (layer, bi, 0, 0)),
                  pl.BlockSpec((None, l, GATE_COLS), lambda bi: (bi, 0, 0))],
        out_specs=pl.BlockSpec((None, l, dd), lambda bi: (bi, 0, 0)),
        out_shape=jax.ShapeDtypeStruct((b, l, dd), BF16),
        compiler_params=_cparams(("parallel",)),
        name="fox_sample",
    )(slab3, slab3, slab3, ck, cv, clf, gc)


def _outproj_kernel(ha_ref, hb_ref, x_ref, wa_ref, wb_ref, g_ref, b_ref, y_ref, *, alpha):
    tm = x_ref.shape[0]
    tr = min(tm, 256)
    for r in range(tm // tr):
        rows = slice(r * tr, (r + 1) * tr)
        mix = (jnp.dot(ha_ref[rows, :], wa_ref[...], preferred_element_type=F32)
               + jnp.dot(hb_ref[rows, :], wb_ref[...], preferred_element_type=F32))
        y_ref[rows, :] = _layer_norm(alpha * x_ref[rows, :] + mix, g_ref[...], b_ref[...])


def _outproj(ha2, hb2, x2, w_out, ln_g, ln_b, layer, alpha, tm_pref):
    m, d = x2.shape
    da = ha2.shape[1]
    db = hb2.shape[1]
    assert da == db
    tm = _tile(m, tm_pref)
    vec = pl.BlockSpec((None, 1, d), lambda i: (layer, 0, 0))
    return pl.pallas_call(
        functools.partial(_outproj_kernel, alpha=alpha),
        grid=(m // tm,),
        in_specs=[pl.BlockSpec((tm, da), lambda i: (i, 0)),
                  pl.BlockSpec((tm, db), lambda i: (i, 0)),
                  pl.BlockSpec((tm, d), lambda i: (i, 0)),
                  pl.BlockSpec((None, da, d), lambda i: (layer, 0, 0)),
                  pl.BlockSpec((None, db, d), lambda i: (layer, 1, 0)),
                  vec, vec],
        out_specs=pl.BlockSpec((tm, d), lambda i: (i, 0)),
        out_shape=jax.ShapeDtypeStruct((m, d), F32),
        compiler_params=_cparams(("parallel",)),
        name="outproj_ln",
    )(ha2, hb2, x2, w_out, w_out, ln_g, ln_b)


def _ffn_kernel(x_ref, wu_ref, wd_ref, g_ref, b_ref, y_ref, xb_ref, acc_ref, *, alpha):
    j = pl.program_id(1)

    @pl.when(j == 0)
    def _():
        xb_ref[...] = x_ref[...].astype(BF16)
        acc_ref[...] = jnp.zeros_like(acc_ref)

    u = jnp.dot(xb_ref[...], wu_ref[...], preferred_element_type=F32)
    u = jnp.maximum(u, 0.0)
    acc_ref[...] += jnp.dot((u * u).astype(BF16), wd_ref[...], preferred_element_type=F32)

    @pl.when(j == pl.num_programs(1) - 1)
    def _():
        y_ref[...] = _layer_norm(alpha * x_ref[...] + acc_ref[...], g_ref[...], b_ref[...])


def _ffn(x2, w_up, w_down, ln_g, ln_b, layer, alpha, tm_pref, tf_pref):
    m, d = x2.shape
    f = w_up.shape[2]
    tm = _tile(m, tm_pref)
    tf = _tile(f, tf_pref)
    vec = pl.BlockSpec((None, 1, d), lambda i, j: (layer, 0, 0))
    return pl.pallas_call(
        functools.partial(_ffn_kernel, alpha=alpha),
        grid=(m // tm, f // tf),
        in_specs=[pl.BlockSpec((tm, d), lambda i, j: (i, 0)),
                  pl.BlockSpec((None, d, tf), lambda i, j: (layer, 0, j)),
                  pl.BlockSpec((None, tf, d), lambda i, j: (layer, j, 0)),
                  vec, vec],
        out_specs=pl.BlockSpec((tm, d), lambda i, j: (i, 0)),
        out_shape=jax.ShapeDtypeStruct((m, d), F32),
        scratch_shapes=[pltpu.VMEM((tm, d), BF16), pltpu.VMEM((tm, d), F32)],
        compiler_params=_cparams(("parallel", "arbitrary")),
        name="ffn_ln",
    )(x2, w_up, w_down, ln_g, ln_b)


def _layer(x, layer, cache, kv_prev, wts, dims, tiles):
    h_a, dk, dv, h_b, dh = dims
    (w_main, w_gate, col_scale, b_gate, g_head, w_out, ln1_g, ln1_b, w_up, w_down, ln2_g, ln2_b, alpha) = wts
    b, l, d = x.shape
    m = b * l
    d_b = h_b * dh
    x2 = x.reshape(m, d)

    slab, k_all, v_all, gpre = _inproj(x2, w_main, w_gate, col_scale, layer, d_b, kv_prev, tiles["tm_in"])
    slab3 = slab.reshape(b, l, slab.shape[1])
    n_rows = 2 * h_a + h_b
    lf, gc, gr = _gates(gpre.reshape(b, l, GATE_COLS), b_gate, layer, h_a, n_rows)
    q_col = (2 * h_a * dk + 2 * h_a * dv) // dh

    if cache is None:
        c0 = jnp.zeros((b, h_a, dv, dk), F32)
        n0 = jnp.zeros((b, h_a, dk), F32)
        m0 = jnp.zeros((b, h_a), F32)
        hb = _fox_prompt(slab3, gc, h_b, dh, q_col, 2 * h_a, tiles["tq"], tiles["fox_heads"])
    else:
        ck, cv, clf, c_all, n_all, m_all = cache
        c0, n0, m0 = c_all[layer], n_all[layer], m_all[layer]
        hb = _fox_sample(slab3, ck, cv, clf, layer, gc, h_b, dh, q_col, 2 * h_a)
    ha, c, n, mm = _mlstm(slab3, gc, gr, g_head, layer, c0, n0, m0, h_a, dk, dv, tiles["lc"])

    x1 = _outproj(ha.reshape(m, -1), hb.reshape(m, -1), x2, w_out, ln1_g, ln1_b, layer, alpha, tiles["tm_out"])
    y = _ffn(x1, w_up, w_down, ln2_g, ln2_b, layer, alpha, tiles["tm_ffn"], tiles["tf_ffn"])
    lfb = lf[:, :, 2 * h_a:n_rows]
    return y.reshape(b, l, d), (k_all, v_all), (lfb, c, n, mm)


def kernel(x_prompt, x_sample, cache_fox_k, cache_fox_v, cache_fox_logf, state_mlstm_c, state_mlstm_n,
           state_mlstm_m, w_in, b_gates, g_mlstm, w_out, ln1_g, ln1_b, w_up, w_down, ln2_g, ln2_b):
    depth, d_model, _ = w_in.shape
    _, _, _, h_b, dh = cache_fox_k.shape
    _, _, h_a, dv, dk = state_mlstm_c.shape
    d_a = h_a * dv
    d_b = h_b * dh
    dims = (h_a, dk, dv, h_b, dh)
    alpha = (2 * depth) ** 0.25
    widths = (h_a * dk, h_a * dk, d_a, d_a, h_a, h_a, d_b, d_b, d_b, h_b)
    offs = [0]
    for w in widths:
        offs.append(offs[-1] + w)
    n_gates = 2 * h_a + h_b

    w_main = jnp.concatenate([w_in[:, :, offs[0]:offs[4]], w_in[:, :, offs[6]:offs[9]]], axis=2).astype(BF16)
    w_gate = jnp.concatenate([w_in[:, :, offs[4]:offs[6]], w_in[:, :, offs[9]:offs[10]]], axis=2)
    w_gate = jnp.pad(w_gate, ((0, 0), (0, 0), (0, GATE_COLS - n_gates))).astype(BF16)
    b_gate = jnp.pad(b_gates, ((0, 0), (0, GATE_COLS - n_gates))).reshape(depth, 1, GATE_COLS)
    n_main = w_main.shape[2]
    col_scale = jnp.concatenate([jnp.full((h_a * dk,), dk ** -0.5, F32),
                                 jnp.ones((offs[4] - offs[1],), F32),
                                 jnp.full((d_b,), dh ** -0.5 * LOG2E, F32),
                                 jnp.ones((2 * d_b,), F32)]).reshape(1, n_main)
    wts = (w_main, w_gate, col_scale, b_gate, g_mlstm.reshape(depth, 1, d_a), w_out.astype(BF16),
           ln1_g.reshape(depth, 1, d_model), ln1_b.reshape(depth, 1, d_model),
           w_up.astype(BF16), w_down.astype(BF16),
           ln2_g.reshape(depth, 1, d_model), ln2_b.reshape(depth, 1, d_model), alpha)
    cache = (cache_fox_k, cache_fox_v, cache_fox_logf, state_mlstm_c, state_mlstm_n, state_mlstm_m)

    tiles = dict(tm_in=1024, tq=1024, fox_heads=2, lc=256, tm_out=512, tm_ffn=512, tf_ffn=1024)

    yp, ys = x_prompt, x_sample
    kv_p = kv_s = None
    small_p, small_s = [], []
    for layer in range(depth):
        yp, kv_p, st_p = _layer(yp, layer, None, kv_p, wts, dims, tiles)
        ys, kv_s, st_s = _layer(ys, layer, cache, kv_s, wts, dims, tiles)
        small_p.append(st_p)
        small_s.append(st_s)
    stack = lambda states, i: jnp.stack([s[i] for s in states], axis=0)
    bp, lp, _ = x_prompt.shape
    bs, ls, _ = x_sample.shape
    return ((yp, ys)
            + tuple(a.reshape(depth, bp, lp, h_b, dh) for a in kv_p) + tuple(stack(small_p, i) for i in range(4))
            + tuple(a.reshape(depth, bs, ls, h_b, dh) for a in kv_s) + tuple(stack(small_s, i) for i in range(4)))
```

```python
import functools
import math

import jax
import jax.numpy as jnp
from jax import lax
from jax.experimental import pallas as pl
from jax.experimental.pallas import tpu as pltpu

F32 = jnp.float32
BF16 = jnp.bfloat16

LN_EPS = 1e-5
HEAD_NORM_EPS = 1e-6
LOG2E = math.log2(math.e)
LANES = 128
GATE_COLS = LANES
VMEM_LIMIT = 56 * 1024 * 1024


def _cparams(sem):
    return pltpu.CompilerParams(dimension_semantics=sem, vmem_limit_bytes=VMEM_LIMIT)


def _tile(n, pref):
    t = min(n, pref)
    assert n % t == 0, (n, pref)
    return t


def _log_sigmoid(g):
    return jnp.minimum(g, 0.0) - jnp.log1p(jnp.exp(-jnp.abs(g)))


def _dot_nt(a, b):
    return lax.dot_general(a, b, (((1,), (1,)), ((), ())), preferred_element_type=F32)


def _dot_tn(a, b):
    return lax.dot_general(a, b, (((0,), (0,)), ((), ())), preferred_element_type=F32)


def _layer_norm(z, g, b):
    mu = jnp.mean(z, axis=-1, keepdims=True)
    zc = z - mu
    var = jnp.mean(zc * zc, axis=-1, keepdims=True)
    return zc * lax.rsqrt(var + LN_EPS) * g + b


def _prefix_sum_rows(v):
    row = lax.broadcasted_iota(jnp.int32, v.shape, 0)
    s = 1
    while s < v.shape[0]:
        v = v + jnp.where(row >= s, pltpu.roll(v, s, 0), 0.0)
        s *= 2
    return v


def _split3(f):
    hi = f.astype(BF16).astype(F32)
    r = f - hi
    mid = r.astype(BF16).astype(F32)
    lo = (r - mid).astype(BF16).astype(F32)
    return hi, mid, lo


def _bias_lanes(f, query_side):
    hi, mid, lo = _split3(f)
    lane = lax.broadcasted_iota(jnp.int32, (f.shape[0], LANES), 1)
    if query_side:
        a = jnp.where(lane == 0, hi, jnp.where(lane == 1, mid, jnp.where(lane == 2, lo, jnp.where(lane < 6, 1.0, 0.0))))
    else:
        a = jnp.where(lane == 3, -hi, jnp.where(lane == 4, -mid, jnp.where(lane == 5, -lo, jnp.where(lane < 3, 1.0, 0.0))))
    return a.astype(BF16)


def _pick_lane(x, idx):
    lane = lax.broadcasted_iota(jnp.int32, x.shape, 1)
    return jnp.sum(jnp.where(lane == idx, x, 0.0), axis=1, keepdims=True)


def _win_prep_kernel(w_ref, main_ref, gate_ref, *, main_cols, gate_cols):
    w = w_ref[...]
    main_ref[...] = jnp.concatenate([w[:, a:b] for a, b in main_cols], axis=1).astype(BF16)
    gates = [w[:, a:b] for a, b in gate_cols]
    pad = GATE_COLS - sum(b - a for a, b in gate_cols)
    gate_ref[...] = jnp.concatenate(gates + [jnp.zeros((w.shape[0], pad), F32)], axis=1).astype(BF16)


def _win_prep(w_in, main_cols, gate_cols):
    depth, d, n_in = w_in.shape
    n_main = sum(b - a for a, b in main_cols)
    tk = _tile(d, 256)
    return pl.pallas_call(
        functools.partial(_win_prep_kernel, main_cols=main_cols, gate_cols=gate_cols),
        grid=(depth, d // tk),
        in_specs=[pl.BlockSpec((None, tk, n_in), lambda l, t: (l, t, 0))],
        out_specs=[pl.BlockSpec((None, tk, n_main), lambda l, t: (l, t, 0)),
                   pl.BlockSpec((None, tk, GATE_COLS), lambda l, t: (l, t, 0))],
        out_shape=[jax.ShapeDtypeStruct((depth, d, n_main), BF16),
                   jax.ShapeDtypeStruct((depth, d, GATE_COLS), BF16)],
        compiler_params=_cparams(("parallel", "parallel")),
        name="win_prep",
    )(w_in)


def _inproj_kernel(*refs, jk0, jv0, aliased):
    x_ref, w_ref, wg_ref, cs_ref = refs[:4]
    slab_ref, k_ref, v_ref, g_ref, xb_ref = refs[-5:]
    del aliased
    j = pl.program_id(1)

    @pl.when(j == 0)
    def _():
        xb = x_ref[...].astype(BF16)
        xb_ref[...] = xb
        g_ref[...] = jnp.dot(xb, wg_ref[...], preferred_element_type=F32)

    def tile(f32_ref):
        acc = jnp.dot(xb_ref[...], w_ref[...], preferred_element_type=F32)
        slab_ref[...] = (acc * cs_ref[...]).astype(BF16)
        if f32_ref is not None:
            f32_ref[...] = acc

    pl.when(j < jk0)(lambda: tile(None))
    pl.when((j >= jk0) & (j < jv0))(lambda: tile(k_ref))
    pl.when(j >= jv0)(lambda: tile(v_ref))


def _inproj(x2, w_main, w_gate, col_scale, layer, d_b, kv_prev, tm_pref):
    m, d = x2.shape
    depth, _, n = w_main.shape
    tm = _tile(m, tm_pref)
    tn = _tile(d_b, 512)
    nj = n // tn
    nkv = d_b // tn
    jk0 = nj - 2 * nkv
    jv0 = nj - nkv
    aliased = kv_prev is not None
    in_specs = [pl.BlockSpec((tm, d), lambda i, j: (i, 0)),
                pl.BlockSpec((None, d, tn), lambda i, j: (layer, 0, j)),
                pl.BlockSpec((None, d, GATE_COLS), lambda i, j: (layer, 0, 0)),
                pl.BlockSpec((1, tn), lambda i, j: (0, j))]
    args = [x2, w_main, w_gate, col_scale]
    aliases = {}
    if aliased:
        in_specs += [pl.BlockSpec(memory_space=pl.ANY), pl.BlockSpec(memory_space=pl.ANY)]
        args += list(kv_prev)
        aliases = {4: 1, 5: 2}
    return pl.pallas_call(
        functools.partial(_inproj_kernel, jk0=jk0, jv0=jv0, aliased=aliased),
        grid=(m // tm, nj),
        in_specs=in_specs,
        out_specs=[pl.BlockSpec((tm, tn), lambda i, j: (i, j)),
                   pl.BlockSpec((None, tm, tn), lambda i, j: (layer, i, jnp.clip(j - jk0, 0, nkv - 1))),
                   pl.BlockSpec((None, tm, tn), lambda i, j: (layer, i, jnp.clip(j - jv0, 0, nkv - 1))),
                   pl.BlockSpec((tm, GATE_COLS), lambda i, j: (i, 0))],
        out_shape=[jax.ShapeDtypeStruct((m, n), BF16),
                   jax.ShapeDtypeStruct((depth, m, d_b), F32),
                   jax.ShapeDtypeStruct((depth, m, d_b), F32),
                   jax.ShapeDtypeStruct((m, GATE_COLS), F32)],
        scratch_shapes=[pltpu.VMEM((tm, d), BF16)],
        input_output_aliases=aliases,
        compiler_params=_cparams(("parallel", "arbitrary")),
        name="inproj",
    )(*args)


def _gates_kernel(gp_ref, bg_ref, lf_ref, gc_ref, gr_ref, carry_ref, *, n_raw, n_rows):
    t = pl.program_id(1)

    @pl.when(t == 0)
    def _():
        carry_ref[...] = jnp.zeros_like(carry_ref)

    g = gp_ref[...] + bg_ref[...]
    tl = g.shape[0]
    lane = lax.broadcasted_iota(jnp.int32, g.shape, 1)
    raw = lane < n_raw
    val = jnp.where(raw, g, _log_sigmoid(g))
    lf_ref[...] = val
    c = _prefix_sum_rows(val) + carry_ref[...]
    carry_ref[...] = c[tl - 1:tl, :]
    out = jnp.where(raw, g, c)
    gc_ref[...] = out
    gr_ref[...] = out.T[:n_rows, :]


def _gates(gpre3, b_gate, layer, n_raw, n_rows):
    b, l, _ = gpre3.shape
    tl = _tile(l, 512)
    return pl.pallas_call(
        functools.partial(_gates_kernel, n_raw=n_raw, n_rows=n_rows),
        grid=(b, l // tl),
        in_specs=[pl.BlockSpec((None, tl, GATE_COLS), lambda i, t: (i, t, 0)),
                  pl.BlockSpec((None, 1, GATE_COLS), lambda i, t: (layer, 0, 0))],
        out_specs=[pl.BlockSpec((None, tl, GATE_COLS), lambda i, t: (i, t, 0)),
                   pl.BlockSpec((None, tl, GATE_COLS), lambda i, t: (i, t, 0)),
                   pl.BlockSpec((None, n_rows, tl), lambda i, t: (i, 0, t))],
        out_shape=[jax.ShapeDtypeStruct((b, l, GATE_COLS), F32),
                   jax.ShapeDtypeStruct((b, l, GATE_COLS), F32),
                   jax.ShapeDtypeStruct((b, n_rows, l), F32)],
        scratch_shapes=[pltpu.VMEM((1, GATE_COLS), F32)],
        compiler_params=_cparams(("parallel", "arbitrary")),
        name="gates",
    )(gpre3, b_gate)


def _mlstm_kernel(q_ref, k_ref, v_ref, o_ref, gc_ref, gr_ref, gh_ref, c0_ref, n0_ref, m0_ref,
                  h_ref, c_ref, n_ref, m_ref, ct_s, n_s, m_s, fp_s, *, n_heads, dk, dv):
    c_idx = pl.program_id(1)
    n_chunks = pl.num_programs(1)
    lc = q_ref.shape[0]

    @pl.when(c_idx == 0)
    def _():
        for h in range(n_heads):
            ct_s[h] = c0_ref[h].T
        n_s[...] = n0_ref[...]
        m_s[...] = m0_ref[...]
        fp_s[...] = jnp.zeros_like(fp_s)

    row = lax.broadcasted_iota(jnp.int32, (lc, lc), 0)
    col = lax.broadcasted_iota(jnp.int32, (lc, lc), 1)
    causal = col <= row
    gc = gc_ref[...]
    gr = gr_ref[...]
    fp = fp_s[...]
    m_all = m_s[...]
    for h in range(n_heads):
        q = q_ref[:, h * dk:(h + 1) * dk]
        k = k_ref[:, h * dk:(h + 1) * dk]
        v = v_ref[:, h * dv:(h + 1) * dv]
        f0 = fp[:, n_heads + h:n_heads + h + 1]
        ig_row = gr[h:h + 1, :]
        ig_col = gc[:, h:h + 1]
        b_row = gr[n_heads + h:n_heads + h + 1, :] - f0
        b_col = gc[:, n_heads + h:n_heads + h + 1] - f0
        m_prev = m_all[:, h:h + 1]
        ct = ct_s[h]
        nvec = n_s[h:h + 1, :]

        d = jnp.where(causal, b_col - b_row + ig_row, -jnp.inf)
        g = b_col + m_prev
        m_t = jnp.maximum(g, jnp.max(d, axis=1, keepdims=True))
        w_inter = jnp.exp(g - m_t)
        s = _dot_nt(q, k) * jnp.exp(d - m_t)
        inter = jnp.dot(q, ct.astype(BF16), preferred_element_type=F32)
        num = w_inter * inter + jnp.dot(s.astype(BF16), v, preferred_element_type=F32)
        qn = jnp.sum(q.astype(F32) * nvec, axis=1, keepdims=True)
        den = w_inter * qn + jnp.sum(s, axis=1, keepdims=True)
        hh = num / jnp.maximum(jnp.abs(den), jnp.exp(-m_t))

        m_new = m_t[lc - 1:lc, :]
        b_last = b_col[lc - 1:lc, :]
        a_col = jnp.exp(b_last - b_col + ig_col - m_new)
        decay = jnp.exp(b_last + m_prev - m_new)
        ka = k.astype(F32) * a_col
        ct_s[h] = decay * ct + _dot_tn(ka.astype(BF16), v)
        n_s[h:h + 1, :] = decay * nvec + jnp.sum(ka, axis=0, keepdims=True)
        m_s[:, h:h + 1] = m_new

        hn = hh * lax.rsqrt(jnp.mean(hh * hh, axis=1, keepdims=True) + HEAD_NORM_EPS)
        og = o_ref[:, h * dv:(h + 1) * dv].astype(F32)
        hn = hn * gh_ref[:, h * dv:(h + 1) * dv] * jax.nn.sigmoid(og)
        h_ref[:, h * dv:(h + 1) * dv] = hn.astype(h_ref.dtype)
    fp_s[...] = gc[lc - 1:lc, :]

    @pl.when(c_idx == n_chunks - 1)
    def _():
        for h in range(n_heads):
            c_ref[h] = ct_s[h].T
        n_ref[...] = n_s[...]
        m_ref[...] = m_s[...]


def _mlstm(slab3, gc, gr, g_head, layer, c0, n0, m0, n_heads, dk, dv, lc_pref):
    b, l, _ = slab3.shape
    lc = _tile(l, lc_pref)
    dq = n_heads * dk
    da = n_heads * dv
    assert da == 2 * dq
    m0r = m0.reshape(b, 1, n_heads)
    bmap = lambda i, c: (i, 0, 0)
    outs = pl.pallas_call(
        functools.partial(_mlstm_kernel, n_heads=n_heads, dk=dk, dv=dv),
        grid=(b, l // lc),
        in_specs=[pl.BlockSpec((None, lc, dq), lambda i, c: (i, c, 0)),
                  pl.BlockSpec((None, lc, dq), lambda i, c: (i, c, 1)),
                  pl.BlockSpec((None, lc, da), lambda i, c: (i, c, 1)),
                  pl.BlockSpec((None, lc, da), lambda i, c: (i, c, 2)),
                  pl.BlockSpec((None, lc, GATE_COLS), lambda i, c: (i, c, 0)),
                  pl.BlockSpec((None, gr.shape[1], lc), lambda i, c: (i, 0, c)),
                  pl.BlockSpec((None, 1, da), lambda i, c: (layer, 0, 0)),
                  pl.BlockSpec((None, n_heads, dv, dk), lambda i, c: (i, 0, 0, 0)),
                  pl.BlockSpec((None, n_heads, dk), bmap),
                  pl.BlockSpec((None, 1, n_heads), bmap)],
        out_specs=[pl.BlockSpec((None, lc, da), lambda i, c: (i, c, 0)),
                   pl.BlockSpec((None, n_heads, dv, dk), lambda i, c: (i, 0, 0, 0)),
                   pl.BlockSpec((None, n_heads, dk), bmap),
                   pl.BlockSpec((None, 1, n_heads), bmap)],
        out_shape=[jax.ShapeDtypeStruct((b, l, da), BF16),
                   jax.ShapeDtypeStruct((b, n_heads, dv, dk), F32),
                   jax.ShapeDtypeStruct((b, n_heads, dk), F32),
                   jax.ShapeDtypeStruct((b, 1, n_heads), F32)],
        scratch_shapes=[pltpu.VMEM((n_heads, dk, dv), F32),
                        pltpu.VMEM((n_heads, dk), F32),
                        pltpu.VMEM((1, n_heads), F32),
                        pltpu.VMEM((1, GATE_COLS), F32)],
        compiler_params=_cparams(("parallel", "arbitrary")),
        name="mlstm",
    )(slab3, slab3, slab3, slab3, gc, gr, g_head, c0, n0, m0r)
    h_a, c, n, m = outs
    return h_a, c, n, m.reshape(b, n_heads)


def _fox_prompt_kernel(q_ref, k_ref, v_ref, gcq_ref, gck_ref, o_ref, qa_s, ka_s, va_s, p_s, m_s, al_s, acc_s,
                       *, lane0, tq, hp):
    hg = pl.program_id(1)
    i = pl.program_id(2)
    dh = q_ref.shape[1] // hp

    @pl.when(i == 0)
    def _():
        gck = gck_ref[...]
        lane = lax.broadcasted_iota(jnp.int32, (k_ref.shape[0], dh), 1)
        ones_lane = jnp.where(lane == 0, 1.0, 0.0).astype(BF16)
        for u in range(hp):
            ka_s[u, :, :dh] = k_ref[:, u * dh:(u + 1) * dh]
            ka_s[u, :, dh:] = _bias_lanes(_pick_lane(gck, lane0 + hg * hp + u) * LOG2E, False)
            va_s[u, :, :dh] = v_ref[:, u * dh:(u + 1) * dh]
            va_s[u, :, dh:] = ones_lane

    gcq = gcq_ref[...]
    for u in range(hp):
        qa_s[u, :, :dh] = q_ref[:, u * dh:(u + 1) * dh]
        qa_s[u, :, dh:] = _bias_lanes(_pick_lane(gcq, lane0 + hg * hp + u) * LOG2E, True)

    def logits(u, j):
        start = pl.multiple_of(j * tq, tq)
        return _dot_nt(qa_s[u], ka_s[u, pl.ds(start, tq), :])

    def softmax(u, s, diagonal):
        if diagonal:
            row = lax.broadcasted_iota(jnp.int32, s.shape, 0)
            col = lax.broadcasted_iota(jnp.int32, s.shape, 1)
            s = jnp.where(col <= row, s, -jnp.inf)
            m_new = jnp.broadcast_to(jnp.max(s, axis=1, keepdims=True), (tq, LANES))
            al_s[u] = jnp.zeros_like(m_new)
        else:
            m = m_s[u]
            m_new = jnp.maximum(m, jnp.max(s, axis=1, keepdims=True))
            al_s[u] = jnp.exp2(m - m_new)
        m_s[u] = m_new
        p_s[u] = jnp.exp2(s - jnp.concatenate([m_new] * (tq // LANES), axis=1)).astype(BF16)

    def values(u, j):
        start = pl.multiple_of(j * tq, tq)
        pv = jnp.dot(p_s[u], va_s[u, pl.ds(start, tq), :], preferred_element_type=F32)
        return jnp.concatenate([al_s[u]] * (2 * dh // LANES), axis=1) * acc_s[u] + pv

    for u in range(hp):
        acc_s[u] = jnp.zeros(acc_s.shape[1:], F32)
        softmax(u, logits(u, i), True)

    def trip(j, _):
        prev = jnp.where(j == 0, i, j - 1)
        s = [logits(u, j) for u in range(hp)]
        for u in range(hp):
            acc_s[u] = values(u, prev)
        for u in range(hp):
            softmax(u, s[u], False)
        return 0

    lax.fori_loop(0, i, trip, 0)
    last = jnp.where(i == 0, i, i - 1)
    for u in range(hp):
        acc = values(u, last)
        o_ref[:, u * dh:(u + 1) * dh] = (acc[:, :dh] / acc[:, dh:dh + 1]).astype(o_ref.dtype)


def _fox_prompt(slab3, gc, n_heads, dh, q_col, lane0, tq_pref, hp):
    b, l, _ = slab3.shape
    tq = _tile(l, tq_pref)
    assert n_heads % hp == 0 and q_col % hp == 0
    qb, ng, w = q_col // hp, n_heads // hp, hp * dh
    return pl.pallas_call(
        functools.partial(_fox_prompt_kernel, lane0=lane0, tq=tq, hp=hp),
        grid=(b, ng, l // tq),
        in_specs=[pl.BlockSpec((None, tq, w), lambda bi, h, i: (bi, i, qb + h)),
                  pl.BlockSpec((None, l, w), lambda bi, h, i: (bi, 0, qb + ng + h), pipeline_mode=pl.Buffered(1)),
                  pl.BlockSpec((None, l, w), lambda bi, h, i: (bi, 0, qb + 2 * ng + h), pipeline_mode=pl.Buffered(1)),
                  pl.BlockSpec((None, tq, GATE_COLS), lambda bi, h, i: (bi, i, 0)),
                  pl.BlockSpec((None, l, GATE_COLS), lambda bi, h, i: (bi, 0, 0), pipeline_mode=pl.Buffered(1))],
        out_specs=pl.BlockSpec((None, tq, w), lambda bi, h, i: (bi, i, h)),
        out_shape=jax.ShapeDtypeStruct((b, l, n_heads * dh), BF16),
        scratch_shapes=[pltpu.VMEM((hp, tq, 2 * dh), BF16), pltpu.VMEM((hp, l, 2 * dh), BF16),
                        pltpu.VMEM((hp, l, 2 * dh), BF16), pltpu.VMEM((hp, tq, tq), BF16),
                        pltpu.VMEM((hp, tq, LANES), F32), pltpu.VMEM((hp, tq, LANES), F32),
                        pltpu.VMEM((hp, tq, 2 * dh), F32)],
        compiler_params=_cparams(("parallel", "parallel", "arbitrary")),
        name="fox_prompt",
    )(slab3, slab3, slab3, gc, gc)


def _fox_sample_kernel(q_ref, k_ref, v_ref, ck_ref, cv_ref, clf_ref, gc_ref, o_ref, *, n_heads, dh, lane0):
    l = q_ref.shape[0]
    cum = _prefix_sum_rows(clf_ref[...])
    past = cum.shape[0]
    gc = gc_ref[...]
    row = lax.broadcasted_iota(jnp.int32, (l, l), 0)
    col = lax.broadcasted_iota(jnp.int32, (l, l), 1)
    for h in range(n_heads):
        f_old = cum[:, h:h + 1]
        f_new = f_old[past - 1:past, :] + gc[:, lane0 + h:lane0 + h + 1]
        bias_new_k = _bias_lanes(f_new * LOG2E, False)
        qa = jnp.concatenate([q_ref[:, h * dh:(h + 1) * dh], _bias_lanes(f_new * LOG2E, True)], axis=1)
        ka_new = jnp.concatenate([k_ref[:, h * dh:(h + 1) * dh], bias_new_k], axis=1)
        ka_old = jnp.concatenate([ck_ref[:, h, :].astype(BF16), _bias_lanes(f_old * LOG2E, False)], axis=1)
        s_old = _dot_nt(qa, ka_old)
        s_new = jnp.where(col <= row, _dot_nt(qa, ka_new), -jnp.inf)
        m = jnp.maximum(jnp.max(s_old, axis=1, keepdims=True), jnp.max(s_new, axis=1, keepdims=True))
        p_old = jnp.exp2(s_old - m)
        p_new = jnp.exp2(s_new - m)
        den = jnp.sum(p_old, axis=1, keepdims=True) + jnp.sum(p_new, axis=1, keepdims=True)
        acc = (jnp.dot(p_old.astype(BF16), cv_ref[:, h, :].astype(BF16), preferred_element_type=F32)
               + jnp.dot(p_new.astype(BF16), v_ref[:, h * dh:(h + 1) * dh], preferred_element_type=F32))
        o_ref[:, h * dh:(h + 1) * dh] = (acc / den).astype(o_ref.dtype)


def _fox_sample(slab3, ck, cv, clf, layer, gc, n_heads, dh, q_col, lane0):
    b, l, _ = slab3.shape
    past = ck.shape[2]
    dd = n_heads * dh
    qb = q_col // n_heads
    return pl.pallas_call(
        functools.partial(_fox_sample_kernel, n_heads=n_heads, dh=dh, lane0=lane0),
        grid=(b,),
        in_specs=[pl.BlockSpec((None, l, dd), lambda bi: (bi, 0, qb)),
                  pl.BlockSpec((None, l, dd), lambda bi: (bi, 0, qb + 1)),
                  pl.BlockSpec((None, l, dd), lambda bi: (bi, 0, qb + 2)),
                  pl.BlockSpec((None, None, past, n_heads, dh), lambda bi: (layer, bi, 0, 0, 0)),
                  pl.BlockSpec((None, None, past, n_heads, dh), lambda bi: (layer, bi, 0, 0, 0)),
                  pl.BlockSpec((None, None, past, n_heads), lambda bi: (layer, bi, 0, 0)),
                  pl.BlockSpec((None, l, GATE_COLS), lambda bi: (bi, 0, 0))],
        out_specs=pl.BlockSpec((None, l, dd), lambda bi: (bi, 0, 0)),
        out_shape=jax.ShapeDtypeStruct((b, l, dd), BF16),
        compiler_params=_cparams(("parallel",)),
        name="fox_sample",
    )(slab3, slab3, slab3, ck, cv, clf, gc)


def _outproj_kernel(ha_ref, hb_ref, x_ref, wa_ref, wb_ref, g_ref, b_ref, y_ref, *, alpha):
    tm = x_ref.shape[0]
    tr = min(tm, 256)
    for r in range(tm // tr):
        rows = slice(r * tr, (r + 1) * tr)
        mix = (jnp.dot(ha_ref[rows, :], wa_ref[...], preferred_element_type=F32)
               + jnp.dot(hb_ref[rows, :], wb_ref[...], preferred_element_type=F32))
        y_ref[rows, :] = _layer_norm(alpha * x_ref[rows, :] + mix, g_ref[...], b_ref[...])


def _outproj(ha2, hb2, x2, w_out, ln_g, ln_b, layer, alpha, tm_pref):
    m, d = x2.shape
    da = ha2.shape[1]
    db = hb2.shape[1]
    assert da == db
    tm = _tile(m, tm_pref)
    vec = pl.BlockSpec((None, 1, d), lambda i: (layer, 0, 0))
    return pl.pallas_call(
        functools.partial(_outproj_kernel, alpha=alpha),
        grid=(m // tm,),
        in_specs=[pl.BlockSpec((tm, da), lambda i: (i, 0)),
                  pl.BlockSpec((tm, db), lambda i: (i, 0)),
                  pl.BlockSpec((tm, d), lambda i: (i, 0)),
                  pl.BlockSpec((None, da, d), lambda i: (layer, 0, 0)),
                  pl.BlockSpec((None, db, d), lambda i: (layer, 1, 0)),
                  vec, vec],
        out_specs=pl.BlockSpec((tm, d), lambda i: (i, 0)),
        out_shape=jax.ShapeDtypeStruct((m, d), F32),
        compiler_params=_cparams(("parallel",)),
        name="outproj_ln",
    )(ha2, hb2, x2, w_out, w_out, ln_g, ln_b)


def _ffn_kernel(x_ref, wu_ref, wd_ref, g_ref, b_ref, y_ref, xb_ref, acc_ref, *, alpha):
    j = pl.program_id(1)

    @pl.when(j == 0)
    def _():
        xb_ref[...] = x_ref[...].astype(BF16)
        acc_ref[...] = jnp.zeros_like(acc_ref)

    u = jnp.dot(xb_ref[...], wu_ref[...], preferred_element_type=F32)
    u = jnp.maximum(u, 0.0)
    acc_ref[...] += jnp.dot((u * u).astype(BF16), wd_ref[...], preferred_element_type=F32)

    @pl.when(j == pl.num_programs(1) - 1)
    def _():
        y_ref[...] = _layer_norm(alpha * x_ref[...] + acc_ref[...], g_ref[...], b_ref[...])


def _ffn(x2, w_up, w_down, ln_g, ln_b, layer, alpha, tm_pref, tf_pref):
    m, d = x2.shape
    f = w_up.shape[2]
    tm = _tile(m, tm_pref)
    tf = _tile(f, tf_pref)
    vec = pl.BlockSpec((None, 1, d), lambda i, j: (layer, 0, 0))
    return pl.pallas_call(
        functools.partial(_ffn_kernel, alpha=alpha),
        grid=(m // tm, f // tf),
        in_specs=[pl.BlockSpec((tm, d), lambda i, j: (i, 0)),
                  pl.BlockSpec((None, d, tf), lambda i, j: (layer, 0, j)),
                  pl.BlockSpec((None, tf, d), lambda i, j: (layer, j, 0)),
                  vec, vec],
        out_specs=pl.BlockSpec((tm, d), lambda i, j: (i, 0)),
        out_shape=jax.ShapeDtypeStruct((m, d), F32),
        scratch_shapes=[pltpu.VMEM((tm, d), BF16), pltpu.VMEM((tm, d), F32)],
        compiler_params=_cparams(("parallel", "arbitrary")),
        name="ffn_ln",
    )(x2, w_up, w_down, ln_g, ln_b)


def _layer(x, layer, cache, kv_prev, wts, dims, tiles):
    h_a, dk, dv, h_b, dh = dims
    (w_main, w_gate, col_scale, b_gate, g_head, w_out, ln1_g, ln1_b, w_up, w_down, ln2_g, ln2_b, alpha) = wts
    b, l, d = x.shape
    m = b * l
    d_b = h_b * dh
    x2 = x.reshape(m, d)

    slab, k_all, v_all, gpre = _inproj(x2, w_main, w_gate, col_scale, layer, d_b, kv_prev, tiles["tm_in"])
    slab3 = slab.reshape(b, l, slab.shape[1])
    n_rows = 2 * h_a + h_b
    lf, gc, gr = _gates(gpre.reshape(b, l, GATE_COLS), b_gate, layer, h_a, n_rows)
    q_col = (2 * h_a * dk + 2 * h_a * dv) // dh

    if cache is None:
        c0 = jnp.zeros((b, h_a, dv, dk), F32)
        n0 = jnp.zeros((b, h_a, dk), F32)
        m0 = jnp.zeros((b, h_a), F32)
        hb = _fox_prompt(slab3, gc, h_b, dh, q_col, 2 * h_a, tiles["tq"], tiles["fox_heads"])
    else:
        ck, cv, clf, c_all, n_all, m_all = cache
        c0, n0, m0 = c_all[layer], n_all[layer], m_all[layer]
        hb = _fox_sample(slab3, ck, cv, clf, layer, gc, h_b, dh, q_col, 2 * h_a)
    ha, c, n, mm = _mlstm(slab3, gc, gr, g_head, layer, c0, n0, m0, h_a, dk, dv, tiles["lc"])

    x1 = _outproj(ha.reshape(m, -1), hb.reshape(m, -1), x2, w_out, ln1_g, ln1_b, layer, alpha, tiles["tm_out"])
    y = _ffn(x1, w_up, w_down, ln2_g, ln2_b, layer, alpha, tiles["tm_ffn"], tiles["tf_ffn"])
    lfb = lf[:, :, 2 * h_a:n_rows]
    return y.reshape(b, l, d), (k_all, v_all), (lfb, c, n, mm)


def kernel(x_prompt, x_sample, cache_fox_k, cache_fox_v, cache_fox_logf, state_mlstm_c, state_mlstm_n,
           state_mlstm_m, w_in, b_gates, g_mlstm, w_out, ln1_g, ln1_b, w_up, w_down, ln2_g, ln2_b):
    depth, d_model, _ = w_in.shape
    _, _, _, h_b, dh = cache_fox_k.shape
    _, _, h_a, dv, dk = state_mlstm_c.shape
    d_a = h_a * dv
    d_b = h_b * dh
    dims = (h_a, dk, dv, h_b, dh)
    alpha = (2 * depth) ** 0.25
    widths = (h_a * dk, h_a * dk, d_a, d_a, h_a, h_a, d_b, d_b, d_b, h_b)
    offs = [0]
    for w in widths:
        offs.append(offs[-1] + w)
    n_gates = 2 * h_a + h_b

    w_main, w_gate = _win_prep(w_in, ((offs[0], offs[4]), (offs[6], offs[9])), ((offs[4], offs[6]), (offs[9], offs[10])))
    b_gate = jnp.pad(b_gates, ((0, 0), (0, GATE_COLS - n_gates))).reshape(depth, 1, GATE_COLS)
    n_main = w_main.shape[2]
    col_scale = jnp.concatenate([jnp.full((h_a * dk,), dk ** -0.5, F32),
                                 jnp.ones((offs[4] - offs[1],), F32),
                                 jnp.full((d_b,), dh ** -0.5 * LOG2E, F32),
                                 jnp.ones((2 * d_b,), F32)]).reshape(1, n_main)
    wts = (w_main, w_gate, col_scale, b_gate, g_mlstm.reshape(depth, 1, d_a), w_out.astype(BF16),
           ln1_g.reshape(depth, 1, d_model), ln1_b.reshape(depth, 1, d_model),
           w_up.astype(BF16), w_down.astype(BF16),
           ln2_g.reshape(depth, 1, d_model), ln2_b.reshape(depth, 1, d_model), alpha)
    cache = (cache_fox_k, cache_fox_v, cache_fox_logf, state_mlstm_c, state_mlstm_n, state_mlstm_m)

    tiles = dict(tm_in=1024, tq=1024, fox_heads=2, lc=256, tm_out=512, tm_ffn=512, tf_ffn=1024)

    yp, ys = x_prompt, x_sample
    kv_p = kv_s = None
    small_p, small_s = [], []
    for layer in range(depth):
        yp, kv_p, st_p = _layer(yp, layer, None, kv_p, wts, dims, tiles)
        ys, kv_s, st_s = _layer(ys, layer, cache, kv_s, wts, dims, tiles)
        small_p.append(st_p)
        small_s.append(st_s)
    stack = lambda states, i: jnp.stack([s[i] for s in states], axis=0)
    bp, lp, _ = x_prompt.shape
    bs, ls, _ = x_sample.shape
    return ((yp, ys)
            + tuple(a.reshape(depth, bp, lp, h_b, dh) for a in kv_p) + tuple(stack(small_p, i) for i in range(4))
            + tuple(a.reshape(depth, bs, ls, h_b, dh) for a in kv_s) + tuple(stack(small_s, i) for i in range(4)))
```

```python
import functools
import math

import jax
import jax.numpy as jnp
from jax import lax
from jax.experimental import pallas as pl
from jax.experimental.pallas import tpu as pltpu

F32 = jnp.float32
BF16 = jnp.bfloat16

LN_EPS = 1e-5
HEAD_NORM_EPS = 1e-6
LOG2E = math.log2(math.e)
LANES = 128
GATE_COLS = LANES
VMEM_LIMIT = 56 * 1024 * 1024


def _cparams(sem):
    return pltpu.CompilerParams(dimension_semantics=sem, vmem_limit_bytes=VMEM_LIMIT)


def _tile(n, pref):
    t = min(n, pref)
    assert n % t == 0, (n, pref)
    return t


def _log_sigmoid(g):
    return jnp.minimum(g, 0.0) - jnp.log1p(jnp.exp(-jnp.abs(g)))


def _dot_nt(a, b):
    return lax.dot_general(a, b, (((1,), (1,)), ((), ())), preferred_element_type=F32)


def _dot_tn(a, b):
    return lax.dot_general(a, b, (((0,), (0,)), ((), ())), preferred_element_type=F32)


def _layer_norm(z, g, b):
    mu = jnp.mean(z, axis=-1, keepdims=True)
    zc = z - mu
    var = jnp.mean(zc * zc, axis=-1, keepdims=True)
    return zc * lax.rsqrt(var + LN_EPS) * g + b


def _prefix_sum_rows(v):
    row = lax.broadcasted_iota(jnp.int32, v.shape, 0)
    s = 1
    while s < v.shape[0]:
        v = v + jnp.where(row >= s, pltpu.roll(v, s, 0), 0.0)
        s *= 2
    return v


def _split3(f):
    hi = f.astype(BF16).astype(F32)
    r = f - hi
    mid = r.astype(BF16).astype(F32)
    lo = (r - mid).astype(BF16).astype(F32)
    return hi, mid, lo


def _bias_lanes(f, query_side):
    hi, mid, lo = _split3(f)
    lane = lax.broadcasted_iota(jnp.int32, (f.shape[0], LANES), 1)
    if query_side:
        a = jnp.where(lane == 0, hi, jnp.where(lane == 1, mid, jnp.where(lane == 2, lo, jnp.where(lane < 6, 1.0, 0.0))))
    else:
        a = jnp.where(lane == 3, -hi, jnp.where(lane == 4, -mid, jnp.where(lane == 5, -lo, jnp.where(lane < 3, 1.0, 0.0))))
    return a.astype(BF16)


def _pick_lane(x, idx):
    lane = lax.broadcasted_iota(jnp.int32, x.shape, 1)
    return jnp.sum(jnp.where(lane == idx, x, 0.0), axis=1, keepdims=True)


def _win_prep_kernel(w_ref, main_ref, gate_ref, *, main_cols, gate_cols):
    w = w_ref[...]
    main = jnp.concatenate([w[:, a:b] for a, b in main_cols], axis=1).astype(BF16)
    nj, _, tn = main_ref.shape
    for j in range(nj):
        main_ref[j] = main[:, j * tn:(j + 1) * tn]
    gates = [w[:, a:b] for a, b in gate_cols]
    pad = GATE_COLS - sum(b - a for a, b in gate_cols)
    gate_ref[...] = jnp.concatenate(gates + [jnp.zeros((w.shape[0], pad), F32)], axis=1).astype(BF16)


def _win_prep(w_in, main_cols, gate_cols, tn):
    depth, d, n_in = w_in.shape
    n_main = sum(b - a for a, b in main_cols)
    nj = n_main // tn
    tk = _tile(d, 256)
    return pl.pallas_call(
        functools.partial(_win_prep_kernel, main_cols=main_cols, gate_cols=gate_cols),
        grid=(depth, d // tk),
        in_specs=[pl.BlockSpec((None, tk, n_in), lambda l, t: (l, t, 0))],
        out_specs=[pl.BlockSpec((None, nj, tk, tn), lambda l, t: (l, 0, t, 0)),
                   pl.BlockSpec((None, tk, GATE_COLS), lambda l, t: (l, t, 0))],
        out_shape=[jax.ShapeDtypeStruct((depth, nj, d, tn), BF16),
                   jax.ShapeDtypeStruct((depth, d, GATE_COLS), BF16)],
        compiler_params=_cparams(("parallel", "parallel")),
        name="win_prep",
    )(w_in)


def _inproj_kernel(*refs, jk0, jv0, aliased):
    x_ref, w_ref, wg_ref, cs_ref = refs[:4]
    slab_ref, k_ref, v_ref, g_ref, xb_ref = refs[-5:]
    del aliased
    j = pl.program_id(1)

    @pl.when(j == 0)
    def _():
        xb = x_ref[...].astype(BF16)
        xb_ref[...] = xb
        g_ref[...] = jnp.dot(xb, wg_ref[...], preferred_element_type=F32)

    def tile(f32_ref):
        acc = jnp.dot(xb_ref[...], w_ref[...], preferred_element_type=F32)
        slab_ref[...] = (acc * cs_ref[...]).astype(BF16)
        if f32_ref is not None:
            f32_ref[...] = acc

    pl.when(j < jk0)(lambda: tile(None))
    pl.when((j >= jk0) & (j < jv0))(lambda: tile(k_ref))
    pl.when(j >= jv0)(lambda: tile(v_ref))


def _inproj(x2, w_main, w_gate, col_scale, layer, d_b, kv_prev, tm_pref):
    m, d = x2.shape
    depth, nj, _, tn = w_main.shape
    n = nj * tn
    tm = _tile(m, tm_pref)
    assert d_b % tn == 0
    nkv = d_b // tn
    jk0 = nj - 2 * nkv
    jv0 = nj - nkv
    aliased = kv_prev is not None
    in_specs = [pl.BlockSpec((tm, d), lambda i, j: (i, 0)),
                pl.BlockSpec((None, None, d, tn), lambda i, j: (layer, j, 0, 0)),
                pl.BlockSpec((None, d, GATE_COLS), lambda i, j: (layer, 0, 0)),
                pl.BlockSpec((1, tn), lambda i, j: (0, j))]
    args = [x2, w_main, w_gate, col_scale]
    aliases = {}
    if aliased:
        in_specs += [pl.BlockSpec(memory_space=pl.ANY), pl.BlockSpec(memory_space=pl.ANY)]
        args += list(kv_prev)
        aliases = {4: 1, 5: 2}
    return pl.pallas_call(
        functools.partial(_inproj_kernel, jk0=jk0, jv0=jv0, aliased=aliased),
        grid=(m // tm, nj),
        in_specs=in_specs,
        out_specs=[pl.BlockSpec((tm, tn), lambda i, j: (i, j)),
                   pl.BlockSpec((None, tm, tn), lambda i, j: (layer, i, jnp.clip(j - jk0, 0, nkv - 1))),
                   pl.BlockSpec((None, tm, tn), lambda i, j: (layer, i, jnp.clip(j - jv0, 0, nkv - 1))),
                   pl.BlockSpec((tm, GATE_COLS), lambda i, j: (i, 0))],
        out_shape=[jax.ShapeDtypeStruct((m, n), BF16),
                   jax.ShapeDtypeStruct((depth, m, d_b), F32),
                   jax.ShapeDtypeStruct((depth, m, d_b), F32),
                   jax.ShapeDtypeStruct((m, GATE_COLS), F32)],
        scratch_shapes=[pltpu.VMEM((tm, d), BF16)],
        input_output_aliases=aliases,
        compiler_params=_cparams(("parallel", "arbitrary")),
        name="inproj",
    )(*args)


def _gates_kernel(gp_ref, bg_ref, lf_ref, gc_ref, gr_ref, carry_ref, *, n_raw, n_rows):
    t = pl.program_id(1)

    @pl.when(t == 0)
    def _():
        carry_ref[...] = jnp.zeros_like(carry_ref)

    g = gp_ref[...] + bg_ref[...]
    tl = g.shape[0]
    lane = lax.broadcasted_iota(jnp.int32, g.shape, 1)
    raw = lane < n_raw
    val = jnp.where(raw, g, _log_sigmoid(g))
    lf_ref[...] = val
    c = _prefix_sum_rows(val) + carry_ref[...]
    carry_ref[...] = c[tl - 1:tl, :]
    out = jnp.where(raw, g, c)
    gc_ref[...] = out
    gr_ref[...] = out.T[:n_rows, :]


def _gates(gpre3, b_gate, layer, n_raw, n_rows):
    b, l, _ = gpre3.shape
    tl = _tile(l, 512)
    return pl.pallas_call(
        functools.partial(_gates_kernel, n_raw=n_raw, n_rows=n_rows),
        grid=(b, l // tl),
        in_specs=[pl.BlockSpec((None, tl, GATE_COLS), lambda i, t: (i, t, 0)),
                  pl.BlockSpec((None, 1, GATE_COLS), lambda i, t: (layer, 0, 0))],
        out_specs=[pl.BlockSpec((None, tl, GATE_COLS), lambda i, t: (i, t, 0)),
                   pl.BlockSpec((None, tl, GATE_COLS), lambda i, t: (i, t, 0)),
                   pl.BlockSpec((None, n_rows, tl), lambda i, t: (i, 0, t))],
        out_shape=[jax.ShapeDtypeStruct((b, l, GATE_COLS), F32),
                   jax.ShapeDtypeStruct((b, l, GATE_COLS), F32),
                   jax.ShapeDtypeStruct((b, n_rows, l), F32)],
        scratch_shapes=[pltpu.VMEM((1, GATE_COLS), F32)],
        compiler_params=_cparams(("parallel", "arbitrary")),
        name="gates",
    )(gpre3, b_gate)


def _mlstm_kernel(q_ref, k_ref, v_ref, o_ref, gc_ref, gr_ref, gh_ref, c0_ref, n0_ref, m0_ref,
                  h_ref, c_ref, n_ref, m_ref, cta_s, m_s, fp_s, *, n_heads, dk, dv):
    c_idx = pl.program_id(1)
    n_chunks = pl.num_programs(1)
    lc = q_ref.shape[0]

    @pl.when(c_idx == 0)
    def _():
        lane = lax.broadcasted_iota(jnp.int32, (dk, LANES), 1)
        for h in range(n_heads):
            cta_s[h, :, :dv] = c0_ref[h].T
            n_cols = jnp.broadcast_to(n0_ref[h:h + 1, :], (LANES, dk)).T
            cta_s[h, :, dv:] = jnp.where(lane == 0, n_cols, 0.0)
        m_s[...] = m0_ref[...]
        fp_s[...] = jnp.zeros_like(fp_s)

    gc = gc_ref[...]
    gr = gr_ref[...]
    fp = fp_s[...]
    m_prev = m_s[...]
    row1 = lax.broadcasted_iota(jnp.int32, gc.shape, 0)
    b = gc - fp
    ig = pltpu.roll(gc, n_heads, 1)
    run = ig - b
    step = 1
    while step < lc:
        run = jnp.maximum(run, jnp.where(row1 >= step, pltpu.roll(run, step, 0), -jnp.inf))
        step *= 2
    g = b + m_prev
    m_t = jnp.maximum(g, b + run)
    w_inter = jnp.exp(g - m_t)
    e_neg = jnp.exp(-m_t)
    m_new = m_t[lc - 1:lc, :]
    b_last = b[lc - 1:lc, :]
    a_all = jnp.exp(b_last - b + ig - m_new)
    decay = jnp.exp(b_last + m_prev - m_new)
    m_s[...] = m_new
    fp_s[...] = gc[lc - 1:lc, :]

    row = lax.broadcasted_iota(jnp.int32, (lc, lc), 0)
    col = lax.broadcasted_iota(jnp.int32, (lc, lc), 1)
    causal = col <= row
    lane = lax.broadcasted_iota(jnp.int32, (lc, LANES), 1)
    ones_col = jnp.where(lane == 0, 1.0, 0.0).astype(BF16)
    for h in range(n_heads):
        hl = n_heads + h
        q = q_ref[:, h * dk:(h + 1) * dk]
        k = k_ref[:, h * dk:(h + 1) * dk]
        va = jnp.concatenate([v_ref[:, h * dv:(h + 1) * dv], ones_col], axis=1)
        ig_row = gr[h:h + 1, :]
        b_row = gr[hl:hl + 1, :] - fp[:, hl:hl + 1]
        d = jnp.where(causal, b[:, hl:hl + 1] - b_row + ig_row, -jnp.inf)
        s = _dot_nt(q, k) * jnp.exp(d - m_t[:, hl:hl + 1])
        cta = cta_s[h]
        both = (w_inter[:, hl:hl + 1] * jnp.dot(q, cta.astype(BF16), preferred_element_type=F32)
                + jnp.dot(s.astype(BF16), va, preferred_element_type=F32))
        hh = both[:, :dv] / jnp.maximum(jnp.abs(both[:, dv:dv + 1]), e_neg[:, hl:hl + 1])

        ka = (k.astype(F32) * a_all[:, hl:hl + 1]).astype(BF16)
        cta_s[h] = decay[:, hl:hl + 1] * cta + _dot_tn(ka, va)

        hn = hh * lax.rsqrt(jnp.mean(hh * hh, axis=1, keepdims=True) + HEAD_NORM_EPS)
        og = o_ref[:, h * dv:(h + 1) * dv].astype(F32)
        hn = hn * gh_ref[:, h * dv:(h + 1) * dv] * jax.nn.sigmoid(og)
        h_ref[:, h * dv:(h + 1) * dv] = hn.astype(h_ref.dtype)

    @pl.when(c_idx == n_chunks - 1)
    def _():
        for h in range(n_heads):
            c_ref[h] = cta_s[h, :, :dv].T
            n_ref[h:h + 1, :] = cta_s[h, :, dv:].T[0:1, :]
        m_ref[...] = m_s[...]


def _mlstm(slab3, gc, gr, g_head, layer, c0, n0, m0, n_heads, dk, dv, lc_pref):
    b, l, _ = slab3.shape
    lc = _tile(l, lc_pref)
    dq = n_heads * dk
    da = n_heads * dv
    assert da == 2 * dq and dk == LANES
    m0r = jnp.pad(m0, ((0, 0), (n_heads, GATE_COLS - 2 * n_heads))).reshape(b, 1, GATE_COLS)
    bmap = lambda i, c: (i, 0, 0)
    outs = pl.pallas_call(
        functools.partial(_mlstm_kernel, n_heads=n_heads, dk=dk, dv=dv),
        grid=(b, l // lc),
        in_specs=[pl.BlockSpec((None, lc, dq), lambda i, c: (i, c, 0)),
                  pl.BlockSpec((None, lc, dq), lambda i, c: (i, c, 1)),
                  pl.BlockSpec((None, lc, da), lambda i, c: (i, c, 1)),
                  pl.BlockSpec((None, lc, da), lambda i, c: (i, c, 2)),
                  pl.BlockSpec((None, lc, GATE_COLS), lambda i, c: (i, c, 0)),
                  pl.BlockSpec((None, gr.shape[1], lc), lambda i, c: (i, 0, c)),
                  pl.BlockSpec((None, 1, da), lambda i, c: (layer, 0, 0)),
                  pl.BlockSpec((None, n_heads, dv, dk), lambda i, c: (i, 0, 0, 0)),
                  pl.BlockSpec((None, n_heads, dk), bmap),
                  pl.BlockSpec((None, 1, GATE_COLS), bmap)],
        out_specs=[pl.BlockSpec((None, lc, da), lambda i, c: (i, c, 0)),
                   pl.BlockSpec((None, n_heads, dv, dk), lambda i, c: (i, 0, 0, 0)),
                   pl.BlockSpec((None, n_heads, dk), bmap),
                   pl.BlockSpec((None, 1, GATE_COLS), bmap)],
        out_shape=[jax.ShapeDtypeStruct((b, l, da), BF16),
                   jax.ShapeDtypeStruct((b, n_heads, dv, dk), F32),
                   jax.ShapeDtypeStruct((b, n_heads, dk), F32),
                   jax.ShapeDtypeStruct((b, 1, GATE_COLS), F32)],
        scratch_shapes=[pltpu.VMEM((n_heads, dk, dv + LANES), F32),
                        pltpu.VMEM((1, GATE_COLS), F32),
                        pltpu.VMEM((1, GATE_COLS), F32)],
        compiler_params=_cparams(("parallel", "arbitrary")),
        name="mlstm",
    )(slab3, slab3, slab3, slab3, gc, gr, g_head, c0, n0, m0r)
    h_a, c, n, m = outs
    return h_a, c, n, m[:, 0, n_heads:2 * n_heads]


def _fox_prompt_kernel(q_ref, k_ref, v_ref, gcq_ref, gck_ref, o_ref, qa_s, ka_s, va_s, p_s, m_s, al_s, acc_s,
                       *, lane0, tq, hp):
    hg = pl.program_id(1)
    i = pl.program_id(2)
    dh = q_ref.shape[1] // hp

    @pl.when(i == 0)
    def _():
        gck = gck_ref[...]
        lane = lax.broadcasted_iota(jnp.int32, (k_ref.shape[0], dh), 1)
        ones_lane = jnp.where(lane == 0, 1.0, 0.0).astype(BF16)
        for u in range(hp):
            ka_s[u, :, :dh] = k_ref[:, u * dh:(u + 1) * dh]
            ka_s[u, :, dh:] = _bias_lanes(_pick_lane(gck, lane0 + hg * hp + u) * LOG2E, False)
            va_s[u, :, :dh] = v_ref[:, u * dh:(u + 1) * dh]
            va_s[u, :, dh:] = ones_lane

    gcq = gcq_ref[...]
    for u in range(hp):
        qa_s[u, :, :dh] = q_ref[:, u * dh:(u + 1) * dh]
        qa_s[u, :, dh:] = _bias_lanes(_pick_lane(gcq, lane0 + hg * hp + u) * LOG2E, True)

    def logits(u, j):
        start = pl.multiple_of(j * tq, tq)
        return _dot_nt(qa_s[u], ka_s[u, pl.ds(start, tq), :])

    def softmax(u, s, diagonal):
        if diagonal:
            row = lax.broadcasted_iota(jnp.int32, s.shape, 0)
            col = lax.broadcasted_iota(jnp.int32, s.shape, 1)
            s = jnp.where(col <= row, s, -jnp.inf)
            m_new = jnp.broadcast_to(jnp.max(s, axis=1, keepdims=True), (tq, LANES))
            al_s[u] = jnp.zeros_like(m_new)
        else:
            m = m_s[u]
            m_new = jnp.maximum(m, jnp.max(s, axis=1, keepdims=True))
            al_s[u] = jnp.exp2(m - m_new)
        m_s[u] = m_new
        p_s[u] = jnp.exp2(s - jnp.concatenate([m_new] * (tq // LANES), axis=1)).astype(BF16)

    def values(u, j):
        start = pl.multiple_of(j * tq, tq)
        pv = jnp.dot(p_s[u], va_s[u, pl.ds(start, tq), :], preferred_element_type=F32)
        return jnp.concatenate([al_s[u]] * (2 * dh // LANES), axis=1) * acc_s[u] + pv

    for u in range(hp):
        acc_s[u] = jnp.zeros(acc_s.shape[1:], F32)
        softmax(u, logits(u, i), True)

    def trip(j, _):
        prev = jnp.where(j == 0, i, j - 1)
        s = [logits(u, j) for u in range(hp)]
        for u in range(hp):
            acc_s[u] = values(u, prev)
        for u in range(hp):
            softmax(u, s[u], False)
        return 0

    lax.fori_loop(0, i, trip, 0)
    last = jnp.where(i == 0, i, i - 1)
    for u in range(hp):
        acc = values(u, last)
        o_ref[:, u * dh:(u + 1) * dh] = (acc[:, :dh] / acc[:, dh:dh + 1]).astype(o_ref.dtype)


def _fox_prompt(slab3, gc, n_heads, dh, q_col, lane0, tq_pref, hp):
    b, l, _ = slab3.shape
    tq = _tile(l, tq_pref)
    assert n_heads % hp == 0 and q_col % hp == 0
    qb, ng, w = q_col // hp, n_heads // hp, hp * dh
    return pl.pallas_call(
        functools.partial(_fox_prompt_kernel, lane0=lane0, tq=tq, hp=hp),
        grid=(b, ng, l // tq),
        in_specs=[pl.BlockSpec((None, tq, w), lambda bi, h, i: (bi, i, qb + h)),
                  pl.BlockSpec((None, l, w), lambda bi, h, i: (bi, 0, qb + ng + h)),
                  pl.BlockSpec((None, l, w), lambda bi, h, i: (bi, 0, qb + 2 * ng + h)),
                  pl.BlockSpec((None, tq, GATE_COLS), lambda bi, h, i: (bi, i, 0)),
                  pl.BlockSpec((None, l, GATE_COLS), lambda bi, h, i: (bi, 0, 0))],
        out_specs=pl.BlockSpec((None, tq, w), lambda bi, h, i: (bi, i, h)),
        out_shape=jax.ShapeDtypeStruct((b, l, n_heads * dh), BF16),
        scratch_shapes=[pltpu.VMEM((hp, tq, 2 * dh), BF16), pltpu.VMEM((hp, l, 2 * dh), BF16),
                        pltpu.VMEM((hp, l, 2 * dh), BF16), pltpu.VMEM((hp, tq, tq), BF16),
                        pltpu.VMEM((hp, tq, LANES), F32), pltpu.VMEM((hp, tq, LANES), F32),
                        pltpu.VMEM((hp, tq, 2 * dh), F32)],
        compiler_params=_cparams(("parallel", "parallel", "arbitrary")),
        name="fox_prompt",
    )(slab3, slab3, slab3, gc, gc)


def _fox_sample_kernel(q_ref, k_ref, v_ref, ck_ref, cv_ref, clf_ref, gc_ref, o_ref, *, n_heads, dh, lane0):
    l = q_ref.shape[0]
    cum = _prefix_sum_rows(clf_ref[...])
    past = cum.shape[0]
    gc = gc_ref[...]
    row = lax.broadcasted_iota(jnp.int32, (l, l), 0)
    col = lax.broadcasted_iota(jnp.int32, (l, l), 1)
    for h in range(n_heads):
        f_old = cum[:, h:h + 1]
        f_new = f_old[past - 1:past, :] + gc[:, lane0 + h:lane0 + h + 1]
        bias_new_k = _bias_lanes(f_new * LOG2E, False)
        qa = jnp.concatenate([q_ref[:, h * dh:(h + 1) * dh], _bias_lanes(f_new * LOG2E, True)], axis=1)
        ka_new = jnp.concatenate([k_ref[:, h * dh:(h + 1) * dh], bias_new_k], axis=1)
        ka_old = jnp.concatenate([ck_ref[:, h, :].astype(BF16), _bias_lanes(f_old * LOG2E, False)], axis=1)
        s_old = _dot_nt(qa, ka_old)
        s_new = jnp.where(col <= row, _dot_nt(qa, ka_new), -jnp.inf)
        m = jnp.maximum(jnp.max(s_old, axis=1, keepdims=True), jnp.max(s_new, axis=1, keepdims=True))
        p_old = jnp.exp2(s_old - m)
        p_new = jnp.exp2(s_new - m)
        den = jnp.sum(p_old, axis=1, keepdims=True) + jnp.sum(p_new, axis=1, keepdims=True)
        acc = (jnp.dot(p_old.astype(BF16), cv_ref[:, h, :].astype(BF16), preferred_element_type=F32)
               + jnp.dot(p_new.astype(BF16), v_ref[:, h * dh:(h + 1) * dh], preferred_element_type=F32))
        o_ref[:, h * dh:(h + 1) * dh] = (acc / den).astype(o_ref.dtype)


def _fox_sample(slab3, ck, cv, clf, layer, gc, n_heads, dh, q_col, lane0):
    b, l, _ = slab3.shape
    past = ck.shape[2]
    dd = n_heads * dh
    qb = q_col // n_heads
    return pl.pallas_call(
        functools.partial(_fox_sample_kernel, n_heads=n_heads, dh=dh, lane0=lane0),
        grid=(b,),
        in_specs=[pl.BlockSpec((None, l, dd), lambda bi: (bi, 0, qb)),
                  pl.BlockSpec((None, l, dd), lambda bi: (bi, 0, qb + 1)),
                  pl.BlockSpec((None, l, dd), lambda bi: (bi, 0, qb + 2)),
                  pl.BlockSpec((None, None, past, n_heads, dh), lambda bi: (layer, bi, 0, 0, 0)),
                  pl.BlockSpec((None, None, past, n_heads, dh), lambda bi: (layer, bi, 0, 0, 0)),
                  pl.BlockSpec((None, None, past, n_heads), lambda bi: (layer, bi, 0, 0)),
                  pl.BlockSpec((None, l, GATE_COLS), lambda bi: (bi, 0, 0))],
        out_specs=pl.BlockSpec((None, l, dd), lambda bi: (bi, 0, 0)),
        out_shape=jax.ShapeDtypeStruct((b, l, dd), BF16),
        compiler_params=_cparams(("parallel",)),
        name="fox_sample",
    )(slab3, slab3, slab3, ck, cv, clf, gc)


def _outproj_kernel(ha_ref, hb_ref, x_ref, wa_ref, wb_ref, g_ref, b_ref, y_ref, *, alpha):
    tm = x_ref.shape[0]
    tr = min(tm, 256)
    for r in range(tm // tr):
        rows = slice(r * tr, (r + 1) * tr)
        mix = (jnp.dot(ha_ref[rows, :], wa_ref[...], preferred_element_type=F32)
               + jnp.dot(hb_ref[rows, :], wb_ref[...], preferred_element_type=F32))
        y_ref[rows, :] = _layer_norm(alpha * x_ref[rows, :] + mix, g_ref[...], b_ref[...])


def _outproj(ha2, hb2, x2, w_out, ln_g, ln_b, layer, alpha, tm_pref):
    m, d = x2.shape
    da = ha2.shape[1]
    db = hb2.shape[1]
    assert da == db
    tm = _tile(m, tm_pref)
    vec = pl.BlockSpec((None, 1, d), lambda i: (layer, 0, 0))
    return pl.pallas_call(
        functools.partial(_outproj_kernel, alpha=alpha),
        grid=(m // tm,),
        in_specs=[pl.BlockSpec((tm, da), lambda i: (i, 0)),
                  pl.BlockSpec((tm, db), lambda i: (i, 0)),
                  pl.BlockSpec((tm, d), lambda i: (i, 0)),
                  pl.BlockSpec((None, da, d), lambda i: (layer, 0, 0)),
                  pl.BlockSpec((None, db, d), lambda i: (layer, 1, 0)),
                  vec, vec],
        out_specs=pl.BlockSpec((tm, d), lambda i: (i, 0)),
        out_shape=jax.ShapeDtypeStruct((m, d), F32),
        compiler_params=_cparams(("parallel",)),
        name="outproj_ln",
    )(ha2, hb2, x2, w_out, w_out, ln_g, ln_b)


def _ffn_kernel(x_ref, wu_ref, wd_ref, g_ref, b_ref, y_ref, xb_ref, acc_ref, *, alpha):
    j = pl.program_id(1)

    @pl.when(j == 0)
    def _():
        xb_ref[...] = x_ref[...].astype(BF16)
        acc_ref[...] = jnp.zeros_like(acc_ref)

    u = jnp.dot(xb_ref[...], wu_ref[...], preferred_element_type=F32)
    u = jnp.maximum(u, 0.0)
    acc_ref[...] += jnp.dot((u * u).astype(BF16), wd_ref[...], preferred_element_type=F32)

    @pl.when(j == pl.num_programs(1) - 1)
    def _():
        y_ref[...] = _layer_norm(alpha * x_ref[...] + acc_ref[...], g_ref[...], b_ref[...])


def _ffn(x2, w_up, w_down, ln_g, ln_b, layer, alpha, tm_pref, tf_pref):
    m, d = x2.shape
    f = w_up.shape[2]
    tm = _tile(m, tm_pref)
    tf = _tile(f, tf_pref)
    vec = pl.BlockSpec((None, 1, d), lambda i, j: (layer, 0, 0))
    return pl.pallas_call(
        functools.partial(_ffn_kernel, alpha=alpha),
        grid=(m // tm, f // tf),
        in_specs=[pl.BlockSpec((tm, d), lambda i, j: (i, 0)),
                  pl.BlockSpec((None, d, tf), lambda i, j: (layer, 0, j)),
                  pl.BlockSpec((None, tf, d), lambda i, j: (layer, j, 0)),
                  vec, vec],
        out_specs=pl.BlockSpec((tm, d), lambda i, j: (i, 0)),
        out_shape=jax.ShapeDtypeStruct((m, d), F32),
        scratch_shapes=[pltpu.VMEM((tm, d), BF16), pltpu.VMEM((tm, d), F32)],
        compiler_params=_cparams(("parallel", "arbitrary")),
        name="ffn_ln",
    )(x2, w_up, w_down, ln_g, ln_b)


def _layer(x, layer, cache, kv_prev, wts, dims, tiles):
    h_a, dk, dv, h_b, dh = dims
    (w_main, w_gate, col_scale, b_gate, g_head, w_out, ln1_g, ln1_b, w_up, w_down, ln2_g, ln2_b, alpha) = wts
    b, l, d = x.shape
    m = b * l
    d_b = h_b * dh
    x2 = x.reshape(m, d)

    slab, k_all, v_all, gpre = _inproj(x2, w_main, w_gate, col_scale, layer, d_b, kv_prev, tiles["tm_in"])
    slab3 = slab.reshape(b, l, slab.shape[1])
    n_rows = 2 * h_a + h_b
    lf, gc, gr = _gates(gpre.reshape(b, l, GATE_COLS), b_gate, layer, h_a, n_rows)
    q_col = (2 * h_a * dk + 2 * h_a * dv) // dh

    if cache is None:
        c0 = jnp.zeros((b, h_a, dv, dk), F32)
        n0 = jnp.zeros((b, h_a, dk), F32)
        m0 = jnp.zeros((b, h_a), F32)
        hb = _fox_prompt(slab3, gc, h_b, dh, q_col, 2 * h_a, tiles["tq"], tiles["fox_heads"])
    else:
        ck, cv, clf, c_all, n_all, m_all = cache
        c0, n0, m0 = c_all[layer], n_all[layer], m_all[layer]
        hb = _fox_sample(slab3, ck, cv, clf, layer, gc, h_b, dh, q_col, 2 * h_a)
    ha, c, n, mm = _mlstm(slab3, gc, gr, g_head, layer, c0, n0, m0, h_a, dk, dv, tiles["lc"])

    x1 = _outproj(ha.reshape(m, -1), hb.reshape(m, -1), x2, w_out, ln1_g, ln1_b, layer, alpha, tiles["tm_out"])
    y = _ffn(x1, w_up, w_down, ln2_g, ln2_b, layer, alpha, tiles["tm_ffn"], tiles["tf_ffn"])
    lfb = lf[:, :, 2 * h_a:n_rows]
    return y.reshape(b, l, d), (k_all, v_all), (lfb, c, n, mm)


def kernel(x_prompt, x_sample, cache_fox_k, cache_fox_v, cache_fox_logf, state_mlstm_c, state_mlstm_n,
           state_mlstm_m, w_in, b_gates, g_mlstm, w_out, ln1_g, ln1_b, w_up, w_down, ln2_g, ln2_b):
    depth, d_model, _ = w_in.shape
    _, _, _, h_b, dh = cache_fox_k.shape
    _, _, h_a, dv, dk = state_mlstm_c.shape
    d_a = h_a * dv
    d_b = h_b * dh
    dims = (h_a, dk, dv, h_b, dh)
    alpha = (2 * depth) ** 0.25
    widths = (h_a * dk, h_a * dk, d_a, d_a, h_a, h_a, d_b, d_b, d_b, h_b)
    offs = [0]
    for w in widths:
        offs.append(offs[-1] + w)
    n_gates = 2 * h_a + h_b

    tiles = dict(tm_in=1024, tn_in=512, tq=1024, fox_heads=2, lc=256, tm_out=512, tm_ffn=512, tf_ffn=1024)

    w_main, w_gate = _win_prep(w_in, ((offs[0], offs[4]), (offs[6], offs[9])), ((offs[4], offs[6]), (offs[9], offs[10])),
                               tiles["tn_in"])
    b_gate = jnp.pad(b_gates, ((0, 0), (0, GATE_COLS - n_gates))).reshape(depth, 1, GATE_COLS)
    n_main = w_main.shape[1] * w_main.shape[3]
    col_scale = jnp.concatenate([jnp.full((h_a * dk,), dk ** -0.5, F32),
                                 jnp.ones((offs[4] - offs[1],), F32),
                                 jnp.full((d_b,), dh ** -0.5 * LOG2E, F32),
                                 jnp.ones((2 * d_b,), F32)]).reshape(1, n_main)
    wts = (w_main, w_gate, col_scale, b_gate, g_mlstm.reshape(depth, 1, d_a), w_out.astype(BF16),
           ln1_g.reshape(depth, 1, d_model), ln1_b.reshape(depth, 1, d_model),
           w_up.astype(BF16), w_down.astype(BF16),
           ln2_g.reshape(depth, 1, d_model), ln2_b.reshape(depth, 1, d_model), alpha)
    cache = (cache_fox_k, cache_fox_v, cache_fox_logf, state_mlstm_c, state_mlstm_n, state_mlstm_m)


    yp, ys = x_prompt, x_sample
    kv_p = kv_s = None
    small_p, small_s = [], []
    for layer in range(depth):
        yp, kv_p, st_p = _layer(yp, layer, None, kv_p, wts, dims, tiles)
        ys, kv_s, st_s = _layer(ys, layer, cache, kv_s, wts, dims, tiles)
        small_p.append(st_p)
        small_s.append(st_s)
    stack = lambda states, i: jnp.stack([s[i] for s in states], axis=0)
    bp, lp, _ = x_prompt.shape
    bs, ls, _ = x_sample.shape
    return ((yp, ys)
            + tuple(a.reshape(depth, bp, lp, h_b, dh) for a in kv_p) + tuple(stack(small_p, i) for i in range(4))
            + tuple(a.reshape(depth, bs, ls, h_b, dh) for a in kv_s) + tuple(stack(small_s, i) for i in range(4)))
```

```python
import functools
import math

import jax
import jax.numpy as jnp
from jax import lax
from jax.experimental import pallas as pl
from jax.experimental.pallas import tpu as pltpu

F32 = jnp.float32
BF16 = jnp.bfloat16

LN_EPS = 1e-5
HEAD_NORM_EPS = 1e-6
LOG2E = math.log2(math.e)
LANES = 128
GATE_COLS = LANES
VMEM_LIMIT = 56 * 1024 * 1024


def _cparams(sem):
    return pltpu.CompilerParams(dimension_semantics=sem, vmem_limit_bytes=VMEM_LIMIT)


def _tile(n, pref):
    t = min(n, pref)
    assert n % t == 0, (n, pref)
    return t


def _log_sigmoid(g):
    return jnp.minimum(g, 0.0) - jnp.log1p(jnp.exp(-jnp.abs(g)))


def _dot_nt(a, b):
    return lax.dot_general(a, b, (((1,), (1,)), ((), ())), preferred_element_type=F32)


def _dot_tn(a, b):
    return lax.dot_general(a, b, (((0,), (0,)), ((), ())), preferred_element_type=F32)


def _layer_norm(z, g, b):
    mu = jnp.mean(z, axis=-1, keepdims=True)
    zc = z - mu
    var = jnp.mean(zc * zc, axis=-1, keepdims=True)
    return zc * lax.rsqrt(var + LN_EPS) * g + b


def _prefix_sum_rows(v):
    row = lax.broadcasted_iota(jnp.int32, v.shape, 0)
    s = 1
    while s < v.shape[0]:
        v = v + jnp.where(row >= s, pltpu.roll(v, s, 0), 0.0)
        s *= 2
    return v


def _split3(f):
    hi = f.astype(BF16).astype(F32)
    r = f - hi
    mid = r.astype(BF16).astype(F32)
    lo = (r - mid).astype(BF16).astype(F32)
    return hi, mid, lo


def _bias_lanes(f, query_side):
    hi, mid, lo = _split3(f)
    lane = lax.broadcasted_iota(jnp.int32, (f.shape[0], LANES), 1)
    if query_side:
        a = jnp.where(lane == 0, hi, jnp.where(lane == 1, mid, jnp.where(lane == 2, lo, jnp.where(lane < 6, 1.0, 0.0))))
    else:
        a = jnp.where(lane == 3, -hi, jnp.where(lane == 4, -mid, jnp.where(lane == 5, -lo, jnp.where(lane < 3, 1.0, 0.0))))
    return a.astype(BF16)


def _pick_lane(x, idx):
    lane = lax.broadcasted_iota(jnp.int32, x.shape, 1)
    return jnp.sum(jnp.where(lane == idx, x, 0.0), axis=1, keepdims=True)


def _win_prep_kernel(w_ref, main_ref, gate_ref, *, main_cols, gate_cols):
    w = w_ref[...]
    main = jnp.concatenate([w[:, a:b] for a, b in main_cols], axis=1).astype(BF16)
    nj, _, tn = main_ref.shape
    for j in range(nj):
        main_ref[j] = main[:, j * tn:(j + 1) * tn]
    gates = [w[:, a:b] for a, b in gate_cols]
    pad = GATE_COLS - sum(b - a for a, b in gate_cols)
    gate_ref[...] = jnp.concatenate(gates + [jnp.zeros((w.shape[0], pad), F32)], axis=1).astype(BF16)


def _win_prep(w_in, main_cols, gate_cols, tn):
    depth, d, n_in = w_in.shape
    n_main = sum(b - a for a, b in main_cols)
    nj = n_main // tn
    tk = _tile(d, 256)
    return pl.pallas_call(
        functools.partial(_win_prep_kernel, main_cols=main_cols, gate_cols=gate_cols),
        grid=(depth, d // tk),
        in_specs=[pl.BlockSpec((None, tk, n_in), lambda l, t: (l, t, 0))],
        out_specs=[pl.BlockSpec((None, nj, tk, tn), lambda l, t: (l, 0, t, 0)),
                   pl.BlockSpec((None, tk, GATE_COLS), lambda l, t: (l, t, 0))],
        out_shape=[jax.ShapeDtypeStruct((depth, nj, d, tn), BF16),
                   jax.ShapeDtypeStruct((depth, d, GATE_COLS), BF16)],
        compiler_params=_cparams(("parallel", "parallel")),
        name="win_prep",
    )(w_in)


def _inproj_kernel(*refs, jk0, jv0, aliased):
    x_ref, w_ref, wg_ref, cs_ref = refs[:4]
    slab_ref, k_ref, v_ref, g_ref, xb_ref = refs[-5:]
    del aliased
    j = pl.program_id(1)

    @pl.when(j == 0)
    def _():
        xb = x_ref[...].astype(BF16)
        xb_ref[...] = xb
        g_ref[...] = jnp.dot(xb, wg_ref[...], preferred_element_type=F32)

    def tile(f32_ref):
        acc = jnp.dot(xb_ref[...], w_ref[...], preferred_element_type=F32)
        slab_ref[...] = (acc * cs_ref[...]).astype(BF16)
        if f32_ref is not None:
            f32_ref[...] = acc

    pl.when(j < jk0)(lambda: tile(None))
    pl.when((j >= jk0) & (j < jv0))(lambda: tile(k_ref))
    pl.when(j >= jv0)(lambda: tile(v_ref))


def _inproj(x2, w_main, w_gate, col_scale, layer, d_b, kv_prev, tm_pref):
    m, d = x2.shape
    depth, nj, _, tn = w_main.shape
    n = nj * tn
    tm = _tile(m, tm_pref)
    assert d_b % tn == 0
    nkv = d_b // tn
    jk0 = nj - 2 * nkv
    jv0 = nj - nkv
    aliased = kv_prev is not None
    in_specs = [pl.BlockSpec((tm, d), lambda i, j: (i, 0)),
                pl.BlockSpec((None, None, d, tn), lambda i, j: (layer, j, 0, 0)),
                pl.BlockSpec((None, d, GATE_COLS), lambda i, j: (layer, 0, 0)),
                pl.BlockSpec((1, tn), lambda i, j: (0, j))]
    args = [x2, w_main, w_gate, col_scale]
    aliases = {}
    if aliased:
        in_specs += [pl.BlockSpec(memory_space=pl.ANY), pl.BlockSpec(memory_space=pl.ANY)]
        args += list(kv_prev)
        aliases = {4: 1, 5: 2}
    return pl.pallas_call(
        functools.partial(_inproj_kernel, jk0=jk0, jv0=jv0, aliased=aliased),
        grid=(m // tm, nj),
        in_specs=in_specs,
        out_specs=[pl.BlockSpec((tm, tn), lambda i, j: (i, j)),
                   pl.BlockSpec((None, tm, tn), lambda i, j: (layer, i, jnp.clip(j - jk0, 0, nkv - 1))),
                   pl.BlockSpec((None, tm, tn), lambda i, j: (layer, i, jnp.clip(j - jv0, 0, nkv - 1))),
                   pl.BlockSpec((tm, GATE_COLS), lambda i, j: (i, 0))],
        out_shape=[jax.ShapeDtypeStruct((m, n), BF16),
                   jax.ShapeDtypeStruct((depth, m, d_b), F32),
                   jax.ShapeDtypeStruct((depth, m, d_b), F32),
                   jax.ShapeDtypeStruct((m, GATE_COLS), F32)],
        scratch_shapes=[pltpu.VMEM((tm, d), BF16)],
        input_output_aliases=aliases,
        compiler_params=_cparams(("parallel", "arbitrary")),
        name="inproj",
    )(*args)


BIAS_LANES = 6


def _bias_placement(lane0, n_heads):
    import numpy as np
    place = np.zeros((3 * LANES, 2 * LANES), np.float32)
    ones = np.zeros((2, LANES), np.float32)
    for h in range(n_heads):
        for t in range(3):
            place[t * LANES + lane0 + h, BIAS_LANES * h + t] = 1.0
            place[t * LANES + lane0 + h, LANES + BIAS_LANES * h + 3 + t] = -1.0
            ones[0, BIAS_LANES * h + 3 + t] = 1.0
            ones[1, BIAS_LANES * h + t] = 1.0
    return jnp.asarray(place, BF16), jnp.asarray(ones, F32)


def _gates_kernel(gp_ref, bg_ref, place_ref, ones_ref, lf_ref, gc_ref, gr_ref, qb_ref, kb_ref, carry_ref,
                  *, n_raw, n_rows):
    t = pl.program_id(1)

    @pl.when(t == 0)
    def _():
        carry_ref[...] = jnp.zeros_like(carry_ref)

    g = gp_ref[...] + bg_ref[...]
    tl = g.shape[0]
    lane = lax.broadcasted_iota(jnp.int32, g.shape, 1)
    raw = lane < n_raw
    val = jnp.where(raw, g, _log_sigmoid(g))
    lf_ref[...] = val
    c = _prefix_sum_rows(val) + carry_ref[...]
    carry_ref[...] = c[tl - 1:tl, :]
    out = jnp.where(raw, g, c)
    gc_ref[...] = out
    gr_ref[...] = out.T[:n_rows, :]
    split = jnp.concatenate(_split3(c * LOG2E), axis=1).astype(BF16)
    moved = jnp.dot(split, place_ref[...], preferred_element_type=F32)
    qb_ref[...] = (moved[:, :LANES] + ones_ref[0:1, :]).astype(BF16)
    kb_ref[...] = (moved[:, LANES:] + ones_ref[1:2, :]).astype(BF16)


def _gates(gpre3, b_gate, layer, n_raw, n_rows, placement):
    b, l, _ = gpre3.shape
    tl = _tile(l, 512)
    place, ones = placement
    const = lambda i, t: (0, 0)
    tile = pl.BlockSpec((None, tl, GATE_COLS), lambda i, t: (i, t, 0))
    return pl.pallas_call(
        functools.partial(_gates_kernel, n_raw=n_raw, n_rows=n_rows),
        grid=(b, l // tl),
        in_specs=[tile,
                  pl.BlockSpec((None, 1, GATE_COLS), lambda i, t: (layer, 0, 0)),
                  pl.BlockSpec(place.shape, const),
                  pl.BlockSpec(ones.shape, const)],
        out_specs=[tile, tile, pl.BlockSpec((None, n_rows, tl), lambda i, t: (i, 0, t)), tile, tile],
        out_shape=[jax.ShapeDtypeStruct((b, l, GATE_COLS), F32),
                   jax.ShapeDtypeStruct((b, l, GATE_COLS), F32),
                   jax.ShapeDtypeStruct((b, n_rows, l), F32),
                   jax.ShapeDtypeStruct((b, l, GATE_COLS), BF16),
                   jax.ShapeDtypeStruct((b, l, GATE_COLS), BF16)],
        scratch_shapes=[pltpu.VMEM((1, GATE_COLS), F32)],
        compiler_params=_cparams(("parallel", "arbitrary")),
        name="gates",
    )(gpre3, b_gate, place, ones)


def _mlstm_kernel(q_ref, k_ref, v_ref, o_ref, gc_ref, gr_ref, gh_ref, c0_ref, n0_ref, m0_ref,
                  h_ref, c_ref, n_ref, m_ref, cta_s, m_s, fp_s, *, n_heads, dk, dv):
    c_idx = pl.program_id(1)
    n_chunks = pl.num_programs(1)
    lc = q_ref.shape[0]

    @pl.when(c_idx == 0)
    def _():
        lane = lax.broadcasted_iota(jnp.int32, (dk, LANES), 1)
        for h in range(n_heads):
            cta_s[h, :, :dv] = c0_ref[h].T
            n_cols = jnp.broadcast_to(n0_ref[h:h + 1, :], (LANES, dk)).T
            cta_s[h, :, dv:] = jnp.where(lane == 0, n_cols, 0.0)
        m_s[...] = m0_ref[...]
        fp_s[...] = jnp.zeros_like(fp_s)

    gc = gc_ref[...]
    gr = gr_ref[...]
    fp = fp_s[...]
    m_prev = m_s[...]
    row1 = lax.broadcasted_iota(jnp.int32, gc.shape, 0)
    b = gc - fp
    ig = pltpu.roll(gc, n_heads, 1)
    run = ig - b
    step = 1
    while step < lc:
        run = jnp.maximum(run, jnp.where(row1 >= step, pltpu.roll(run, step, 0), -jnp.inf))
        step *= 2
    g = b + m_prev
    m_t = jnp.maximum(g, b + run)
    w_inter = jnp.exp(g - m_t)
    e_neg = jnp.exp(-m_t)
    m_new = m_t[lc - 1:lc, :]
    b_last = b[lc - 1:lc, :]
    a_all = jnp.exp(b_last - b + ig - m_new)
    decay = jnp.exp(b_last + m_prev - m_new)
    m_s[...] = m_new
    fp_s[...] = gc[lc - 1:lc, :]

    row = lax.broadcasted_iota(jnp.int32, (lc, lc), 0)
    col = lax.broadcasted_iota(jnp.int32, (lc, lc), 1)
    causal = col <= row
    lane = lax.broadcasted_iota(jnp.int32, (lc, LANES), 1)
    ones_col = jnp.where(lane == 0, 1.0, 0.0).astype(BF16)
    for h in range(n_heads):
        hl = n_heads + h
        q = q_ref[:, h * dk:(h + 1) * dk]
        k = k_ref[:, h * dk:(h + 1) * dk]
        va = jnp.concatenate([v_ref[:, h * dv:(h + 1) * dv], ones_col], axis=1)
        ig_row = gr[h:h + 1, :]
        b_row = gr[hl:hl + 1, :] - fp[:, hl:hl + 1]
        d = jnp.where(causal, b[:, hl:hl + 1] - b_row + ig_row, -jnp.inf)
        s = _dot_nt(q, k) * jnp.exp(d - m_t[:, hl:hl + 1])
        cta = cta_s[h]
        both = (w_inter[:, hl:hl + 1] * jnp.dot(q, cta.astype(BF16), preferred_element_type=F32)
                + jnp.dot(s.astype(BF16), va, preferred_element_type=F32))
        hh = both[:, :dv] / jnp.maximum(jnp.abs(both[:, dv:dv + 1]), e_neg[:, hl:hl + 1])

        ka = (k.astype(F32) * a_all[:, hl:hl + 1]).astype(BF16)
        cta_s[h] = decay[:, hl:hl + 1] * cta + _dot_tn(ka, va)

        hn = hh * lax.rsqrt(jnp.mean(hh * hh, axis=1, keepdims=True) + HEAD_NORM_EPS)
        og = o_ref[:, h * dv:(h + 1) * dv].astype(F32)
        hn = hn * gh_ref[:, h * dv:(h + 1) * dv] * jax.nn.sigmoid(og)
        h_ref[:, h * dv:(h + 1) * dv] = hn.astype(h_ref.dtype)

    @pl.when(c_idx == n_chunks - 1)
    def _():
        for h in range(n_heads):
            c_ref[h] = cta_s[h, :, :dv].T
            n_ref[h:h + 1, :] = cta_s[h, :, dv:].T[0:1, :]
        m_ref[...] = m_s[...]


def _mlstm(slab3, gc, gr, g_head, layer, c0, n0, m0, n_heads, dk, dv, lc_pref):
    b, l, _ = slab3.shape
    lc = _tile(l, lc_pref)
    dq = n_heads * dk
    da = n_heads * dv
    assert da == 2 * dq and dk == LANES
    m0r = jnp.pad(m0, ((0, 0), (n_heads, GATE_COLS - 2 * n_heads))).reshape(b, 1, GATE_COLS)
    bmap = lambda i, c: (i, 0, 0)
    outs = pl.pallas_call(
        functools.partial(_mlstm_kernel, n_heads=n_heads, dk=dk, dv=dv),
        grid=(b, l // lc),
        in_specs=[pl.BlockSpec((None, lc, dq), lambda i, c: (i, c, 0)),
                  pl.BlockSpec((None, lc, dq), lambda i, c: (i, c, 1)),
                  pl.BlockSpec((None, lc, da), lambda i, c: (i, c, 1)),
                  pl.BlockSpec((None, lc, da), lambda i, c: (i, c, 2)),
                  pl.BlockSpec((None, lc, GATE_COLS), lambda i, c: (i, c, 0)),
                  pl.BlockSpec((None, gr.shape[1], lc), lambda i, c: (i, 0, c)),
                  pl.BlockSpec((None, 1, da), lambda i, c: (layer, 0, 0)),
                  pl.BlockSpec((None, n_heads, dv, dk), lambda i, c: (i, 0, 0, 0)),
                  pl.BlockSpec((None, n_heads, dk), bmap),
                  pl.BlockSpec((None, 1, GATE_COLS), bmap)],
        out_specs=[pl.BlockSpec((None, lc, da), lambda i, c: (i, c, 0)),
                   pl.BlockSpec((None, n_heads, dv, dk), lambda i, c: (i, 0, 0, 0)),
                   pl.BlockSpec((None, n_heads, dk), bmap),
                   pl.BlockSpec((None, 1, GATE_COLS), bmap)],
        out_shape=[jax.ShapeDtypeStruct((b, l, da), BF16),
                   jax.ShapeDtypeStruct((b, n_heads, dv, dk), F32),
                   jax.ShapeDtypeStruct((b, n_heads, dk), F32),
                   jax.ShapeDtypeStruct((b, 1, GATE_COLS), F32)],
        scratch_shapes=[pltpu.VMEM((n_heads, dk, dv + LANES), F32),
                        pltpu.VMEM((1, GATE_COLS), F32),
                        pltpu.VMEM((1, GATE_COLS), F32)],
        compiler_params=_cparams(("parallel", "arbitrary")),
        name="mlstm",
    )(slab3, slab3, slab3, slab3, gc, gr, g_head, c0, n0, m0r)
    h_a, c, n, m = outs
    return h_a, c, n, m[:, 0, n_heads:2 * n_heads]


def _fox_prompt_kernel(q_ref, k_ref, v_ref, qb_ref, kb_ref, o_ref, qa_s, ka_s, va_s, p_s, m_s, al_s, acc_s,
                       *, tq, hp):
    hg = pl.program_id(1)
    i = pl.program_id(2)
    dh = q_ref.shape[1] // hp

    @pl.when(i == 0)
    def _():
        kb = kb_ref[...]
        lane = lax.broadcasted_iota(jnp.int32, kb.shape, 1)
        ones_lane = jnp.where(lane == 0, 1.0, 0.0).astype(BF16)
        for u in range(hp):
            first = BIAS_LANES * (hg * hp + u)
            ka_s[u, :, :dh] = k_ref[:, u * dh:(u + 1) * dh]
            ka_s[u, :, dh:] = jnp.where((lane >= first) & (lane < first + BIAS_LANES), kb, jnp.zeros_like(kb))
            va_s[u, :, :dh] = v_ref[:, u * dh:(u + 1) * dh]
            va_s[u, :, dh:] = ones_lane

    for u in range(hp):
        qa_s[u, :, :dh] = q_ref[:, u * dh:(u + 1) * dh]
        qa_s[u, :, dh:] = qb_ref[...]

    def logits(u, j):
        start = pl.multiple_of(j * tq, tq)
        return _dot_nt(qa_s[u], ka_s[u, pl.ds(start, tq), :])

    def softmax(u, s, diagonal):
        if diagonal:
            row = lax.broadcasted_iota(jnp.int32, s.shape, 0)
            col = lax.broadcasted_iota(jnp.int32, s.shape, 1)
            s = jnp.where(col <= row, s, -jnp.inf)
            m_new = jnp.broadcast_to(jnp.max(s, axis=1, keepdims=True), (tq, LANES))
            al_s[u] = jnp.zeros_like(m_new)
        else:
            m = m_s[u]
            m_new = jnp.maximum(m, jnp.max(s, axis=1, keepdims=True))
            al_s[u] = jnp.exp2(m - m_new)
        m_s[u] = m_new
        p_s[u] = jnp.exp2(s - jnp.concatenate([m_new] * (tq // LANES), axis=1)).astype(BF16)

    def values(u, j):
        start = pl.multiple_of(j * tq, tq)
        pv = jnp.dot(p_s[u], va_s[u, pl.ds(start, tq), :], preferred_element_type=F32)
        return jnp.concatenate([al_s[u]] * (2 * dh // LANES), axis=1) * acc_s[u] + pv

    for u in range(hp):
        acc_s[u] = jnp.zeros(acc_s.shape[1:], F32)
        softmax(u, logits(u, i), True)

    def trip(j, _):
        prev = jnp.where(j == 0, i, j - 1)
        s = [logits(u, j) for u in range(hp)]
        for u in range(hp):
            acc_s[u] = values(u, prev)
        for u in range(hp):
            softmax(u, s[u], False)
        return 0

    lax.fori_loop(0, i, trip, 0)
    last = jnp.where(i == 0, i, i - 1)
    for u in range(hp):
        acc = values(u, last)
        o_ref[:, u * dh:(u + 1) * dh] = (acc[:, :dh] / acc[:, dh:dh + 1]).astype(o_ref.dtype)


def _fox_prompt(slab3, qb_all, kb_all, n_heads, dh, q_col, tq_pref, hp):
    b, l, _ = slab3.shape
    tq = _tile(l, tq_pref)
    assert n_heads % hp == 0 and q_col % hp == 0
    qb, ng, w = q_col // hp, n_heads // hp, hp * dh
    return pl.pallas_call(
        functools.partial(_fox_prompt_kernel, tq=tq, hp=hp),
        grid=(b, ng, l // tq),
        in_specs=[pl.BlockSpec((None, tq, w), lambda bi, h, i: (bi, i, qb + h)),
                  pl.BlockSpec((None, l, w), lambda bi, h, i: (bi, 0, qb + ng + h)),
                  pl.BlockSpec((None, l, w), lambda bi, h, i: (bi, 0, qb + 2 * ng + h)),
                  pl.BlockSpec((None, tq, GATE_COLS), lambda bi, h, i: (bi, i, 0)),
                  pl.BlockSpec((None, l, GATE_COLS), lambda bi, h, i: (bi, 0, 0))],
        out_specs=pl.BlockSpec((None, tq, w), lambda bi, h, i: (bi, i, h)),
        out_shape=jax.ShapeDtypeStruct((b, l, n_heads * dh), BF16),
        scratch_shapes=[pltpu.VMEM((hp, tq, 2 * dh), BF16), pltpu.VMEM((hp, l, 2 * dh), BF16),
                        pltpu.VMEM((hp, l, 2 * dh), BF16), pltpu.VMEM((hp, tq, tq), BF16),
                        pltpu.VMEM((hp, tq, LANES), F32), pltpu.VMEM((hp, tq, LANES), F32),
                        pltpu.VMEM((hp, tq, 2 * dh), F32)],
        compiler_params=_cparams(("parallel", "parallel", "arbitrary")),
        name="fox_prompt",
    )(slab3, slab3, slab3, qb_all, kb_all)


def _fox_sample_kernel(q_ref, k_ref, v_ref, ck_ref, cv_ref, clf_ref, gc_ref, o_ref, *, n_heads, dh, lane0):
    l = q_ref.shape[0]
    cum = _prefix_sum_rows(clf_ref[...])
    past = cum.shape[0]
    gc = gc_ref[...]
    row = lax.broadcasted_iota(jnp.int32, (l, l), 0)
    col = lax.broadcasted_iota(jnp.int32, (l, l), 1)
    for h in range(n_heads):
        f_old = cum[:, h:h + 1]
        f_new = f_old[past - 1:past, :] + gc[:, lane0 + h:lane0 + h + 1]
        bias_new_k = _bias_lanes(f_new * LOG2E, False)
        qa = jnp.concatenate([q_ref[:, h * dh:(h + 1) * dh], _bias_lanes(f_new * LOG2E, True)], axis=1)
        ka_new = jnp.concatenate([k_ref[:, h * dh:(h + 1) * dh], bias_new_k], axis=1)
        ka_old = jnp.concatenate([ck_ref[:, h, :].astype(BF16), _bias_lanes(f_old * LOG2E, False)], axis=1)
        s_old = _dot_nt(qa, ka_old)
        s_new = jnp.where(col <= row, _dot_nt(qa, ka_new), -jnp.inf)
        m = jnp.maximum(jnp.max(s_old, axis=1, keepdims=True), jnp.max(s_new, axis=1, keepdims=True))
        p_old = jnp.exp2(s_old - m)
        p_new = jnp.exp2(s_new - m)
        den = jnp.sum(p_old, axis=1, keepdims=True) + jnp.sum(p_new, axis=1, keepdims=True)
        acc = (jnp.dot(p_old.astype(BF16), cv_ref[:, h, :].astype(BF16), preferred_element_type=F32)
               + jnp.dot(p_new.astype(BF16), v_ref[:, h * dh:(h + 1) * dh], preferred_element_type=F32))
        o_ref[:, h * dh:(h + 1) * dh] = (acc / den).astype(o_ref.dtype)


def _fox_sample(slab3, ck, cv, clf, layer, gc, n_heads, dh, q_col, lane0):
    b, l, _ = slab3.shape
    past = ck.shape[2]
    dd = n_heads * dh
    qb = q_col // n_heads
    return pl.pallas_call(
        functools.partial(_fox_sample_kernel, n_heads=n_heads, dh=dh, lane0=lane0),
        grid=(b,),
        in_specs=[pl.BlockSpec((None, l, dd), lambda bi: (bi, 0, qb)),
                  pl.BlockSpec((None, l, dd), lambda bi: (bi, 0, qb + 1)),
                  pl.BlockSpec((None, l, dd), lambda bi: (bi, 0, qb + 2)),
                  pl.BlockSpec((None, None, past, n_heads, dh), lambda bi: (layer, bi, 0, 0, 0)),
                  pl.BlockSpec((None, None, past, n_heads, dh), lambda bi: (layer, bi, 0, 0, 0)),
                  pl.BlockSpec((None, None, past, n_heads), lambda bi: (layer, bi, 0, 0)),
                  pl.BlockSpec((None, l, GATE_COLS), lambda bi: (bi, 0, 0))],
        out_specs=pl.BlockSpec((None, l, dd), lambda bi: (bi, 0, 0)),
        out_shape=jax.ShapeDtypeStruct((b, l, dd), BF16),
        compiler_params=_cparams(("parallel",)),
        name="fox_sample",
    )(slab3, slab3, slab3, ck, cv, clf, gc)


def _outproj_kernel(ha_ref, hb_ref, x_ref, wa_ref, wb_ref, g_ref, b_ref, y_ref, *, alpha):
    tm = x_ref.shape[0]
    tr = min(tm, 256)
    for r in range(tm // tr):
        rows = slice(r * tr, (r + 1) * tr)
        mix = (jnp.dot(ha_ref[rows, :], wa_ref[...], preferred_element_type=F32)
               + jnp.dot(hb_ref[rows, :], wb_ref[...], preferred_element_type=F32))
        y_ref[rows, :] = _layer_norm(alpha * x_ref[rows, :] + mix, g_ref[...], b_ref[...])


def _outproj(ha2, hb2, x2, w_out, ln_g, ln_b, layer, alpha, tm_pref):
    m, d = x2.shape
    da = ha2.shape[1]
    db = hb2.shape[1]
    assert da == db
    tm = _tile(m, tm_pref)
    vec = pl.BlockSpec((None, 1, d), lambda i: (layer, 0, 0))
    return pl.pallas_call(
        functools.partial(_outproj_kernel, alpha=alpha),
        grid=(m // tm,),
        in_specs=[pl.BlockSpec((tm, da), lambda i: (i, 0)),
                  pl.BlockSpec((tm, db), lambda i: (i, 0)),
                  pl.BlockSpec((tm, d), lambda i: (i, 0)),
                  pl.BlockSpec((None, da, d), lambda i: (layer, 0, 0)),
                  pl.BlockSpec((None, db, d), lambda i: (layer, 1, 0)),
                  vec, vec],
        out_specs=pl.BlockSpec((tm, d), lambda i: (i, 0)),
        out_shape=jax.ShapeDtypeStruct((m, d), F32),
        compiler_params=_cparams(("parallel",)),
        name="outproj_ln",
    )(ha2, hb2, x2, w_out, w_out, ln_g, ln_b)


def _ffn_kernel(x_ref, wu_ref, wd_ref, g_ref, b_ref, y_ref, xb_ref, acc_ref, *, alpha):
    j = pl.program_id(1)

    @pl.when(j == 0)
    def _():
        xb_ref[...] = x_ref[...].astype(BF16)
        acc_ref[...] = jnp.zeros_like(acc_ref)

    u = jnp.dot(xb_ref[...], wu_ref[...], preferred_element_type=F32)
    u = jnp.maximum(u, 0.0)
    acc_ref[...] += jnp.dot((u * u).astype(BF16), wd_ref[...], preferred_element_type=F32)

    @pl.when(j == pl.num_programs(1) - 1)
    def _():
        y_ref[...] = _layer_norm(alpha * x_ref[...] + acc_ref[...], g_ref[...], b_ref[...])


def _ffn(x2, w_up, w_down, ln_g, ln_b, layer, alpha, tm_pref, tf_pref):
    m, d = x2.shape
    f = w_up.shape[2]
    tm = _tile(m, tm_pref)
    tf = _tile(f, tf_pref)
    vec = pl.BlockSpec((None, 1, d), lambda i, j: (layer, 0, 0))
    return pl.pallas_call(
        functools.partial(_ffn_kernel, alpha=alpha),
        grid=(m // tm, f // tf),
        in_specs=[pl.BlockSpec((tm, d), lambda i, j: (i, 0)),
                  pl.BlockSpec((None, d, tf), lambda i, j: (layer, 0, j)),
                  pl.BlockSpec((None, tf, d), lambda i, j: (layer, j, 0)),
                  vec, vec],
        out_specs=pl.BlockSpec((tm, d), lambda i, j: (i, 0)),
        out_shape=jax.ShapeDtypeStruct((m, d), F32),
        scratch_shapes=[pltpu.VMEM((tm, d), BF16), pltpu.VMEM((tm, d), F32)],
        compiler_params=_cparams(("parallel", "arbitrary")),
        name="ffn_ln",
    )(x2, w_up, w_down, ln_g, ln_b)


def _layer(x, layer, cache, kv_prev, wts, dims, tiles):
    h_a, dk, dv, h_b, dh = dims
    (w_main, w_gate, col_scale, b_gate, g_head, w_out, ln1_g, ln1_b, w_up, w_down, ln2_g, ln2_b, alpha) = wts
    b, l, d = x.shape
    m = b * l
    d_b = h_b * dh
    x2 = x.reshape(m, d)

    slab, k_all, v_all, gpre = _inproj(x2, w_main, w_gate, col_scale, layer, d_b, kv_prev, tiles["tm_in"])
    slab3 = slab.reshape(b, l, slab.shape[1])
    n_rows = 2 * h_a + h_b
    lf, gc, gr, qb_all, kb_all = _gates(gpre.reshape(b, l, GATE_COLS), b_gate, layer, h_a, n_rows,
                                        _bias_placement(2 * h_a, h_b))
    q_col = (2 * h_a * dk + 2 * h_a * dv) // dh

    if cache is None:
        c0 = jnp.zeros((b, h_a, dv, dk), F32)
        n0 = jnp.zeros((b, h_a, dk), F32)
        m0 = jnp.zeros((b, h_a), F32)
        hb = _fox_prompt(slab3, qb_all, kb_all, h_b, dh, q_col, tiles["tq"], tiles["fox_heads"])
    else:
        ck, cv, clf, c_all, n_all, m_all = cache
        c0, n0, m0 = c_all[layer], n_all[layer], m_all[layer]
        hb = _fox_sample(slab3, ck, cv, clf, layer, gc, h_b, dh, q_col, 2 * h_a)
    ha, c, n, mm = _mlstm(slab3, gc, gr, g_head, layer, c0, n0, m0, h_a, dk, dv, tiles["lc"])

    x1 = _outproj(ha.reshape(m, -1), hb.reshape(m, -1), x2, w_out, ln1_g, ln1_b, layer, alpha, tiles["tm_out"])
    y = _ffn(x1, w_up, w_down, ln2_g, ln2_b, layer, alpha, tiles["tm_ffn"], tiles["tf_ffn"])
    lfb = lf[:, :, 2 * h_a:n_rows]
    return y.reshape(b, l, d), (k_all, v_all), (lfb, c, n, mm)


def kernel(x_prompt, x_sample, cache_fox_k, cache_fox_v, cache_fox_logf, state_mlstm_c, state_mlstm_n,
           state_mlstm_m, w_in, b_gates, g_mlstm, w_out, ln1_g, ln1_b, w_up, w_down, ln2_g, ln2_b):
    depth, d_model, _ = w_in.shape
    _, _, _, h_b, dh = cache_fox_k.shape
    _, _, h_a, dv, dk = state_mlstm_c.shape
    d_a = h_a * dv
    d_b = h_b * dh
    dims = (h_a, dk, dv, h_b, dh)
    alpha = (2 * depth) ** 0.25
    widths = (h_a * dk, h_a * dk, d_a, d_a, h_a, h_a, d_b, d_b, d_b, h_b)
    offs = [0]
    for w in widths:
        offs.append(offs[-1] + w)
    n_gates = 2 * h_a + h_b

    tiles = dict(tm_in=512, tn_in=1024, tq=1024, fox_heads=2, lc=256, tm_out=512, tm_ffn=512, tf_ffn=1024)

    w_main, w_gate = _win_prep(w_in, ((offs[0], offs[4]), (offs[6], offs[9])), ((offs[4], offs[6]), (offs[9], offs[10])),
                               tiles["tn_in"])
    b_gate = jnp.pad(b_gates, ((0, 0), (0, GATE_COLS - n_gates))).reshape(depth, 1, GATE_COLS)
    n_main = w_main.shape[1] * w_main.shape[3]
    col_scale = jnp.concatenate([jnp.full((h_a * dk,), dk ** -0.5, F32),
                                 jnp.ones((offs[4] - offs[1],), F32),
                                 jnp.full((d_b,), dh ** -0.5 * LOG2E, F32),
                                 jnp.ones((2 * d_b,), F32)]).reshape(1, n_main)
    wts = (w_main, w_gate, col_scale, b_gate, g_mlstm.reshape(depth, 1, d_a), w_out.astype(BF16),
           ln1_g.reshape(depth, 1, d_model), ln1_b.reshape(depth, 1, d_model),
           w_up.astype(BF16), w_down.astype(BF16),
           ln2_g.reshape(depth, 1, d_model), ln2_b.reshape(depth, 1, d_model), alpha)
    cache = (cache_fox_k, cache_fox_v, cache_fox_logf, state_mlstm_c, state_mlstm_n, state_mlstm_m)


    yp, ys = x_prompt, x_sample
    kv_p = kv_s = None
    small_p, small_s = [], []
    for layer in range(depth):
        yp, kv_p, st_p = _layer(yp, layer, None, kv_p, wts, dims, tiles)
        ys, kv_s, st_s = _layer(ys, layer, cache, kv_s, wts, dims, tiles)
        small_p.append(st_p)
        small_s.append(st_s)
    stack = lambda states, i: jnp.stack([s[i] for s in states], axis=0)
    bp, lp, _ = x_prompt.shape
    bs, ls, _ = x_sample.shape
    return ((yp, ys)
            + tuple(a.reshape(depth, bp, lp, h_b, dh) for a in kv_p) + tuple(stack(small_p, i) for i in range(4))
            + tuple(a.reshape(depth, bs, ls, h_b, dh) for a in kv_s) + tuple(stack(small_s, i) for i in range(4)))
```

```python
import functools
import math

import jax
import jax.numpy as jnp
from jax import lax
from jax.experimental import pallas as pl
from jax.experimental.pallas import tpu as pltpu

F32 = jnp.float32
BF16 = jnp.bfloat16

LN_EPS = 1e-5
HEAD_NORM_EPS = 1e-6
LOG2E = math.log2(math.e)
LANES = 128
GATE_COLS = LANES
VMEM_LIMIT = 56 * 1024 * 1024


def _cparams(sem):
    return pltpu.CompilerParams(dimension_semantics=sem, vmem_limit_bytes=VMEM_LIMIT)


def _tile(n, pref):
    t = min(n, pref)
    assert n % t == 0, (n, pref)
    return t


def _log_sigmoid(g):
    return jnp.minimum(g, 0.0) - jnp.log1p(jnp.exp(-jnp.abs(g)))


def _dot_nt(a, b):
    return lax.dot_general(a, b, (((1,), (1,)), ((), ())), preferred_element_type=F32)


def _dot_tn(a, b):
    return lax.dot_general(a, b, (((0,), (0,)), ((), ())), preferred_element_type=F32)


def _layer_norm(z, g, b):
    mu = jnp.mean(z, axis=-1, keepdims=True)
    zc = z - mu
    var = jnp.mean(zc * zc, axis=-1, keepdims=True)
    return zc * lax.rsqrt(var + LN_EPS) * g + b


def _prefix_sum_rows(v):
    row = lax.broadcasted_iota(jnp.int32, v.shape, 0)
    s = 1
    while s < v.shape[0]:
        v = v + jnp.where(row >= s, pltpu.roll(v, s, 0), 0.0)
        s *= 2
    return v


def _split3(f):
    hi = f.astype(BF16).astype(F32)
    r = f - hi
    mid = r.astype(BF16).astype(F32)
    lo = (r - mid).astype(BF16).astype(F32)
    return hi, mid, lo


def _bias_lanes(f, query_side):
    hi, mid, lo = _split3(f)
    lane = lax.broadcasted_iota(jnp.int32, (f.shape[0], LANES), 1)
    if query_side:
        a = jnp.where(lane == 0, hi, jnp.where(lane == 1, mid, jnp.where(lane == 2, lo, jnp.where(lane < 6, 1.0, 0.0))))
    else:
        a = jnp.where(lane == 3, -hi, jnp.where(lane == 4, -mid, jnp.where(lane == 5, -lo, jnp.where(lane < 3, 1.0, 0.0))))
    return a.astype(BF16)


def _pick_lane(x, idx):
    lane = lax.broadcasted_iota(jnp.int32, x.shape, 1)
    return jnp.sum(jnp.where(lane == idx, x, 0.0), axis=1, keepdims=True)


def _win_prep_kernel(w_ref, main_ref, gate_ref, *, main_cols, gate_cols):
    w = w_ref[...]
    main = jnp.concatenate([w[:, a:b] for a, b in main_cols], axis=1).astype(BF16)
    nj, _, tn = main_ref.shape
    for j in range(nj):
        main_ref[j] = main[:, j * tn:(j + 1) * tn]
    gates = [w[:, a:b] for a, b in gate_cols]
    pad = GATE_COLS - sum(b - a for a, b in gate_cols)
    gate_ref[...] = jnp.concatenate(gates + [jnp.zeros((w.shape[0], pad), F32)], axis=1).astype(BF16)


def _win_prep(w_in, main_cols, gate_cols, tn):
    depth, d, n_in = w_in.shape
    n_main = sum(b - a for a, b in main_cols)
    nj = n_main // tn
    tk = _tile(d, 256)
    return pl.pallas_call(
        functools.partial(_win_prep_kernel, main_cols=main_cols, gate_cols=gate_cols),
        grid=(depth, d // tk),
        in_specs=[pl.BlockSpec((None, tk, n_in), lambda l, t: (l, t, 0))],
        out_specs=[pl.BlockSpec((None, nj, tk, tn), lambda l, t: (l, 0, t, 0)),
                   pl.BlockSpec((None, tk, GATE_COLS), lambda l, t: (l, t, 0))],
        out_shape=[jax.ShapeDtypeStruct((depth, nj, d, tn), BF16),
                   jax.ShapeDtypeStruct((depth, d, GATE_COLS), BF16)],
        compiler_params=_cparams(("parallel", "parallel")),
        name="win_prep",
    )(w_in)


def _inproj_kernel(*refs, jk0, jv0, aliased):
    x_ref, w_ref, wg_ref, cs_ref = refs[:4]
    slab_ref, k_ref, v_ref, g_ref, xb_ref = refs[-5:]
    del aliased
    j = pl.program_id(1)

    @pl.when(j == 0)
    def _():
        xb = x_ref[...].astype(BF16)
        xb_ref[...] = xb
        g_ref[...] = jnp.dot(xb, wg_ref[...], preferred_element_type=F32)

    def tile(f32_ref):
        acc = jnp.dot(xb_ref[...], w_ref[...], preferred_element_type=F32)
        slab_ref[...] = (acc * cs_ref[...]).astype(BF16)
        if f32_ref is not None:
            f32_ref[...] = acc

    pl.when(j < jk0)(lambda: tile(None))
    pl.when((j >= jk0) & (j < jv0))(lambda: tile(k_ref))
    pl.when(j >= jv0)(lambda: tile(v_ref))


def _inproj(x2, w_main, w_gate, col_scale, layer, d_b, kv_prev, tm_pref):
    m, d = x2.shape
    depth, nj, _, tn = w_main.shape
    n = nj * tn
    tm = _tile(m, tm_pref)
    assert d_b % tn == 0
    nkv = d_b // tn
    jk0 = nj - 2 * nkv
    jv0 = nj - nkv
    aliased = kv_prev is not None
    in_specs = [pl.BlockSpec((tm, d), lambda i, j: (i, 0)),
                pl.BlockSpec((None, None, d, tn), lambda i, j: (layer, j, 0, 0)),
                pl.BlockSpec((None, d, GATE_COLS), lambda i, j: (layer, 0, 0)),
                pl.BlockSpec((1, tn), lambda i, j: (0, j))]
    args = [x2, w_main, w_gate, col_scale]
    aliases = {}
    if aliased:
        in_specs += [pl.BlockSpec(memory_space=pl.ANY), pl.BlockSpec(memory_space=pl.ANY)]
        args += list(kv_prev)
        aliases = {4: 1, 5: 2}
    return pl.pallas_call(
        functools.partial(_inproj_kernel, jk0=jk0, jv0=jv0, aliased=aliased),
        grid=(m // tm, nj),
        in_specs=in_specs,
        out_specs=[pl.BlockSpec((tm, tn), lambda i, j: (i, j)),
                   pl.BlockSpec((None, tm, tn), lambda i, j: (layer, i, jnp.clip(j - jk0, 0, nkv - 1))),
                   pl.BlockSpec((None, tm, tn), lambda i, j: (layer, i, jnp.clip(j - jv0, 0, nkv - 1))),
                   pl.BlockSpec((tm, GATE_COLS), lambda i, j: (i, 0))],
        out_shape=[jax.ShapeDtypeStruct((m, n), BF16),
                   jax.ShapeDtypeStruct((depth, m, d_b), F32),
                   jax.ShapeDtypeStruct((depth, m, d_b), F32),
                   jax.ShapeDtypeStruct((m, GATE_COLS), F32)],
        scratch_shapes=[pltpu.VMEM((tm, d), BF16)],
        input_output_aliases=aliases,
        compiler_params=_cparams(("parallel", "arbitrary")),
        name="inproj",
    )(*args)


BIAS_LANES = 6


def _bias_placement(lane0, n_heads):
    import numpy as np
    place = np.zeros((3 * LANES, 2 * LANES), np.float32)
    ones = np.zeros((2, LANES), np.float32)
    for h in range(n_heads):
        for t in range(3):
            place[t * LANES + lane0 + h, BIAS_LANES * h + t] = 1.0
            place[t * LANES + lane0 + h, LANES + BIAS_LANES * h + 3 + t] = -1.0
            ones[0, BIAS_LANES * h + 3 + t] = 1.0
            ones[1, BIAS_LANES * h + t] = 1.0
    return jnp.asarray(place, BF16), jnp.asarray(ones, F32)


def _gates_kernel(gp_ref, bg_ref, place_ref, ones_ref, lf_ref, gc_ref, gr_ref, qb_ref, kb_ref, carry_ref,
                  *, n_raw, n_rows):
    t = pl.program_id(1)

    @pl.when(t == 0)
    def _():
        carry_ref[...] = jnp.zeros_like(carry_ref)

    g = gp_ref[...] + bg_ref[...]
    tl = g.shape[0]
    lane = lax.broadcasted_iota(jnp.int32, g.shape, 1)
    raw = lane < n_raw
    val = jnp.where(raw, g, _log_sigmoid(g))
    lf_ref[...] = val
    c = _prefix_sum_rows(val) + carry_ref[...]
    carry_ref[...] = c[tl - 1:tl, :]
    out = jnp.where(raw, g, c)
    gc_ref[...] = out
    gr_ref[...] = out.T[:n_rows, :]
    split = jnp.concatenate(_split3(c * LOG2E), axis=1).astype(BF16)
    moved = jnp.dot(split, place_ref[...], preferred_element_type=F32)
    qb_ref[...] = (moved[:, :LANES] + ones_ref[0:1, :]).astype(BF16)
    kb_ref[...] = (moved[:, LANES:] + ones_ref[1:2, :]).astype(BF16)


def _gates(gpre3, b_gate, layer, n_raw, n_rows, placement):
    b, l, _ = gpre3.shape
    tl = _tile(l, 512)
    place, ones = placement
    const = lambda i, t: (0, 0)
    tile = pl.BlockSpec((None, tl, GATE_COLS), lambda i, t: (i, t, 0))
    return pl.pallas_call(
        functools.partial(_gates_kernel, n_raw=n_raw, n_rows=n_rows),
        grid=(b, l // tl),
        in_specs=[tile,
                  pl.BlockSpec((None, 1, GATE_COLS), lambda i, t: (layer, 0, 0)),
                  pl.BlockSpec(place.shape, const),
                  pl.BlockSpec(ones.shape, const)],
        out_specs=[tile, tile, pl.BlockSpec((None, n_rows, tl), lambda i, t: (i, 0, t)), tile, tile],
        out_shape=[jax.ShapeDtypeStruct((b, l, GATE_COLS), F32),
                   jax.ShapeDtypeStruct((b, l, GATE_COLS), F32),
                   jax.ShapeDtypeStruct((b, n_rows, l), F32),
                   jax.ShapeDtypeStruct((b, l, GATE_COLS), BF16),
                   jax.ShapeDtypeStruct((b, l, GATE_COLS), BF16)],
        scratch_shapes=[pltpu.VMEM((1, GATE_COLS), F32)],
        compiler_params=_cparams(("parallel", "arbitrary")),
        name="gates",
    )(gpre3, b_gate, place, ones)


def _mlstm_kernel(q_ref, k_ref, v_ref, o_ref, gc_ref, gr_ref, gh_ref, c0_ref, n0_ref, m0_ref,
                  h_ref, c_ref, n_ref, m_ref, cta_s, m_s, fp_s, *, n_heads, dk, dv):
    c_idx = pl.program_id(1)
    n_chunks = pl.num_programs(1)
    lc = q_ref.shape[0]

    @pl.when(c_idx == 0)
    def _():
        lane = lax.broadcasted_iota(jnp.int32, (dk, LANES), 1)
        for h in range(n_heads):
            cta_s[h, :, :dv] = c0_ref[h].T
            n_cols = jnp.broadcast_to(n0_ref[h:h + 1, :], (LANES, dk)).T
            cta_s[h, :, dv:] = jnp.where(lane == 0, n_cols, 0.0)
        m_s[...] = m0_ref[...]
        fp_s[...] = jnp.zeros_like(fp_s)

    gc = gc_ref[...]
    gr = gr_ref[...]
    fp = fp_s[...]
    m_prev = m_s[...]
    row1 = lax.broadcasted_iota(jnp.int32, gc.shape, 0)
    b = gc - fp
    ig = pltpu.roll(gc, n_heads, 1)
    run = ig - b
    step = 1
    while step < lc:
        run = jnp.maximum(run, jnp.where(row1 >= step, pltpu.roll(run, step, 0), -jnp.inf))
        step *= 2
    g = b + m_prev
    m_t = jnp.maximum(g, b + run)
    w_inter = jnp.exp(g - m_t)
    e_neg = jnp.exp(-m_t)
    m_new = m_t[lc - 1:lc, :]
    b_last = b[lc - 1:lc, :]
    a_all = jnp.exp(b_last - b + ig - m_new)
    decay = jnp.exp(b_last + m_prev - m_new)
    m_s[...] = m_new
    fp_s[...] = gc[lc - 1:lc, :]

    row = lax.broadcasted_iota(jnp.int32, (lc, lc), 0)
    col = lax.broadcasted_iota(jnp.int32, (lc, lc), 1)
    causal = col <= row
    lane = lax.broadcasted_iota(jnp.int32, (lc, LANES), 1)
    ones_col = jnp.where(lane == 0, 1.0, 0.0).astype(BF16)
    for h in range(n_heads):
        hl = n_heads + h
        q = q_ref[:, h * dk:(h + 1) * dk]
        k = k_ref[:, h * dk:(h + 1) * dk]
        va = jnp.concatenate([v_ref[:, h * dv:(h + 1) * dv], ones_col], axis=1)
        ig_row = gr[h:h + 1, :]
        b_row = gr[hl:hl + 1, :] - fp[:, hl:hl + 1]
        d = jnp.where(causal, b[:, hl:hl + 1] - b_row + ig_row, -jnp.inf)
        s = _dot_nt(q, k) * jnp.exp(d - m_t[:, hl:hl + 1])
        cta = cta_s[h]
        both = (w_inter[:, hl:hl + 1] * jnp.dot(q, cta.astype(BF16), preferred_element_type=F32)
                + jnp.dot(s.astype(BF16), va, preferred_element_type=F32))
        hh = both[:, :dv] / jnp.maximum(jnp.abs(both[:, dv:dv + 1]), e_neg[:, hl:hl + 1])

        ka = (k.astype(F32) * a_all[:, hl:hl + 1]).astype(BF16)
        cta_s[h] = decay[:, hl:hl + 1] * cta + _dot_tn(ka, va)

        hn = hh * lax.rsqrt(jnp.mean(hh * hh, axis=1, keepdims=True) + HEAD_NORM_EPS)
        og = o_ref[:, h * dv:(h + 1) * dv].astype(F32)
        hn = hn * gh_ref[:, h * dv:(h + 1) * dv] * jax.nn.sigmoid(og)
        h_ref[:, h * dv:(h + 1) * dv] = hn.astype(h_ref.dtype)

    @pl.when(c_idx == n_chunks - 1)
    def _():
        for h in range(n_heads):
            c_ref[h] = cta_s[h, :, :dv].T
            n_ref[h:h + 1, :] = cta_s[h, :, dv:].T[0:1, :]
        m_ref[...] = m_s[...]


def _mlstm(slab3, gc, gr, g_head, layer, c0, n0, m0, n_heads, dk, dv, lc_pref):
    b, l, _ = slab3.shape
    lc = _tile(l, lc_pref)
    dq = n_heads * dk
    da = n_heads * dv
    assert da == 2 * dq and dk == LANES
    m0r = jnp.pad(m0, ((0, 0), (n_heads, GATE_COLS - 2 * n_heads))).reshape(b, 1, GATE_COLS)
    bmap = lambda i, c: (i, 0, 0)
    outs = pl.pallas_call(
        functools.partial(_mlstm_kernel, n_heads=n_heads, dk=dk, dv=dv),
        grid=(b, l // lc),
        in_specs=[pl.BlockSpec((None, lc, dq), lambda i, c: (i, c, 0)),
                  pl.BlockSpec((None, lc, dq), lambda i, c: (i, c, 1)),
                  pl.BlockSpec((None, lc, da), lambda i, c: (i, c, 1)),
                  pl.BlockSpec((None, lc, da), lambda i, c: (i, c, 2)),
                  pl.BlockSpec((None, lc, GATE_COLS), lambda i, c: (i, c, 0)),
                  pl.BlockSpec((None, gr.shape[1], lc), lambda i, c: (i, 0, c)),
                  pl.BlockSpec((None, 1, da), lambda i, c: (layer, 0, 0)),
                  pl.BlockSpec((None, n_heads, dv, dk), lambda i, c: (i, 0, 0, 0)),
                  pl.BlockSpec((None, n_heads, dk), bmap),
                  pl.BlockSpec((None, 1, GATE_COLS), bmap)],
        out_specs=[pl.BlockSpec((None, lc, da), lambda i, c: (i, c, 0)),
                   pl.BlockSpec((None, n_heads, dv, dk), lambda i, c: (i, 0, 0, 0)),
                   pl.BlockSpec((None, n_heads, dk), bmap),
                   pl.BlockSpec((None, 1, GATE_COLS), bmap)],
        out_shape=[jax.ShapeDtypeStruct((b, l, da), BF16),
                   jax.ShapeDtypeStruct((b, n_heads, dv, dk), F32),
                   jax.ShapeDtypeStruct((b, n_heads, dk), F32),
                   jax.ShapeDtypeStruct((b, 1, GATE_COLS), F32)],
        scratch_shapes=[pltpu.VMEM((n_heads, dk, dv + LANES), F32),
                        pltpu.VMEM((1, GATE_COLS), F32),
                        pltpu.VMEM((1, GATE_COLS), F32)],
        compiler_params=_cparams(("parallel", "arbitrary")),
        name="mlstm",
    )(slab3, slab3, slab3, slab3, gc, gr, g_head, c0, n0, m0r)
    h_a, c, n, m = outs
    return h_a, c, n, m[:, 0, n_heads:2 * n_heads]


def _fox_prompt_kernel(q_ref, k_ref, v_ref, qb_ref, kb_ref, *rest, tq, hp, cast_weights):
    if cast_weights:
        wu_ref, wd_ref, o_ref, wub_ref, wdb_ref = rest[:5]
        wub_ref[...] = wu_ref[...].astype(BF16)
        wdb_ref[...] = wd_ref[...].astype(BF16)
        rest = rest[5:]
    else:
        o_ref, rest = rest[0], rest[1:]
    qa_s, ka_s, va_s, p_s, m_s, al_s, acc_s = rest
    hg = pl.program_id(1)
    i = pl.program_id(2)
    dh = q_ref.shape[1] // hp

    @pl.when(i == 0)
    def _():
        kb = kb_ref[...]
        lane = lax.broadcasted_iota(jnp.int32, kb.shape, 1)
        ones_lane = jnp.where(lane == 0, 1.0, 0.0).astype(BF16)
        for u in range(hp):
            first = BIAS_LANES * (hg * hp + u)
            ka_s[u, :, :dh] = k_ref[:, u * dh:(u + 1) * dh]
            ka_s[u, :, dh:] = jnp.where((lane >= first) & (lane < first + BIAS_LANES), kb, jnp.zeros_like(kb))
            va_s[u, :, :dh] = v_ref[:, u * dh:(u + 1) * dh]
            va_s[u, :, dh:] = ones_lane

    for u in range(hp):
        qa_s[u, :, :dh] = q_ref[:, u * dh:(u + 1) * dh]
        qa_s[u, :, dh:] = qb_ref[...]

    def logits(u, j):
        start = pl.multiple_of(j * tq, tq)
        return _dot_nt(qa_s[u], ka_s[u, pl.ds(start, tq), :])

    def softmax(u, s, diagonal):
        if diagonal:
            row = lax.broadcasted_iota(jnp.int32, s.shape, 0)
            col = lax.broadcasted_iota(jnp.int32, s.shape, 1)
            s = jnp.where(col <= row, s, -jnp.inf)
            m_new = jnp.broadcast_to(jnp.max(s, axis=1, keepdims=True), (tq, LANES))
            al_s[u] = jnp.zeros_like(m_new)
        else:
            m = m_s[u]
            m_new = jnp.maximum(m, jnp.max(s, axis=1, keepdims=True))
            al_s[u] = jnp.exp2(m - m_new)
        m_s[u] = m_new
        p_s[u] = jnp.exp2(s - jnp.concatenate([m_new] * (tq // LANES), axis=1)).astype(BF16)

    def values(u, j):
        start = pl.multiple_of(j * tq, tq)
        pv = jnp.dot(p_s[u], va_s[u, pl.ds(start, tq), :], preferred_element_type=F32)
        return jnp.concatenate([al_s[u]] * (2 * dh // LANES), axis=1) * acc_s[u] + pv

    for u in range(hp):
        acc_s[u] = jnp.zeros(acc_s.shape[1:], F32)
        softmax(u, logits(u, i), True)

    def trip(j, _):
        prev = jnp.where(j == 0, i, j - 1)
        s = [logits(u, j) for u in range(hp)]
        for u in range(hp):
            acc_s[u] = values(u, prev)
        for u in range(hp):
            softmax(u, s[u], False)
        return 0

    lax.fori_loop(0, i, trip, 0)
    last = jnp.where(i == 0, i, i - 1)
    for u in range(hp):
        acc = values(u, last)
        o_ref[:, u * dh:(u + 1) * dh] = (acc[:, :dh] / acc[:, dh:dh + 1]).astype(o_ref.dtype)


def _fox_prompt(slab3, qb_all, kb_all, n_heads, dh, q_col, tq_pref, hp, mlp_w=None, layer=0):
    b, l, _ = slab3.shape
    tq = _tile(l, tq_pref)
    assert n_heads % hp == 0 and q_col % hp == 0
    qb, ng, w = q_col // hp, n_heads // hp, hp * dh
    nq = l // tq
    n_steps = b * ng * nq
    in_specs = [pl.BlockSpec((None, tq, w), lambda bi, h, i: (bi, i, qb + h)),
                pl.BlockSpec((None, l, w), lambda bi, h, i: (bi, 0, qb + ng + h)),
                pl.BlockSpec((None, l, w), lambda bi, h, i: (bi, 0, qb + 2 * ng + h)),
                pl.BlockSpec((None, tq, GATE_COLS), lambda bi, h, i: (bi, i, 0)),
                pl.BlockSpec((None, l, GATE_COLS), lambda bi, h, i: (bi, 0, 0))]
    out_specs = [pl.BlockSpec((None, tq, w), lambda bi, h, i: (bi, i, h))]
    out_shape = [jax.ShapeDtypeStruct((b, l, n_heads * dh), BF16)]
    args = [slab3, slab3, slab3, qb_all, kb_all]
    cast_weights = False
    if mlp_w is not None:
        w_up, w_down = mlp_w
        _, d, f = w_up.shape
        cast_weights = d % (16 * n_steps) == 0 and f % (16 * n_steps) == 0
    if cast_weights:
        ru, rd = d // n_steps, f // n_steps
        step = lambda bi, h, i: (bi * ng + h) * nq + i
        in_specs += [pl.BlockSpec((None, ru, f), lambda bi, h, i: (layer, step(bi, h, i), 0)),
                     pl.BlockSpec((None, rd, d), lambda bi, h, i: (layer, step(bi, h, i), 0))]
        out_specs += [pl.BlockSpec((ru, f), lambda bi, h, i: (step(bi, h, i), 0)),
                      pl.BlockSpec((rd, d), lambda bi, h, i: (step(bi, h, i), 0))]
        out_shape += [jax.ShapeDtypeStruct((d, f), BF16), jax.ShapeDtypeStruct((f, d), BF16)]
        args += [w_up, w_down]
    outs = pl.pallas_call(
        functools.partial(_fox_prompt_kernel, tq=tq, hp=hp, cast_weights=cast_weights),
        grid=(b, ng, nq),
        in_specs=in_specs,
        out_specs=out_specs,
        out_shape=out_shape,
        scratch_shapes=[pltpu.VMEM((hp, tq, 2 * dh), BF16), pltpu.VMEM((hp, l, 2 * dh), BF16),
                        pltpu.VMEM((hp, l, 2 * dh), BF16), pltpu.VMEM((hp, tq, tq), BF16),
                        pltpu.VMEM((hp, tq, LANES), F32), pltpu.VMEM((hp, tq, LANES), F32),
                        pltpu.VMEM((hp, tq, 2 * dh), F32)],
        compiler_params=_cparams(("parallel", "parallel", "arbitrary")),
        name="fox_prompt",
    )(*args)
    if cast_weights:
        return outs[0], (outs[1], outs[2])
    if mlp_w is not None:
        return outs[0], (mlp_w[0][layer].astype(BF16), mlp_w[1][layer].astype(BF16))
    return outs[0], None


def _fox_sample_kernel(q_ref, k_ref, v_ref, ck_ref, cv_ref, clf_ref, gc_ref, o_ref, *, n_heads, dh, lane0):
    l = q_ref.shape[0]
    cum = _prefix_sum_rows(clf_ref[...])
    past = cum.shape[0]
    gc = gc_ref[...]
    row = lax.broadcasted_iota(jnp.int32, (l, l), 0)
    col = lax.broadcasted_iota(jnp.int32, (l, l), 1)
    for h in range(n_heads):
        f_old = cum[:, h:h + 1]
        f_new = f_old[past - 1:past, :] + gc[:, lane0 + h:lane0 + h + 1]
        bias_new_k = _bias_lanes(f_new * LOG2E, False)
        qa = jnp.concatenate([q_ref[:, h * dh:(h + 1) * dh], _bias_lanes(f_new * LOG2E, True)], axis=1)
        ka_new = jnp.concatenate([k_ref[:, h * dh:(h + 1) * dh], bias_new_k], axis=1)
        ka_old = jnp.concatenate([ck_ref[:, h, :].astype(BF16), _bias_lanes(f_old * LOG2E, False)], axis=1)
        s_old = _dot_nt(qa, ka_old)
        s_new = jnp.where(col <= row, _dot_nt(qa, ka_new), -jnp.inf)
        m = jnp.maximum(jnp.max(s_old, axis=1, keepdims=True), jnp.max(s_new, axis=1, keepdims=True))
        p_old = jnp.exp2(s_old - m)
        p_new = jnp.exp2(s_new - m)
        den = jnp.sum(p_old, axis=1, keepdims=True) + jnp.sum(p_new, axis=1, keepdims=True)
        acc = (jnp.dot(p_old.astype(BF16), cv_ref[:, h, :].astype(BF16), preferred_element_type=F32)
               + jnp.dot(p_new.astype(BF16), v_ref[:, h * dh:(h + 1) * dh], preferred_element_type=F32))
        o_ref[:, h * dh:(h + 1) * dh] = (acc / den).astype(o_ref.dtype)


def _fox_sample(slab3, ck, cv, clf, layer, gc, n_heads, dh, q_col, lane0):
    b, l, _ = slab3.shape
    past = ck.shape[2]
    dd = n_heads * dh
    qb = q_col // n_heads
    return pl.pallas_call(
        functools.partial(_fox_sample_kernel, n_heads=n_heads, dh=dh, lane0=lane0),
        grid=(b,),
        in_specs=[pl.BlockSpec((None, l, dd), lambda bi: (bi, 0, qb)),
                  pl.BlockSpec((None, l, dd), lambda bi: (bi, 0, qb + 1)),
                  pl.BlockSpec((None, l, dd), lambda bi: (bi, 0, qb + 2)),
                  pl.BlockSpec((None, None, past, n_heads, dh), lambda bi: (layer, bi, 0, 0, 0)),
                  pl.BlockSpec((None, None, past, n_heads, dh), lambda bi: (layer, bi, 0, 0, 0)),
                  pl.BlockSpec((None, None, past, n_heads), lambda bi: (layer, bi, 0, 0)),
                  pl.BlockSpec((None, l, GATE_COLS), lambda bi: (bi, 0, 0))],
        out_specs=pl.BlockSpec((None, l, dd), lambda bi: (bi, 0, 0)),
        out_shape=jax.ShapeDtypeStruct((b, l, dd), BF16),
        compiler_params=_cparams(("parallel",)),
        name="fox_sample",
    )(slab3, slab3, slab3, ck, cv, clf, gc)


def _outproj_kernel(ha_ref, hb_ref, x_ref, wa_ref, wb_ref, g_ref, b_ref, y_ref, *, alpha):
    tm = x_ref.shape[0]
    tr = min(tm, 256)
    for r in range(tm // tr):
        rows = slice(r * tr, (r + 1) * tr)
        mix = (jnp.dot(ha_ref[rows, :], wa_ref[...], preferred_element_type=F32)
               + jnp.dot(hb_ref[rows, :], wb_ref[...], preferred_element_type=F32))
        y_ref[rows, :] = _layer_norm(alpha * x_ref[rows, :] + mix, g_ref[...], b_ref[...])


def _outproj(ha2, hb2, x2, w_out, ln_g, ln_b, layer, alpha, tm_pref):
    m, d = x2.shape
    da = ha2.shape[1]
    db = hb2.shape[1]
    assert da == db
    tm = _tile(m, tm_pref)
    vec = pl.BlockSpec((None, 1, d), lambda i: (layer, 0, 0))
    return pl.pallas_call(
        functools.partial(_outproj_kernel, alpha=alpha),
        grid=(m // tm,),
        in_specs=[pl.BlockSpec((tm, da), lambda i: (i, 0)),
                  pl.BlockSpec((tm, db), lambda i: (i, 0)),
                  pl.BlockSpec((tm, d), lambda i: (i, 0)),
                  pl.BlockSpec((None, da, d), lambda i: (layer, 0, 0)),
                  pl.BlockSpec((None, db, d), lambda i: (layer, 1, 0)),
                  vec, vec],
        out_specs=pl.BlockSpec((tm, d), lambda i: (i, 0)),
        out_shape=jax.ShapeDtypeStruct((m, d), F32),
        compiler_params=_cparams(("parallel",)),
        name="outproj_ln",
    )(ha2, hb2, x2, w_out, w_out, ln_g, ln_b)


def _ffn_kernel(x_ref, wu_ref, wd_ref, g_ref, b_ref, y_ref, xb_ref, acc_ref, *, alpha):
    j = pl.program_id(1)

    @pl.when(j == 0)
    def _():
        xb_ref[...] = x_ref[...].astype(BF16)
        acc_ref[...] = jnp.zeros_like(acc_ref)

    u = jnp.dot(xb_ref[...], wu_ref[...], preferred_element_type=F32)
    u = jnp.maximum(u, 0.0)
    acc_ref[...] += jnp.dot((u * u).astype(BF16), wd_ref[...], preferred_element_type=F32)

    @pl.when(j == pl.num_programs(1) - 1)
    def _():
        y_ref[...] = _layer_norm(alpha * x_ref[...] + acc_ref[...], g_ref[...], b_ref[...])


def _ffn(x2, w_up, w_down, ln_g, ln_b, layer, alpha, tm_pref, tf_pref):
    m, d = x2.shape
    f = w_up.shape[1]
    tm = _tile(m, tm_pref)
    tf = _tile(f, tf_pref)
    vec = pl.BlockSpec((None, 1, d), lambda i, j: (layer, 0, 0))
    return pl.pallas_call(
        functools.partial(_ffn_kernel, alpha=alpha),
        grid=(m // tm, f // tf),
        in_specs=[pl.BlockSpec((tm, d), lambda i, j: (i, 0)),
                  pl.BlockSpec((d, tf), lambda i, j: (0, j)),
                  pl.BlockSpec((tf, d), lambda i, j: (j, 0)),
                  vec, vec],
        out_specs=pl.BlockSpec((tm, d), lambda i, j: (i, 0)),
        out_shape=jax.ShapeDtypeStruct((m, d), F32),
        scratch_shapes=[pltpu.VMEM((tm, d), BF16), pltpu.VMEM((tm, d), F32)],
        compiler_params=_cparams(("parallel", "arbitrary")),
        name="ffn_ln",
    )(x2, w_up, w_down, ln_g, ln_b)


def _layer(x, layer, cache, kv_prev, mlp_w, wts, dims, tiles):
    h_a, dk, dv, h_b, dh = dims
    (w_main, w_gate, col_scale, b_gate, g_head, w_out, ln1_g, ln1_b, ln2_g, ln2_b, alpha) = wts
    b, l, d = x.shape
    m = b * l
    d_b = h_b * dh
    x2 = x.reshape(m, d)

    slab, k_all, v_all, gpre = _inproj(x2, w_main, w_gate, col_scale, layer, d_b, kv_prev, tiles["tm_in"])
    slab3 = slab.reshape(b, l, slab.shape[1])
    n_rows = 2 * h_a + h_b
    lf, gc, gr, qb_all, kb_all = _gates(gpre.reshape(b, l, GATE_COLS), b_gate, layer, h_a, n_rows,
                                        _bias_placement(2 * h_a, h_b))
    q_col = (2 * h_a * dk + 2 * h_a * dv) // dh

    if cache is None:
        c0 = jnp.zeros((b, h_a, dv, dk), F32)
        n0 = jnp.zeros((b, h_a, dk), F32)
        m0 = jnp.zeros((b, h_a), F32)
        hb, mlp_w = _fox_prompt(slab3, qb_all, kb_all, h_b, dh, q_col, tiles["tq"], tiles["fox_heads"], mlp_w, layer)
    else:
        ck, cv, clf, c_all, n_all, m_all = cache
        c0, n0, m0 = c_all[layer], n_all[layer], m_all[layer]
        hb = _fox_sample(slab3, ck, cv, clf, layer, gc, h_b, dh, q_col, 2 * h_a)
    ha, c, n, mm = _mlstm(slab3, gc, gr, g_head, layer, c0, n0, m0, h_a, dk, dv, tiles["lc"])

    x1 = _outproj(ha.reshape(m, -1), hb.reshape(m, -1), x2, w_out, ln1_g, ln1_b, layer, alpha, tiles["tm_out"])
    y = _ffn(x1, mlp_w[0], mlp_w[1], ln2_g, ln2_b, layer, alpha, tiles["tm_ffn"], tiles["tf_ffn"])
    lfb = lf[:, :, 2 * h_a:n_rows]
    return y.reshape(b, l, d), (k_all, v_all), (lfb, c, n, mm), mlp_w


def kernel(x_prompt, x_sample, cache_fox_k, cache_fox_v, cache_fox_logf, state_mlstm_c, state_mlstm_n,
           state_mlstm_m, w_in, b_gates, g_mlstm, w_out, ln1_g, ln1_b, w_up, w_down, ln2_g, ln2_b):
    depth, d_model, _ = w_in.shape
    _, _, _, h_b, dh = cache_fox_k.shape
    _, _, h_a, dv, dk = state_mlstm_c.shape
    d_a = h_a * dv
    d_b = h_b * dh
    dims = (h_a, dk, dv, h_b, dh)
    alpha = (2 * depth) ** 0.25
    widths = (h_a * dk, h_a * dk, d_a, d_a, h_a, h_a, d_b, d_b, d_b, h_b)
    offs = [0]
    for w in widths:
        offs.append(offs[-1] + w)
    n_gates = 2 * h_a + h_b

    tiles = dict(tm_in=512, tn_in=1024, tq=1024, fox_heads=2, lc=256, tm_out=512, tm_ffn=512, tf_ffn=1024)

    w_main, w_gate = _win_prep(w_in, ((offs[0], offs[4]), (offs[6], offs[9])), ((offs[4], offs[6]), (offs[9], offs[10])),
                               tiles["tn_in"])
    b_gate = jnp.pad(b_gates, ((0, 0), (0, GATE_COLS - n_gates))).reshape(depth, 1, GATE_COLS)
    n_main = w_main.shape[1] * w_main.shape[3]
    col_scale = jnp.concatenate([jnp.full((h_a * dk,), dk ** -0.5, F32),
                                 jnp.ones((offs[4] - offs[1],), F32),
                                 jnp.full((d_b,), dh ** -0.5 * LOG2E, F32),
                                 jnp.ones((2 * d_b,), F32)]).reshape(1, n_main)
    wts = (w_main, w_gate, col_scale, b_gate, g_mlstm.reshape(depth, 1, d_a), w_out.astype(BF16),
           ln1_g.reshape(depth, 1, d_model), ln1_b.reshape(depth, 1, d_model),
           ln2_g.reshape(depth, 1, d_model), ln2_b.reshape(depth, 1, d_model), alpha)
    cache = (cache_fox_k, cache_fox_v, cache_fox_logf, state_mlstm_c, state_mlstm_n, state_mlstm_m)


    yp, ys = x_prompt, x_sample
    kv_p = kv_s = None
    small_p, small_s = [], []
    for layer in range(depth):
        yp, kv_p, st_p, mlp_bf16 = _layer(yp, layer, None, kv_p, (w_up, w_down), wts, dims, tiles)
        ys, kv_s, st_s, _ = _layer(ys, layer, cache, kv_s, mlp_bf16, wts, dims, tiles)
        small_p.append(st_p)
        small_s.append(st_s)
    stack = lambda states, i: jnp.stack([s[i] for s in states], axis=0)
    bp, lp, _ = x_prompt.shape
    bs, ls, _ = x_sample.shape
    return ((yp, ys)
            + tuple(a.reshape(depth, bp, lp, h_b, dh) for a in kv_p) + tuple(stack(small_p, i) for i in range(4))
            + tuple(a.reshape(depth, bs, ls, h_b, dh) for a in kv_s) + tuple(stack(small_s, i) for i in range(4)))
```

```python
import functools
import math

import jax
import jax.numpy as jnp
from jax import lax
from jax.experimental import pallas as pl
from jax.experimental.pallas import tpu as pltpu

F32 = jnp.float32
BF16 = jnp.bfloat16

LN_EPS = 1e-5
HEAD_NORM_EPS = 1e-6
LOG2E = math.log2(math.e)
LANES = 128
GATE_COLS = LANES
VMEM_LIMIT = 56 * 1024 * 1024


def _cparams(sem):
    return pltpu.CompilerParams(dimension_semantics=sem, vmem_limit_bytes=VMEM_LIMIT)


def _tile(n, pref):
    t = min(n, pref)
    assert n % t == 0, (n, pref)
    return t


def _log_sigmoid(g):
    return jnp.minimum(g, 0.0) - jnp.log1p(jnp.exp(-jnp.abs(g)))


def _dot_nt(a, b):
    return lax.dot_general(a, b, (((1,), (1,)), ((), ())), preferred_element_type=F32)


def _dot_tn(a, b):
    return lax.dot_general(a, b, (((0,), (0,)), ((), ())), preferred_element_type=F32)


def _layer_norm(z, g, b):
    mu = jnp.mean(z, axis=-1, keepdims=True)
    zc = z - mu
    var = jnp.mean(zc * zc, axis=-1, keepdims=True)
    return zc * lax.rsqrt(var + LN_EPS) * g + b


def _prefix_sum_rows(v):
    row = lax.broadcasted_iota(jnp.int32, v.shape, 0)
    s = 1
    while s < v.shape[0]:
        v = v + jnp.where(row >= s, pltpu.roll(v, s, 0), 0.0)
        s *= 2
    return v


def _split3(f):
    hi = f.astype(BF16).astype(F32)
    r = f - hi
    mid = r.astype(BF16).astype(F32)
    lo = (r - mid).astype(BF16).astype(F32)
    return hi, mid, lo


def _bias_lanes(f, query_side):
    hi, mid, lo = _split3(f)
    lane = lax.broadcasted_iota(jnp.int32, (f.shape[0], LANES), 1)
    if query_side:
        a = jnp.where(lane == 0, hi, jnp.where(lane == 1, mid, jnp.where(lane == 2, lo, jnp.where(lane < 6, 1.0, 0.0))))
    else:
        a = jnp.where(lane == 3, -hi, jnp.where(lane == 4, -mid, jnp.where(lane == 5, -lo, jnp.where(lane < 3, 1.0, 0.0))))
    return a.astype(BF16)


def _pick_lane(x, idx):
    lane = lax.broadcasted_iota(jnp.int32, x.shape, 1)
    return jnp.sum(jnp.where(lane == idx, x, 0.0), axis=1, keepdims=True)


def _win_prep_kernel(w_ref, main_ref, gate_ref, *, main_cols, gate_cols):
    w = w_ref[...]
    main = jnp.concatenate([w[:, a:b] for a, b in main_cols], axis=1).astype(BF16)
    nj, _, tn = main_ref.shape
    for j in range(nj):
        main_ref[j] = main[:, j * tn:(j + 1) * tn]
    gates = [w[:, a:b] for a, b in gate_cols]
    pad = GATE_COLS - sum(b - a for a, b in gate_cols)
    gate_ref[...] = jnp.concatenate(gates + [jnp.zeros((w.shape[0], pad), F32)], axis=1).astype(BF16)


def _win_prep(w_in, main_cols, gate_cols, tn):
    depth, d, n_in = w_in.shape
    n_main = sum(b - a for a, b in main_cols)
    nj = n_main // tn
    tk = _tile(d, 256)
    return pl.pallas_call(
        functools.partial(_win_prep_kernel, main_cols=main_cols, gate_cols=gate_cols),
        grid=(depth, d // tk),
        in_specs=[pl.BlockSpec((None, tk, n_in), lambda l, t: (l, t, 0))],
        out_specs=[pl.BlockSpec((None, nj, tk, tn), lambda l, t: (l, 0, t, 0)),
                   pl.BlockSpec((None, tk, GATE_COLS), lambda l, t: (l, t, 0))],
        out_shape=[jax.ShapeDtypeStruct((depth, nj, d, tn), BF16),
                   jax.ShapeDtypeStruct((depth, d, GATE_COLS), BF16)],
        compiler_params=_cparams(("parallel", "parallel")),
        name="win_prep",
    )(w_in)


def _inproj_kernel(*refs, jk0, jv0, aliased):
    x_ref, w_ref, wg_ref, cs_ref = refs[:4]
    del aliased
    j = pl.program_id(1)
    if x_ref.dtype == BF16:
        slab_ref, k_ref, v_ref, g_ref = refs[-4:]
        xb_ref = x_ref

        @pl.when(j == 0)
        def _():
            g_ref[...] = jnp.dot(x_ref[...], wg_ref[...], preferred_element_type=F32)
    else:
        slab_ref, k_ref, v_ref, g_ref, xb_ref = refs[-5:]

        @pl.when(j == 0)
        def _():
            xb = x_ref[...].astype(BF16)
            xb_ref[...] = xb
            g_ref[...] = jnp.dot(xb, wg_ref[...], preferred_element_type=F32)

    def tile(f32_ref):
        acc = jnp.dot(xb_ref[...], w_ref[...], preferred_element_type=F32)
        slab_ref[...] = (acc * cs_ref[...]).astype(BF16)
        if f32_ref is not None:
            f32_ref[...] = acc

    pl.when(j < jk0)(lambda: tile(None))
    pl.when((j >= jk0) & (j < jv0))(lambda: tile(k_ref))
    pl.when(j >= jv0)(lambda: tile(v_ref))


def _inproj(x2, w_main, w_gate, col_scale, layer, d_b, kv_prev, tm_pref):
    m, d = x2.shape
    depth, nj, _, tn = w_main.shape
    n = nj * tn
    tm = _tile(m, tm_pref)
    assert d_b % tn == 0
    nkv = d_b // tn
    jk0 = nj - 2 * nkv
    jv0 = nj - nkv
    aliased = kv_prev is not None
    in_specs = [pl.BlockSpec((tm, d), lambda i, j: (i, 0)),
                pl.BlockSpec((None, None, d, tn), lambda i, j: (layer, j, 0, 0)),
                pl.BlockSpec((None, d, GATE_COLS), lambda i, j: (layer, 0, 0)),
                pl.BlockSpec((1, tn), lambda i, j: (0, j))]
    args = [x2, w_main, w_gate, col_scale]
    aliases = {}
    if aliased:
        in_specs += [pl.BlockSpec(memory_space=pl.ANY), pl.BlockSpec(memory_space=pl.ANY)]
        args += list(kv_prev)
        aliases = {4: 1, 5: 2}
    return pl.pallas_call(
        functools.partial(_inproj_kernel, jk0=jk0, jv0=jv0, aliased=aliased),
        grid=(m // tm, nj),
        in_specs=in_specs,
        out_specs=[pl.BlockSpec((tm, tn), lambda i, j: (i, j)),
                   pl.BlockSpec((None, tm, tn), lambda i, j: (layer, i, jnp.clip(j - jk0, 0, nkv - 1))),
                   pl.BlockSpec((None, tm, tn), lambda i, j: (layer, i, jnp.clip(j - jv0, 0, nkv - 1))),
                   pl.BlockSpec((tm, GATE_COLS), lambda i, j: (i, 0))],
        out_shape=[jax.ShapeDtypeStruct((m, n), BF16),
                   jax.ShapeDtypeStruct((depth, m, d_b), F32),
                   jax.ShapeDtypeStruct((depth, m, d_b), F32),
                   jax.ShapeDtypeStruct((m, GATE_COLS), F32)],
        scratch_shapes=[] if x2.dtype == BF16 else [pltpu.VMEM((tm, d), BF16)],
        input_output_aliases=aliases,
        compiler_params=_cparams(("parallel", "arbitrary")),
        name="inproj",
    )(*args)


BIAS_LANES = 6


def _bias_placement(lane0, n_heads):
    import numpy as np
    place = np.zeros((3 * LANES, 2 * LANES), np.float32)
    ones = np.zeros((2, LANES), np.float32)
    for h in range(n_heads):
        for t in range(3):
            place[t * LANES + lane0 + h, BIAS_LANES * h + t] = 1.0
            place[t * LANES + lane0 + h, LANES + BIAS_LANES * h + 3 + t] = -1.0
            ones[0, BIAS_LANES * h + 3 + t] = 1.0
            ones[1, BIAS_LANES * h + t] = 1.0
    return jnp.asarray(place, BF16), jnp.asarray(ones, F32)


def _gates_kernel(gp_ref, bg_ref, place_ref, ones_ref, lf_ref, gc_ref, gr_ref, qb_ref, kb_ref, carry_ref,
                  *, n_raw, n_rows):
    t = pl.program_id(1)

    @pl.when(t == 0)
    def _():
        carry_ref[...] = jnp.zeros_like(carry_ref)

    g = gp_ref[...] + bg_ref[...]
    tl = g.shape[0]
    lane = lax.broadcasted_iota(jnp.int32, g.shape, 1)
    raw = lane < n_raw
    val = jnp.where(raw, g, _log_sigmoid(g))
    lf_ref[...] = val
    c = _prefix_sum_rows(val) + carry_ref[...]
    carry_ref[...] = c[tl - 1:tl, :]
    out = jnp.where(raw, g, c)
    gc_ref[...] = out
    gr_ref[...] = out.T[:n_rows, :]
    split = jnp.concatenate(_split3(c * LOG2E), axis=1).astype(BF16)
    moved = jnp.dot(split, place_ref[...], preferred_element_type=F32)
    qb_ref[...] = (moved[:, :LANES] + ones_ref[0:1, :]).astype(BF16)
    kb_ref[...] = (moved[:, LANES:] + ones_ref[1:2, :]).astype(BF16)


def _gates(gpre3, b_gate, layer, n_raw, n_rows, placement):
    b, l, _ = gpre3.shape
    tl = _tile(l, 512)
    place, ones = placement
    const = lambda i, t: (0, 0)
    tile = pl.BlockSpec((None, tl, GATE_COLS), lambda i, t: (i, t, 0))
    return pl.pallas_call(
        functools.partial(_gates_kernel, n_raw=n_raw, n_rows=n_rows),
        grid=(b, l // tl),
        in_specs=[tile,
                  pl.BlockSpec((None, 1, GATE_COLS), lambda i, t: (layer, 0, 0)),
                  pl.BlockSpec(place.shape, const),
                  pl.BlockSpec(ones.shape, const)],
        out_specs=[tile, tile, pl.BlockSpec((None, n_rows, tl), lambda i, t: (i, 0, t)), tile, tile],
        out_shape=[jax.ShapeDtypeStruct((b, l, GATE_COLS), F32),
                   jax.ShapeDtypeStruct((b, l, GATE_COLS), F32),
                   jax.ShapeDtypeStruct((b, n_rows, l), F32),
                   jax.ShapeDtypeStruct((b, l, GATE_COLS), BF16),
                   jax.ShapeDtypeStruct((b, l, GATE_COLS), BF16)],
        scratch_shapes=[pltpu.VMEM((1, GATE_COLS), F32)],
        compiler_params=_cparams(("parallel", "arbitrary")),
        name="gates",
    )(gpre3, b_gate, place, ones)


def _mlstm_kernel(q_ref, k_ref, v_ref, o_ref, gc_ref, gr_ref, gh_ref, c0_ref, n0_ref, m0_ref,
                  h_ref, c_ref, n_ref, m_ref, cta_s, m_s, fp_s, *, n_heads, dk, dv):
    c_idx = pl.program_id(1)
    n_chunks = pl.num_programs(1)
    lc = q_ref.shape[0]

    @pl.when(c_idx == 0)
    def _():
        lane = lax.broadcasted_iota(jnp.int32, (dk, LANES), 1)
        for h in range(n_heads):
            cta_s[h, :, :dv] = c0_ref[h].T
            n_cols = jnp.broadcast_to(n0_ref[h:h + 1, :], (LANES, dk)).T
            cta_s[h, :, dv:] = jnp.where(lane == 0, n_cols, 0.0)
        m_s[...] = m0_ref[...]
        fp_s[...] = jnp.zeros_like(fp_s)

    gc = gc_ref[...]
    gr = gr_ref[...]
    fp = fp_s[...]
    m_prev = m_s[...]
    row1 = lax.broadcasted_iota(jnp.int32, gc.shape, 0)
    b = gc - fp
    ig = pltpu.roll(gc, n_heads, 1)
    run = ig - b
    step = 1
    while step < lc:
        run = jnp.maximum(run, jnp.where(row1 >= step, pltpu.roll(run, step, 0), -jnp.inf))
        step *= 2
    g = b + m_prev
    m_t = jnp.maximum(g, b + run)
    w_inter = jnp.exp(g - m_t)
    e_neg = jnp.exp(-m_t)
    m_new = m_t[lc - 1:lc, :]
    b_last = b[lc - 1:lc, :]
    a_all = jnp.exp(b_last - b + ig - m_new)
    decay = jnp.exp(b_last + m_prev - m_new)
    m_s[...] = m_new
    fp_s[...] = gc[lc - 1:lc, :]

    row = lax.broadcasted_iota(jnp.int32, (lc, lc), 0)
    col = lax.broadcasted_iota(jnp.int32, (lc, lc), 1)
    causal = col <= row
    lane = lax.broadcasted_iota(jnp.int32, (lc, LANES), 1)
    ones_col = jnp.where(lane == 0, 1.0, 0.0).astype(BF16)
    for h in range(n_heads):
        hl = n_heads + h
        q = q_ref[:, h * dk:(h + 1) * dk]
        k = k_ref[:, h * dk:(h + 1) * dk]
        va = jnp.concatenate([v_ref[:, h * dv:(h + 1) * dv], ones_col], axis=1)
        ig_row = gr[h:h + 1, :]
        b_row = gr[hl:hl + 1, :] - fp[:, hl:hl + 1]
        d = jnp.where(causal, b[:, hl:hl + 1] - b_row + ig_row, -jnp.inf)
        s = _dot_nt(q, k) * jnp.exp(d - m_t[:, hl:hl + 1])
        cta = cta_s[h]
        both = (w_inter[:, hl:hl + 1] * jnp.dot(q, cta.astype(BF16), preferred_element_type=F32)
                + jnp.dot(s.astype(BF16), va, preferred_element_type=F32))
        hh = both[:, :dv] / jnp.maximum(jnp.abs(both[:, dv:dv + 1]), e_neg[:, hl:hl + 1])

        ka = (k.astype(F32) * a_all[:, hl:hl + 1]).astype(BF16)
        cta_s[h] = decay[:, hl:hl + 1] * cta + _dot_tn(ka, va)

        hn = hh * lax.rsqrt(jnp.mean(hh * hh, axis=1, keepdims=True) + HEAD_NORM_EPS)
        og = o_ref[:, h * dv:(h + 1) * dv].astype(F32)
        hn = hn * gh_ref[:, h * dv:(h + 1) * dv] * jax.nn.sigmoid(og)
        h_ref[:, h * dv:(h + 1) * dv] = hn.astype(h_ref.dtype)

    @pl.when(c_idx == n_chunks - 1)
    def _():
        for h in range(n_heads):
            c_ref[h] = cta_s[h, :, :dv].T
            n_ref[h:h + 1, :] = cta_s[h, :, dv:].T[0:1, :]
        m_ref[...] = m_s[...]


def _mlstm(slab3, gc, gr, g_head, layer, c0, n0, m0, n_heads, dk, dv, lc_pref):
    b, l, _ = slab3.shape
    lc = _tile(l, lc_pref)
    dq = n_heads * dk
    da = n_heads * dv
    assert da == 2 * dq and dk == LANES
    m0r = jnp.pad(m0, ((0, 0), (n_heads, GATE_COLS - 2 * n_heads))).reshape(b, 1, GATE_COLS)
    bmap = lambda i, c: (i, 0, 0)
    outs = pl.pallas_call(
        functools.partial(_mlstm_kernel, n_heads=n_heads, dk=dk, dv=dv),
        grid=(b, l // lc),
        in_specs=[pl.BlockSpec((None, lc, dq), lambda i, c: (i, c, 0)),
                  pl.BlockSpec((None, lc, dq), lambda i, c: (i, c, 1)),
                  pl.BlockSpec((None, lc, da), lambda i, c: (i, c, 1)),
                  pl.BlockSpec((None, lc, da), lambda i, c: (i, c, 2)),
                  pl.BlockSpec((None, lc, GATE_COLS), lambda i, c: (i, c, 0)),
                  pl.BlockSpec((None, gr.shape[1], lc), lambda i, c: (i, 0, c)),
                  pl.BlockSpec((None, 1, da), lambda i, c: (layer, 0, 0)),
                  pl.BlockSpec((None, n_heads, dv, dk), lambda i, c: (i, 0, 0, 0)),
                  pl.BlockSpec((None, n_heads, dk), bmap),
                  pl.BlockSpec((None, 1, GATE_COLS), bmap)],
        out_specs=[pl.BlockSpec((None, lc, da), lambda i, c: (i, c, 0)),
                   pl.BlockSpec((None, n_heads, dv, dk), lambda i, c: (i, 0, 0, 0)),
                   pl.BlockSpec((None, n_heads, dk), bmap),
                   pl.BlockSpec((None, 1, GATE_COLS), bmap)],
        out_shape=[jax.ShapeDtypeStruct((b, l, da), BF16),
                   jax.ShapeDtypeStruct((b, n_heads, dv, dk), F32),
                   jax.ShapeDtypeStruct((b, n_heads, dk), F32),
                   jax.ShapeDtypeStruct((b, 1, GATE_COLS), F32)],
        scratch_shapes=[pltpu.VMEM((n_heads, dk, dv + LANES), F32),
                        pltpu.VMEM((1, GATE_COLS), F32),
                        pltpu.VMEM((1, GATE_COLS), F32)],
        compiler_params=_cparams(("parallel", "arbitrary")),
        name="mlstm",
    )(slab3, slab3, slab3, slab3, gc, gr, g_head, c0, n0, m0r)
    h_a, c, n, m = outs
    return h_a, c, n, m[:, 0, n_heads:2 * n_heads]


def _fox_prompt_kernel(q_ref, k_ref, v_ref, qb_ref, kb_ref, *rest, tq, hp, cast_weights):
    if cast_weights:
        wu_ref, wd_ref, o_ref, wub_ref, wdb_ref = rest[:5]
        wub_ref[...] = wu_ref[...].astype(BF16)
        wdb_ref[...] = wd_ref[...].astype(BF16)
        rest = rest[5:]
    else:
        o_ref, rest = rest[0], rest[1:]
    qa_s, ka_s, va_s, p_s, m_s, al_s, acc_s = rest
    hg = pl.program_id(1)
    i = pl.program_id(2)
    dh = q_ref.shape[1] // hp

    @pl.when(i == 0)
    def _():
        kb = kb_ref[...]
        lane = lax.broadcasted_iota(jnp.int32, kb.shape, 1)
        ones_lane = jnp.where(lane == 0, 1.0, 0.0).astype(BF16)
        for u in range(hp):
            first = BIAS_LANES * (hg * hp + u)
            ka_s[u, :, :dh] = k_ref[:, u * dh:(u + 1) * dh]
            ka_s[u, :, dh:] = jnp.where((lane >= first) & (lane < first + BIAS_LANES), kb, jnp.zeros_like(kb))
            va_s[u, :, :dh] = v_ref[:, u * dh:(u + 1) * dh]
            va_s[u, :, dh:] = ones_lane

    for u in range(hp):
        qa_s[u, :, :dh] = q_ref[:, u * dh:(u + 1) * dh]
        qa_s[u, :, dh:] = qb_ref[...]

    def logits(u, j):
        start = pl.multiple_of(j * tq, tq)
        return _dot_nt(qa_s[u], ka_s[u, pl.ds(start, tq), :])

    def softmax(u, s, diagonal):
        if diagonal:
            row = lax.broadcasted_iota(jnp.int32, s.shape, 0)
            col = lax.broadcasted_iota(jnp.int32, s.shape, 1)
            s = jnp.where(col <= row, s, -jnp.inf)
            m_new = jnp.broadcast_to(jnp.max(s, axis=1, keepdims=True), (tq, LANES))
            al_s[u] = jnp.zeros_like(m_new)
        else:
            m = m_s[u]
            m_new = jnp.maximum(m, jnp.max(s, axis=1, keepdims=True))
            al_s[u] = jnp.exp2(m - m_new)
        m_s[u] = m_new
        p_s[u] = jnp.exp2(s - jnp.concatenate([m_new] * (tq // LANES), axis=1)).astype(BF16)

    def values(u, j):
        start = pl.multiple_of(j * tq, tq)
        pv = jnp.dot(p_s[u], va_s[u, pl.ds(start, tq), :], preferred_element_type=F32)
        return jnp.concatenate([al_s[u]] * (2 * dh // LANES), axis=1) * acc_s[u] + pv

    for u in range(hp):
        acc_s[u] = jnp.zeros(acc_s.shape[1:], F32)
        softmax(u, logits(u, i), True)

    def trip(j, _):
        prev = jnp.where(j == 0, i, j - 1)
        s = [logits(u, j) for u in range(hp)]
        for u in range(hp):
            acc_s[u] = values(u, prev)
        for u in range(hp):
            softmax(u, s[u], False)
        return 0

    lax.fori_loop(0, i, trip, 0)
    last = jnp.where(i == 0, i, i - 1)
    for u in range(hp):
        acc = values(u, last)
        o_ref[:, u * dh:(u + 1) * dh] = (acc[:, :dh] / acc[:, dh:dh + 1]).astype(o_ref.dtype)


def _fox_prompt(slab3, qb_all, kb_all, n_heads, dh, q_col, tq_pref, hp, mlp_w=None, layer=0):
    b, l, _ = slab3.shape
    tq = _tile(l, tq_pref)
    assert n_heads % hp == 0 and q_col % hp == 0
    qb, ng, w = q_col // hp, n_heads // hp, hp * dh
    nq = l // tq
    n_steps = b * ng * nq
    in_specs = [pl.BlockSpec((None, tq, w), lambda bi, h, i: (bi, i, qb + h)),
                pl.BlockSpec((None, l, w), lambda bi, h, i: (bi, 0, qb + ng + h)),
                pl.BlockSpec((None, l, w), lambda bi, h, i: (bi, 0, qb + 2 * ng + h)),
                pl.BlockSpec((None, tq, GATE_COLS), lambda bi, h, i: (bi, i, 0)),
                pl.BlockSpec((None, l, GATE_COLS), lambda bi, h, i: (bi, 0, 0))]
    out_specs = [pl.BlockSpec((None, tq, w), lambda bi, h, i: (bi, i, h))]
    out_shape = [jax.ShapeDtypeStruct((b, l, n_heads * dh), BF16)]
    args = [slab3, slab3, slab3, qb_all, kb_all]
    cast_weights = False
    if mlp_w is not None:
        w_up, w_down = mlp_w
        _, d, f = w_up.shape
        cast_weights = d % (16 * n_steps) == 0 and f % (16 * n_steps) == 0
    if cast_weights:
        ru, rd = d // n_steps, f // n_steps
        step = lambda bi, h, i: (bi * ng + h) * nq + i
        in_specs += [pl.BlockSpec((None, ru, f), lambda bi, h, i: (layer, step(bi, h, i), 0)),
                     pl.BlockSpec((None, rd, d), lambda bi, h, i: (layer, step(bi, h, i), 0))]
        out_specs += [pl.BlockSpec((ru, f), lambda bi, h, i: (step(bi, h, i), 0)),
                      pl.BlockSpec((rd, d), lambda bi, h, i: (step(bi, h, i), 0))]
        out_shape += [jax.ShapeDtypeStruct((d, f), BF16), jax.ShapeDtypeStruct((f, d), BF16)]
        args += [w_up, w_down]
    outs = pl.pallas_call(
        functools.partial(_fox_prompt_kernel, tq=tq, hp=hp, cast_weights=cast_weights),
        grid=(b, ng, nq),
        in_specs=in_specs,
        out_specs=out_specs,
        out_shape=out_shape,
        scratch_shapes=[pltpu.VMEM((hp, tq, 2 * dh), BF16), pltpu.VMEM((hp, l, 2 * dh), BF16),
                        pltpu.VMEM((hp, l, 2 * dh), BF16), pltpu.VMEM((hp, tq, tq), BF16),
                        pltpu.VMEM((hp, tq, LANES), F32), pltpu.VMEM((hp, tq, LANES), F32),
                        pltpu.VMEM((hp, tq, 2 * dh), F32)],
        compiler_params=_cparams(("parallel", "parallel", "arbitrary")),
        name="fox_prompt",
    )(*args)
    if cast_weights:
        return outs[0], (outs[1], outs[2])
    if mlp_w is not None:
        return outs[0], (mlp_w[0][layer].astype(BF16), mlp_w[1][layer].astype(BF16))
    return outs[0], None


def _fox_sample_kernel(q_ref, k_ref, v_ref, ck_ref, cv_ref, clf_ref, gc_ref, o_ref, *, n_heads, dh, lane0):
    l = q_ref.shape[0]
    cum = _prefix_sum_rows(clf_ref[...])
    past = cum.shape[0]
    gc = gc_ref[...]
    row = lax.broadcasted_iota(jnp.int32, (l, l), 0)
    col = lax.broadcasted_iota(jnp.int32, (l, l), 1)
    for h in range(n_heads):
        f_old = cum[:, h:h + 1]
        f_new = f_old[past - 1:past, :] + gc[:, lane0 + h:lane0 + h + 1]
        bias_new_k = _bias_lanes(f_new * LOG2E, False)
        qa = jnp.concatenate([q_ref[:, h * dh:(h + 1) * dh], _bias_lanes(f_new * LOG2E, True)], axis=1)
        ka_new = jnp.concatenate([k_ref[:, h * dh:(h + 1) * dh], bias_new_k], axis=1)
        ka_old = jnp.concatenate([ck_ref[:, h, :].astype(BF16), _bias_lanes(f_old * LOG2E, False)], axis=1)
        s_old = _dot_nt(qa, ka_old)
        s_new = jnp.where(col <= row, _dot_nt(qa, ka_new), -jnp.inf)
        m = jnp.maximum(jnp.max(s_old, axis=1, keepdims=True), jnp.max(s_new, axis=1, keepdims=True))
        p_old = jnp.exp2(s_old - m)
        p_new = jnp.exp2(s_new - m)
        den = jnp.sum(p_old, axis=1, keepdims=True) + jnp.sum(p_new, axis=1, keepdims=True)
        acc = (jnp.dot(p_old.astype(BF16), cv_ref[:, h, :].astype(BF16), preferred_element_type=F32)
               + jnp.dot(p_new.astype(BF16), v_ref[:, h * dh:(h + 1) * dh], preferred_element_type=F32))
        o_ref[:, h * dh:(h + 1) * dh] = (acc / den).astype(o_ref.dtype)


def _fox_sample(slab3, ck, cv, clf, layer, gc, n_heads, dh, q_col, lane0):
    b, l, _ = slab3.shape
    past = ck.shape[2]
    dd = n_heads * dh
    qb = q_col // n_heads
    return pl.pallas_call(
        functools.partial(_fox_sample_kernel, n_heads=n_heads, dh=dh, lane0=lane0),
        grid=(b,),
        in_specs=[pl.BlockSpec((None, l, dd), lambda bi: (bi, 0, qb)),
                  pl.BlockSpec((None, l, dd), lambda bi: (bi, 0, qb + 1)),
                  pl.BlockSpec((None, l, dd), lambda bi: (bi, 0, qb + 2)),
                  pl.BlockSpec((None, None, past, n_heads, dh), lambda bi: (layer, bi, 0, 0, 0)),
                  pl.BlockSpec((None, None, past, n_heads, dh), lambda bi: (layer, bi, 0, 0, 0)),
                  pl.BlockSpec((None, None, past, n_heads), lambda bi: (layer, bi, 0, 0)),
                  pl.BlockSpec((None, l, GATE_COLS), lambda bi: (bi, 0, 0))],
        out_specs=pl.BlockSpec((None, l, dd), lambda bi: (bi, 0, 0)),
        out_shape=jax.ShapeDtypeStruct((b, l, dd), BF16),
        compiler_params=_cparams(("parallel",)),
        name="fox_sample",
    )(slab3, slab3, slab3, ck, cv, clf, gc)


def _outproj_kernel(ha_ref, hb_ref, x_ref, wa_ref, wb_ref, g_ref, b_ref, y_ref, *, alpha):
    tm = x_ref.shape[0]
    tr = min(tm, 256)
    for r in range(tm // tr):
        rows = slice(r * tr, (r + 1) * tr)
        mix = (jnp.dot(ha_ref[rows, :], wa_ref[...], preferred_element_type=F32)
               + jnp.dot(hb_ref[rows, :], wb_ref[...], preferred_element_type=F32))
        y_ref[rows, :] = _layer_norm(alpha * x_ref[rows, :] + mix, g_ref[...], b_ref[...])


def _outproj(ha2, hb2, x2, w_out, ln_g, ln_b, layer, alpha, tm_pref):
    m, d = x2.shape
    da = ha2.shape[1]
    db = hb2.shape[1]
    assert da == db
    tm = _tile(m, tm_pref)
    vec = pl.BlockSpec((None, 1, d), lambda i: (layer, 0, 0))
    return pl.pallas_call(
        functools.partial(_outproj_kernel, alpha=alpha),
        grid=(m // tm,),
        in_specs=[pl.BlockSpec((tm, da), lambda i: (i, 0)),
                  pl.BlockSpec((tm, db), lambda i: (i, 0)),
                  pl.BlockSpec((tm, d), lambda i: (i, 0)),
                  pl.BlockSpec((None, da, d), lambda i: (layer, 0, 0)),
                  pl.BlockSpec((None, db, d), lambda i: (layer, 1, 0)),
                  vec, vec],
        out_specs=pl.BlockSpec((tm, d), lambda i: (i, 0)),
        out_shape=jax.ShapeDtypeStruct((m, d), F32),
        compiler_params=_cparams(("parallel",)),
        name="outproj_ln",
    )(ha2, hb2, x2, w_out, w_out, ln_g, ln_b)


def _ffn_kernel(x_ref, wu_ref, wd_ref, g_ref, b_ref, y_ref, *rest, alpha):
    yb_ref = rest[0] if len(rest) == 3 else None
    xb_ref, acc_ref = rest[-2:]
    j = pl.program_id(1)

    @pl.when(j == 0)
    def _():
        xb_ref[...] = x_ref[...].astype(BF16)
        acc_ref[...] = jnp.zeros_like(acc_ref)

    u = jnp.dot(xb_ref[...], wu_ref[...], preferred_element_type=F32)
    u = jnp.maximum(u, 0.0)
    acc_ref[...] += jnp.dot((u * u).astype(BF16), wd_ref[...], preferred_element_type=F32)

    @pl.when(j == pl.num_programs(1) - 1)
    def _():
        y = _layer_norm(alpha * x_ref[...] + acc_ref[...], g_ref[...], b_ref[...])
        y_ref[...] = y
        if yb_ref is not None:
            yb_ref[...] = y.astype(BF16)


def _ffn(x2, w_up, w_down, ln_g, ln_b, layer, alpha, tm_pref, tf_pref, with_bf16):
    m, d = x2.shape
    f = w_up.shape[1]
    tm = _tile(m, tm_pref)
    tf = _tile(f, tf_pref)
    vec = pl.BlockSpec((None, 1, d), lambda i, j: (layer, 0, 0))
    rows = pl.BlockSpec((tm, d), lambda i, j: (i, 0))
    n_out = 2 if with_bf16 else 1
    outs = pl.pallas_call(
        functools.partial(_ffn_kernel, alpha=alpha),
        grid=(m // tm, f // tf),
        in_specs=[rows,
                  pl.BlockSpec((d, tf), lambda i, j: (0, j)),
                  pl.BlockSpec((tf, d), lambda i, j: (j, 0)),
                  vec, vec],
        out_specs=[rows, rows][:n_out],
        out_shape=[jax.ShapeDtypeStruct((m, d), F32), jax.ShapeDtypeStruct((m, d), BF16)][:n_out],
        scratch_shapes=[pltpu.VMEM((tm, d), BF16), pltpu.VMEM((tm, d), F32)],
        compiler_params=_cparams(("parallel", "arbitrary")),
        name="ffn_ln",
    )(x2, w_up, w_down, ln_g, ln_b)
    return (outs[0], outs[1]) if with_bf16 else (outs[0], None)


def _layer(x, xb, layer, last, cache, kv_prev, mlp_w, wts, dims, tiles):
    h_a, dk, dv, h_b, dh = dims
    (w_main, w_gate, col_scale, b_gate, g_head, w_out, ln1_g, ln1_b, ln2_g, ln2_b, alpha) = wts
    b, l, d = x.shape
    m = b * l
    d_b = h_b * dh
    x2 = x.reshape(m, d)

    x_in, tm_in = (x2, tiles["tm_in"]) if xb is None else (xb, tiles["tm_in_bf16"])
    slab, k_all, v_all, gpre = _inproj(x_in, w_main, w_gate, col_scale, layer, d_b, kv_prev, tm_in)
    slab3 = slab.reshape(b, l, slab.shape[1])
    n_rows = 2 * h_a + h_b
    lf, gc, gr, qb_all, kb_all = _gates(gpre.reshape(b, l, GATE_COLS), b_gate, layer, h_a, n_rows,
                                        _bias_placement(2 * h_a, h_b))
    q_col = (2 * h_a * dk + 2 * h_a * dv) // dh

    if cache is None:
        c0 = jnp.zeros((b, h_a, dv, dk), F32)
        n0 = jnp.zeros((b, h_a, dk), F32)
        m0 = jnp.zeros((b, h_a), F32)
        hb, mlp_w = _fox_prompt(slab3, qb_all, kb_all, h_b, dh, q_col, tiles["tq"], tiles["fox_heads"], mlp_w, layer)
    else:
        ck, cv, clf, c_all, n_all, m_all = cache
        c0, n0, m0 = c_all[layer], n_all[layer], m_all[layer]
        hb = _fox_sample(slab3, ck, cv, clf, layer, gc, h_b, dh, q_col, 2 * h_a)
    ha, c, n, mm = _mlstm(slab3, gc, gr, g_head, layer, c0, n0, m0, h_a, dk, dv, tiles["lc"])

    x1 = _outproj(ha.reshape(m, -1), hb.reshape(m, -1), x2, w_out, ln1_g, ln1_b, layer, alpha, tiles["tm_out"])
    y, yb = _ffn(x1, mlp_w[0], mlp_w[1], ln2_g, ln2_b, layer, alpha, tiles["tm_ffn"], tiles["tf_ffn"], not last)
    lfb = lf[:, :, 2 * h_a:n_rows]
    return y.reshape(b, l, d), yb, (k_all, v_all), (lfb, c, n, mm), mlp_w


def kernel(x_prompt, x_sample, cache_fox_k, cache_fox_v, cache_fox_logf, state_mlstm_c, state_mlstm_n,
           state_mlstm_m, w_in, b_gates, g_mlstm, w_out, ln1_g, ln1_b, w_up, w_down, ln2_g, ln2_b):
    depth, d_model, _ = w_in.shape
    _, _, _, h_b, dh = cache_fox_k.shape
    _, _, h_a, dv, dk = state_mlstm_c.shape
    d_a = h_a * dv
    d_b = h_b * dh
    dims = (h_a, dk, dv, h_b, dh)
    alpha = (2 * depth) ** 0.25
    widths = (h_a * dk, h_a * dk, d_a, d_a, h_a, h_a, d_b, d_b, d_b, h_b)
    offs = [0]
    for w in widths:
        offs.append(offs[-1] + w)
    n_gates = 2 * h_a + h_b

    tiles = dict(tm_in=512, tm_in_bf16=1024, tn_in=1024, tq=1024, fox_heads=2, lc=256, tm_out=512, tm_ffn=512, tf_ffn=1024)

    w_main, w_gate = _win_prep(w_in, ((offs[0], offs[4]), (offs[6], offs[9])), ((offs[4], offs[6]), (offs[9], offs[10])),
                               tiles["tn_in"])
    b_gate = jnp.pad(b_gates, ((0, 0), (0, GATE_COLS - n_gates))).reshape(depth, 1, GATE_COLS)
    n_main = w_main.shape[1] * w_main.shape[3]
    col_scale = jnp.concatenate([jnp.full((h_a * dk,), dk ** -0.5, F32),
                                 jnp.ones((offs[4] - offs[1],), F32),
                                 jnp.full((d_b,), dh ** -0.5 * LOG2E, F32),
                                 jnp.ones((2 * d_b,), F32)]).reshape(1, n_main)
    wts = (w_main, w_gate, col_scale, b_gate, g_mlstm.reshape(depth, 1, d_a), w_out.astype(BF16),
           ln1_g.reshape(depth, 1, d_model), ln1_b.reshape(depth, 1, d_model),
           ln2_g.reshape(depth, 1, d_model), ln2_b.reshape(depth, 1, d_model), alpha)
    cache = (cache_fox_k, cache_fox_v, cache_fox_logf, state_mlstm_c, state_mlstm_n, state_mlstm_m)


    yp, ys = x_prompt, x_sample
    ypb = ysb = None
    kv_p = kv_s = None
    small_p, small_s = [], []
    for layer in range(depth):
        last = layer == depth - 1
        yp, ypb, kv_p, st_p, mlp_bf16 = _layer(yp, ypb, layer, last, None, kv_p, (w_up, w_down), wts, dims, tiles)
        ys, ysb, kv_s, st_s, _ = _layer(ys, ysb, layer, last, cache, kv_s, mlp_bf16, wts, dims, tiles)
        small_p.append(st_p)
        small_s.append(st_s)
    stack = lambda states, i: jnp.stack([s[i] for s in states], axis=0)
    bp, lp, _ = x_prompt.shape
    bs, ls, _ = x_sample.shape
    return ((yp, ys)
            + tuple(a.reshape(depth, bp, lp, h_b, dh) for a in kv_p) + tuple(stack(small_p, i) for i in range(4))
            + tuple(a.reshape(depth, bs, ls, h_b, dh) for a in kv_s) + tuple(stack(small_s, i) for i in range(4)))
```

```python
import functools
import math

import jax
import jax.numpy as jnp
from jax import lax
from jax.experimental import pallas as pl
from jax.experimental.pallas import tpu as pltpu

F32 = jnp.float32
BF16 = jnp.bfloat16

LN_EPS = 1e-5
HEAD_NORM_EPS = 1e-6
LOG2E = math.log2(math.e)
LANES = 128
GATE_COLS = LANES
VMEM_LIMIT = 56 * 1024 * 1024


def _cparams(sem):
    return pltpu.CompilerParams(dimension_semantics=sem, vmem_limit_bytes=VMEM_LIMIT)


def _tile(n, pref):
    t = min(n, pref)
    assert n % t == 0, (n, pref)
    return t


def _log_sigmoid(g):
    return jnp.minimum(g, 0.0) - jnp.log1p(jnp.exp(-jnp.abs(g)))


def _dot_nt(a, b):
    return lax.dot_general(a, b, (((1,), (1,)), ((), ())), preferred_element_type=F32)


def _dot_tn(a, b):
    return lax.dot_general(a, b, (((0,), (0,)), ((), ())), preferred_element_type=F32)


def _layer_norm(z, g, b):
    mu = jnp.mean(z, axis=-1, keepdims=True)
    zc = z - mu
    var = jnp.mean(zc * zc, axis=-1, keepdims=True)
    return zc * lax.rsqrt(var + LN_EPS) * g + b


def _prefix_sum_rows(v):
    row = lax.broadcasted_iota(jnp.int32, v.shape, 0)
    s = 1
    while s < v.shape[0]:
        v = v + jnp.where(row >= s, pltpu.roll(v, s, 0), 0.0)
        s *= 2
    return v


def _split3(f):
    hi = f.astype(BF16).astype(F32)
    r = f - hi
    mid = r.astype(BF16).astype(F32)
    lo = (r - mid).astype(BF16).astype(F32)
    return hi, mid, lo


def _bias_lanes(f, query_side):
    hi, mid, lo = _split3(f)
    lane = lax.broadcasted_iota(jnp.int32, (f.shape[0], LANES), 1)
    if query_side:
        a = jnp.where(lane == 0, hi, jnp.where(lane == 1, mid, jnp.where(lane == 2, lo, jnp.where(lane < 6, 1.0, 0.0))))
    else:
        a = jnp.where(lane == 3, -hi, jnp.where(lane == 4, -mid, jnp.where(lane == 5, -lo, jnp.where(lane < 3, 1.0, 0.0))))
    return a.astype(BF16)


def _pick_lane(x, idx):
    lane = lax.broadcasted_iota(jnp.int32, x.shape, 1)
    return jnp.sum(jnp.where(lane == idx, x, 0.0), axis=1, keepdims=True)


def _win_prep_kernel(wt_ref, main_ref, gate_ref, *, main_cols, gate_cols):
    nj, tk, tn = main_ref.shape
    starts = [a + off for a, b in main_cols for off in range(0, b - a, tn)]
    for j in range(nj):
        main_ref[j] = wt_ref[starts[j]:starts[j] + tn, :].T.astype(BF16)
    gates = [wt_ref[a:b, :] for a, b in gate_cols]
    pad = GATE_COLS - sum(b - a for a, b in gate_cols)
    gate_ref[...] = jnp.concatenate(gates + [jnp.zeros((pad, tk), F32)], axis=0).T.astype(BF16)


def _win_prep(w_in, main_cols, gate_cols, tn):
    depth, d, n_in = w_in.shape
    assert all((b - a) % tn == 0 and a % 8 == 0 for a, b in main_cols)
    assert all((b - a) % 8 == 0 and a % 8 == 0 for a, b in gate_cols)
    n_main = sum(b - a for a, b in main_cols)
    nj = n_main // tn
    tk = _tile(d, 256)
    return pl.pallas_call(
        functools.partial(_win_prep_kernel, main_cols=main_cols, gate_cols=gate_cols),
        grid=(depth, d // tk),
        in_specs=[pl.BlockSpec((None, n_in, tk), lambda l, t: (l, 0, t))],
        out_specs=[pl.BlockSpec((None, nj, tk, tn), lambda l, t: (l, 0, t, 0)),
                   pl.BlockSpec((None, tk, GATE_COLS), lambda l, t: (l, t, 0))],
        out_shape=[jax.ShapeDtypeStruct((depth, nj, d, tn), BF16),
                   jax.ShapeDtypeStruct((depth, d, GATE_COLS), BF16)],
        compiler_params=_cparams(("parallel", "parallel")),
        name="win_prep",
    )(jnp.swapaxes(w_in, 1, 2))


def _inproj_kernel(*refs, jk0, jv0, aliased):
    x_ref, w_ref, wg_ref, cs_ref = refs[:4]
    del aliased
    j = pl.program_id(1)
    if x_ref.dtype == BF16:
        slab_ref, k_ref, v_ref, g_ref = refs[-4:]
        xb_ref = x_ref

        @pl.when(j == 0)
        def _():
            g_ref[...] = jnp.dot(x_ref[...], wg_ref[...], preferred_element_type=F32)
    else:
        slab_ref, k_ref, v_ref, g_ref, xb_ref = refs[-5:]

        @pl.when(j == 0)
        def _():
            xb = x_ref[...].astype(BF16)
            xb_ref[...] = xb
            g_ref[...] = jnp.dot(xb, wg_ref[...], preferred_element_type=F32)

    def tile(f32_ref):
        acc = jnp.dot(xb_ref[...], w_ref[...], preferred_element_type=F32)
        slab_ref[...] = (acc * cs_ref[...]).astype(BF16)
        if f32_ref is not None:
            f32_ref[...] = acc

    pl.when(j < jk0)(lambda: tile(None))
    pl.when((j >= jk0) & (j < jv0))(lambda: tile(k_ref))
    pl.when(j >= jv0)(lambda: tile(v_ref))


def _inproj(x2, w_main, w_gate, col_scale, layer, d_b, kv_prev, tm_pref):
    m, d = x2.shape
    depth, nj, _, tn = w_main.shape
    n = nj * tn
    tm = _tile(m, tm_pref)
    assert d_b % tn == 0
    nkv = d_b // tn
    jk0 = nj - 2 * nkv
    jv0 = nj - nkv
    aliased = kv_prev is not None
    in_specs = [pl.BlockSpec((tm, d), lambda i, j: (i, 0)),
                pl.BlockSpec((None, None, d, tn), lambda i, j: (layer, j, 0, 0)),
                pl.BlockSpec((None, d, GATE_COLS), lambda i, j: (layer, 0, 0)),
                pl.BlockSpec((1, tn), lambda i, j: (0, j))]
    args = [x2, w_main, w_gate, col_scale]
    aliases = {}
    if aliased:
        in_specs += [pl.BlockSpec(memory_space=pl.ANY), pl.BlockSpec(memory_space=pl.ANY)]
        args += list(kv_prev)
        aliases = {4: 1, 5: 2}
    return pl.pallas_call(
        functools.partial(_inproj_kernel, jk0=jk0, jv0=jv0, aliased=aliased),
        grid=(m // tm, nj),
        in_specs=in_specs,
        out_specs=[pl.BlockSpec((tm, tn), lambda i, j: (i, j)),
                   pl.BlockSpec((None, tm, tn), lambda i, j: (layer, i, jnp.clip(j - jk0, 0, nkv - 1))),
                   pl.BlockSpec((None, tm, tn), lambda i, j: (layer, i, jnp.clip(j - jv0, 0, nkv - 1))),
                   pl.BlockSpec((tm, GATE_COLS), lambda i, j: (i, 0))],
        out_shape=[jax.ShapeDtypeStruct((m, n), BF16),
                   jax.ShapeDtypeStruct((depth, m, d_b), F32),
                   jax.ShapeDtypeStruct((depth, m, d_b), F32),
                   jax.ShapeDtypeStruct((m, GATE_COLS), F32)],
        scratch_shapes=[] if x2.dtype == BF16 else [pltpu.VMEM((tm, d), BF16)],
        input_output_aliases=aliases,
        compiler_params=_cparams(("parallel", "arbitrary")),
        name="inproj",
    )(*args)


BIAS_LANES = 6


def _bias_placement(lane0, n_heads):
    import numpy as np
    place = np.zeros((3 * LANES, 2 * LANES), np.float32)
    ones = np.zeros((2, LANES), np.float32)
    for h in range(n_heads):
        for t in range(3):
            place[t * LANES + lane0 + h, BIAS_LANES * h + t] = 1.0
            place[t * LANES + lane0 + h, LANES + BIAS_LANES * h + 3 + t] = -1.0
            ones[0, BIAS_LANES * h + 3 + t] = 1.0
            ones[1, BIAS_LANES * h + t] = 1.0
    return jnp.asarray(place, BF16), jnp.asarray(ones, F32)


def _gates_kernel(gp_ref, bg_ref, place_ref, ones_ref, lf_ref, gc_ref, gr_ref, qb_ref, kb_ref, carry_ref,
                  *, n_raw, n_rows):
    t = pl.program_id(1)

    @pl.when(t == 0)
    def _():
        carry_ref[...] = jnp.zeros_like(carry_ref)

    g = gp_ref[...] + bg_ref[...]
    tl = g.shape[0]
    lane = lax.broadcasted_iota(jnp.int32, g.shape, 1)
    raw = lane < n_raw
    val = jnp.where(raw, g, _log_sigmoid(g))
    lf_ref[...] = val
    c = _prefix_sum_rows(val) + carry_ref[...]
    carry_ref[...] = c[tl - 1:tl, :]
    out = jnp.where(raw, g, c)
    gc_ref[...] = out
    gr_ref[...] = out.T[:n_rows, :]
    split = jnp.concatenate(_split3(c * LOG2E), axis=1).astype(BF16)
    moved = jnp.dot(split, place_ref[...], preferred_element_type=F32)
    qb_ref[...] = (moved[:, :LANES] + ones_ref[0:1, :]).astype(BF16)
    kb_ref[...] = (moved[:, LANES:] + ones_ref[1:2, :]).astype(BF16)


def _gates(gpre3, b_gate, layer, n_raw, n_rows, placement):
    b, l, _ = gpre3.shape
    tl = _tile(l, 512)
    place, ones = placement
    const = lambda i, t: (0, 0)
    tile = pl.BlockSpec((None, tl, GATE_COLS), lambda i, t: (i, t, 0))
    return pl.pallas_call(
        functools.partial(_gates_kernel, n_raw=n_raw, n_rows=n_rows),
        grid=(b, l // tl),
        in_specs=[tile,
                  pl.BlockSpec((None, 1, GATE_COLS), lambda i, t: (layer, 0, 0)),
                  pl.BlockSpec(place.shape, const),
                  pl.BlockSpec(ones.shape, const)],
        out_specs=[tile, tile, pl.BlockSpec((None, n_rows, tl), lambda i, t: (i, 0, t)), tile, tile],
        out_shape=[jax.ShapeDtypeStruct((b, l, GATE_COLS), F32),
                   jax.ShapeDtypeStruct((b, l, GATE_COLS), F32),
                   jax.ShapeDtypeStruct((b, n_rows, l), F32),
                   jax.ShapeDtypeStruct((b, l, GATE_COLS), BF16),
                   jax.ShapeDtypeStruct((b, l, GATE_COLS), BF16)],
        scratch_shapes=[pltpu.VMEM((1, GATE_COLS), F32)],
        compiler_params=_cparams(("parallel", "arbitrary")),
        name="gates",
    )(gpre3, b_gate, place, ones)


def _mlstm_kernel(q_ref, k_ref, v_ref, o_ref, gc_ref, gr_ref, gh_ref, c0_ref, n0_ref, m0_ref,
                  h_ref, c_ref, n_ref, m_ref, cta_s, m_s, fp_s, *, n_heads, dk, dv):
    c_idx = pl.program_id(1)
    n_chunks = pl.num_programs(1)
    lc = q_ref.shape[0]

    @pl.when(c_idx == 0)
    def _():
        lane = lax.broadcasted_iota(jnp.int32, (dk, LANES), 1)
        for h in range(n_heads):
            cta_s[h, :, :dv] = c0_ref[h].T
            n_cols = jnp.broadcast_to(n0_ref[h:h + 1, :], (LANES, dk)).T
            cta_s[h, :, dv:] = jnp.where(lane == 0, n_cols, 0.0)
        m_s[...] = m0_ref[...]
        fp_s[...] = jnp.zeros_like(fp_s)

    gc = gc_ref[...]
    gr = gr_ref[...]
    fp = fp_s[...]
    m_prev = m_s[...]
    row1 = lax.broadcasted_iota(jnp.int32, gc.shape, 0)
    b = gc - fp
    ig = pltpu.roll(gc, n_heads, 1)
    run = ig - b
    step = 1
    while step < lc:
        run = jnp.maximum(run, jnp.where(row1 >= step, pltpu.roll(run, step, 0), -jnp.inf))
        step *= 2
    g = b + m_prev
    m_t = jnp.maximum(g, b + run)
    w_inter = jnp.exp(g - m_t)
    e_neg = jnp.exp(-m_t)
    m_new = m_t[lc - 1:lc, :]
    b_last = b[lc - 1:lc, :]
    a_all = jnp.exp(b_last - b + ig - m_new)
    decay = jnp.exp(b_last + m_prev - m_new)
    m_s[...] = m_new
    fp_s[...] = gc[lc - 1:lc, :]

    row = lax.broadcasted_iota(jnp.int32, (lc, lc), 0)
    col = lax.broadcasted_iota(jnp.int32, (lc, lc), 1)
    causal = col <= row
    lane = lax.broadcasted_iota(jnp.int32, (lc, LANES), 1)
    ones_col = jnp.where(lane == 0, 1.0, 0.0).astype(BF16)
    for h in range(n_heads):
        hl = n_heads + h
        q = q_ref[:, h * dk:(h + 1) * dk]
        k = k_ref[:, h * dk:(h + 1) * dk]
        va = jnp.concatenate([v_ref[:, h * dv:(h + 1) * dv], ones_col], axis=1)
        ig_row = gr[h:h + 1, :]
        b_row = gr[hl:hl + 1, :] - fp[:, hl:hl + 1]
        d = jnp.where(causal, b[:, hl:hl + 1] - b_row + ig_row, -jnp.inf)
        s = _dot_nt(q, k) * jnp.exp(d - m_t[:, hl:hl + 1])
        cta = cta_s[h]
        both = (w_inter[:, hl:hl + 1] * jnp.dot(q, cta.astype(BF16), preferred_element_type=F32)
                + jnp.dot(s.astype(BF16), va, preferred_element_type=F32))
        hh = both[:, :dv] / jnp.maximum(jnp.abs(both[:, dv:dv + 1]), e_neg[:, hl:hl + 1])

        ka = (k.astype(F32) * a_all[:, hl:hl + 1]).astype(BF16)
        cta_s[h] = decay[:, hl:hl + 1] * cta + _dot_tn(ka, va)

        hn = hh * lax.rsqrt(jnp.mean(hh * hh, axis=1, keepdims=True) + HEAD_NORM_EPS)
        og = o_ref[:, h * dv:(h + 1) * dv].astype(F32)
        hn = hn * gh_ref[:, h * dv:(h + 1) * dv] * jax.nn.sigmoid(og)
        h_ref[:, h * dv:(h + 1) * dv] = hn.astype(h_ref.dtype)

    @pl.when(c_idx == n_chunks - 1)
    def _():
        for h in range(n_heads):
            c_ref[h] = cta_s[h, :, :dv].T
            n_ref[h:h + 1, :] = cta_s[h, :, dv:].T[0:1, :]
        m_ref[...] = m_s[...]


def _mlstm(slab3, gc, gr, g_head, layer, c0, n0, m0, n_heads, dk, dv, lc_pref):
    b, l, _ = slab3.shape
    lc = _tile(l, lc_pref)
    dq = n_heads * dk
    da = n_heads * dv
    assert da == 2 * dq and dk == LANES
    m0r = jnp.pad(m0, ((0, 0), (n_heads, GATE_COLS - 2 * n_heads))).reshape(b, 1, GATE_COLS)
    bmap = lambda i, c: (i, 0, 0)
    outs = pl.pallas_call(
        functools.partial(_mlstm_kernel, n_heads=n_heads, dk=dk, dv=dv),
        grid=(b, l // lc),
        in_specs=[pl.BlockSpec((None, lc, dq), lambda i, c: (i, c, 0)),
                  pl.BlockSpec((None, lc, dq), lambda i, c: (i, c, 1)),
                  pl.BlockSpec((None, lc, da), lambda i, c: (i, c, 1)),
                  pl.BlockSpec((None, lc, da), lambda i, c: (i, c, 2)),
                  pl.BlockSpec((None, lc, GATE_COLS), lambda i, c: (i, c, 0)),
                  pl.BlockSpec((None, gr.shape[1], lc), lambda i, c: (i, 0, c)),
                  pl.BlockSpec((None, 1, da), lambda i, c: (layer, 0, 0)),
                  pl.BlockSpec((None, n_heads, dv, dk), lambda i, c: (i, 0, 0, 0)),
                  pl.BlockSpec((None, n_heads, dk), bmap),
                  pl.BlockSpec((None, 1, GATE_COLS), bmap)],
        out_specs=[pl.BlockSpec((None, lc, da), lambda i, c: (i, c, 0)),
                   pl.BlockSpec((None, n_heads, dv, dk), lambda i, c: (i, 0, 0, 0)),
                   pl.BlockSpec((None, n_heads, dk), bmap),
                   pl.BlockSpec((None, 1, GATE_COLS), bmap)],
        out_shape=[jax.ShapeDtypeStruct((b, l, da), BF16),
                   jax.ShapeDtypeStruct((b, n_heads, dv, dk), F32),
                   jax.ShapeDtypeStruct((b, n_heads, dk), F32),
                   jax.ShapeDtypeStruct((b, 1, GATE_COLS), F32)],
        scratch_shapes=[pltpu.VMEM((n_heads, dk, dv + LANES), F32),
                        pltpu.VMEM((1, GATE_COLS), F32),
                        pltpu.VMEM((1, GATE_COLS), F32)],
        compiler_params=_cparams(("parallel", "arbitrary")),
        name="mlstm",
    )(slab3, slab3, slab3, slab3, gc, gr, g_head, c0, n0, m0r)
    h_a, c, n, m = outs
    return h_a, c, n, m[:, 0, n_heads:2 * n_heads]


def _fox_prompt_kernel(q_ref, k_ref, v_ref, qb_ref, kb_ref, *rest, tq, hp, cast_weights):
    if cast_weights:
        wu_ref, wd_ref, o_ref, wub_ref, wdb_ref = rest[:5]
        wub_ref[...] = wu_ref[...].astype(BF16)
        wdb_ref[...] = wd_ref[...].astype(BF16)
        rest = rest[5:]
    else:
        o_ref, rest = rest[0], rest[1:]
    qa_s, ka_s, va_s, p_s, m_s, al_s, acc_s = rest
    hg = pl.program_id(1)
    i = pl.program_id(2)
    dh = q_ref.shape[1] // hp

    @pl.when(i == 0)
    def _():
        kb = kb_ref[...]
        lane = lax.broadcasted_iota(jnp.int32, kb.shape, 1)
        ones_lane = jnp.where(lane == 0, 1.0, 0.0).astype(BF16)
        for u in range(hp):
            first = BIAS_LANES * (hg * hp + u)
            ka_s[u, :, :dh] = k_ref[:, u * dh:(u + 1) * dh]
            ka_s[u, :, dh:] = jnp.where((lane >= first) & (lane < first + BIAS_LANES), kb, jnp.zeros_like(kb))
            va_s[u, :, :dh] = v_ref[:, u * dh:(u + 1) * dh]
            va_s[u, :, dh:] = ones_lane

    for u in range(hp):
        qa_s[u, :, :dh] = q_ref[:, u * dh:(u + 1) * dh]
        qa_s[u, :, dh:] = qb_ref[...]

    def logits(u, j):
        start = pl.multiple_of(j * tq, tq)
        return _dot_nt(qa_s[u], ka_s[u, pl.ds(start, tq), :])

    def softmax(u, s, diagonal):
        if diagonal:
            row = lax.broadcasted_iota(jnp.int32, s.shape, 0)
            col = lax.broadcasted_iota(jnp.int32, s.shape, 1)
            s = jnp.where(col <= row, s, -jnp.inf)
            m_new = jnp.broadcast_to(jnp.max(s, axis=1, keepdims=True), (tq, LANES))
            al_s[u] = jnp.zeros_like(m_new)
        else:
            m = m_s[u]
            m_new = jnp.maximum(m, jnp.max(s, axis=1, keepdims=True))
            al_s[u] = jnp.exp2(m - m_new)
        m_s[u] = m_new
        p_s[u] = jnp.exp2(s - jnp.concatenate([m_new] * (tq // LANES), axis=1)).astype(BF16)

    def values(u, j):
        start = pl.multiple_of(j * tq, tq)
        pv = jnp.dot(p_s[u], va_s[u, pl.ds(start, tq), :], preferred_element_type=F32)
        return jnp.concatenate([al_s[u]] * (2 * dh // LANES), axis=1) * acc_s[u] + pv

    for u in range(hp):
        acc_s[u] = jnp.zeros(acc_s.shape[1:], F32)
        softmax(u, logits(u, i), True)

    def trip(j, _):
        prev = jnp.where(j == 0, i, j - 1)
        s = [logits(u, j) for u in range(hp)]
        for u in range(hp):
            acc_s[u] = values(u, prev)
        for u in range(hp):
            softmax(u, s[u], False)
        return 0

    lax.fori_loop(0, i, trip, 0)
    last = jnp.where(i == 0, i, i - 1)
    for u in range(hp):
        acc = values(u, last)
        o_ref[:, u * dh:(u + 1) * dh] = (acc[:, :dh] / acc[:, dh:dh + 1]).astype(o_ref.dtype)


def _fox_prompt(slab3, qb_all, kb_all, n_heads, dh, q_col, tq_pref, hp, mlp_w=None, layer=0):
    b, l, _ = slab3.shape
    tq = _tile(l, tq_pref)
    assert n_heads % hp == 0 and q_col % hp == 0
    qb, ng, w = q_col // hp, n_heads // hp, hp * dh
    nq = l // tq
    n_steps = b * ng * nq
    in_specs = [pl.BlockSpec((None, tq, w), lambda bi, h, i: (bi, i, qb + h)),
                pl.BlockSpec((None, l, w), lambda bi, h, i: (bi, 0, qb + ng + h)),
                pl.BlockSpec((None, l, w), lambda bi, h, i: (bi, 0, qb + 2 * ng + h)),
                pl.BlockSpec((None, tq, GATE_COLS), lambda bi, h, i: (bi, i, 0)),
                pl.BlockSpec((None, l, GATE_COLS), lambda bi, h, i: (bi, 0, 0))]
    out_specs = [pl.BlockSpec((None, tq, w), lambda bi, h, i: (bi, i, h))]
    out_shape = [jax.ShapeDtypeStruct((b, l, n_heads * dh), BF16)]
    args = [slab3, slab3, slab3, qb_all, kb_all]
    cast_weights = False
    if mlp_w is not None:
        w_up, w_down = mlp_w
        _, d, f = w_up.shape
        cast_weights = d % (16 * n_steps) == 0 and f % (16 * n_steps) == 0
    if cast_weights:
        ru, rd = d // n_steps, f // n_steps
        step = lambda bi, h, i: (bi * ng + h) * nq + i
        in_specs += [pl.BlockSpec((None, ru, f), lambda bi, h, i: (layer, step(bi, h, i), 0)),
                     pl.BlockSpec((None, rd, d), lambda bi, h, i: (layer, step(bi, h, i), 0))]
        out_specs += [pl.BlockSpec((ru, f), lambda bi, h, i: (step(bi, h, i), 0)),
                      pl.BlockSpec((rd, d), lambda bi, h, i: (step(bi, h, i), 0))]
        out_shape += [jax.ShapeDtypeStruct((d, f), BF16), jax.ShapeDtypeStruct((f, d), BF16)]
        args += [w_up, w_down]
    outs = pl.pallas_call(
        functools.partial(_fox_prompt_kernel, tq=tq, hp=hp, cast_weights=cast_weights),
        grid=(b, ng, nq),
        in_specs=in_specs,
        out_specs=out_specs,
        out_shape=out_shape,
        scratch_shapes=[pltpu.VMEM((hp, tq, 2 * dh), BF16), pltpu.VMEM((hp, l, 2 * dh), BF16),
                        pltpu.VMEM((hp, l, 2 * dh), BF16), pltpu.VMEM((hp, tq, tq), BF16),
                        pltpu.VMEM((hp, tq, LANES), F32), pltpu.VMEM((hp, tq, LANES), F32),
                        pltpu.VMEM((hp, tq, 2 * dh), F32)],
        compiler_params=_cparams(("parallel", "parallel", "arbitrary")),
        name="fox_prompt",
    )(*args)
    if cast_weights:
        return outs[0], (outs[1], outs[2])
    if mlp_w is not None:
        return outs[0], (mlp_w[0][layer].astype(BF16), mlp_w[1][layer].astype(BF16))
    return outs[0], None


def _fox_sample_kernel(q_ref, k_ref, v_ref, ck_ref, cv_ref, clf_ref, gc_ref, o_ref, *, n_heads, dh, lane0):
    l = q_ref.shape[0]
    cum = _prefix_sum_rows(clf_ref[...])
    past = cum.shape[0]
    gc = gc_ref[...]
    row = lax.broadcasted_iota(jnp.int32, (l, l), 0)
    col = lax.broadcasted_iota(jnp.int32, (l, l), 1)
    for h in range(n_heads):
        f_old = cum[:, h:h + 1]
        f_new = f_old[past - 1:past, :] + gc[:, lane0 + h:lane0 + h + 1]
        bias_new_k = _bias_lanes(f_new * LOG2E, False)
        qa = jnp.concatenate([q_ref[:, h * dh:(h + 1) * dh], _bias_lanes(f_new * LOG2E, True)], axis=1)
        ka_new = jnp.concatenate([k_ref[:, h * dh:(h + 1) * dh], bias_new_k], axis=1)
        ka_old = jnp.concatenate([ck_ref[:, h, :].astype(BF16), _bias_lanes(f_old * LOG2E, False)], axis=1)
        s_old = _dot_nt(qa, ka_old)
        s_new = jnp.where(col <= row, _dot_nt(qa, ka_new), -jnp.inf)
        m = jnp.maximum(jnp.max(s_old, axis=1, keepdims=True), jnp.max(s_new, axis=1, keepdims=True))
        p_old = jnp.exp2(s_old - m)
        p_new = jnp.exp2(s_new - m)
        den = jnp.sum(p_old, axis=1, keepdims=True) + jnp.sum(p_new, axis=1, keepdims=True)
        acc = (jnp.dot(p_old.astype(BF16), cv_ref[:, h, :].astype(BF16), preferred_element_type=F32)
               + jnp.dot(p_new.astype(BF16), v_ref[:, h * dh:(h + 1) * dh], preferred_element_type=F32))
        o_ref[:, h * dh:(h + 1) * dh] = (acc / den).astype(o_ref.dtype)


def _fox_sample(slab3, ck, cv, clf, layer, gc, n_heads, dh, q_col, lane0):
    b, l, _ = slab3.shape
    past = ck.shape[2]
    dd = n_heads * dh
    qb = q_col // n_heads
    return pl.pallas_call(
        functools.partial(_fox_sample_kernel, n_heads=n_heads, dh=dh, lane0=lane0),
        grid=(b,),
        in_specs=[pl.BlockSpec((None, l, dd), lambda bi: (bi, 0, qb)),
                  pl.BlockSpec((None, l, dd), lambda bi: (bi, 0, qb + 1)),
                  pl.BlockSpec((None, l, dd), lambda bi: (bi, 0, qb + 2)),
                  pl.BlockSpec((None, None, past, n_heads, dh), lambda bi: (layer, bi, 0, 0, 0)),
                  pl.BlockSpec((None, None, past, n_heads, dh), lambda bi: (layer, bi, 0, 0, 0)),
                  pl.BlockSpec((None, None, past, n_heads), lambda bi: (layer, bi, 0, 0)),
                  pl.BlockSpec((None, l, GATE_COLS), lambda bi: (bi, 0, 0))],
        out_specs=pl.BlockSpec((None, l, dd), lambda bi: (bi, 0, 0)),
        out_shape=jax.ShapeDtypeStruct((b, l, dd), BF16),
        compiler_params=_cparams(("parallel",)),
        name="fox_sample",
    )(slab3, slab3, slab3, ck, cv, clf, gc)


def _outproj_kernel(ha_ref, hb_ref, x_ref, wa_ref, wb_ref, g_ref, b_ref, y_ref, *, alpha):
    tm = x_ref.shape[0]
    tr = min(tm, 256)
    for r in range(tm // tr):
        rows = slice(r * tr, (r + 1) * tr)
        mix = (jnp.dot(ha_ref[rows, :], wa_ref[...], preferred_element_type=F32)
               + jnp.dot(hb_ref[rows, :], wb_ref[...], preferred_element_type=F32))
        y_ref[rows, :] = _layer_norm(alpha * x_ref[rows, :] + mix, g_ref[...], b_ref[...])


def _outproj(ha2, hb2, x2, w_out, ln_g, ln_b, layer, alpha, tm_pref):
    m, d = x2.shape
    da = ha2.shape[1]
    db = hb2.shape[1]
    assert da == db
    tm = _tile(m, tm_pref)
    vec = pl.BlockSpec((None, 1, d), lambda i: (layer, 0, 0))
    return pl.pallas_call(
        functools.partial(_outproj_kernel, alpha=alpha),
        grid=(m // tm,),
        in_specs=[pl.BlockSpec((tm, da), lambda i: (i, 0)),
                  pl.BlockSpec((tm, db), lambda i: (i, 0)),
                  pl.BlockSpec((tm, d), lambda i: (i, 0)),
                  pl.BlockSpec((None, da, d), lambda i: (layer, 0, 0)),
                  pl.BlockSpec((None, db, d), lambda i: (layer, 1, 0)),
                  vec, vec],
        out_specs=pl.BlockSpec((tm, d), lambda i: (i, 0)),
        out_shape=jax.ShapeDtypeStruct((m, d), F32),
        compiler_params=_cparams(("parallel",)),
        name="outproj_ln",
    )(ha2, hb2, x2, w_out, w_out, ln_g, ln_b)


def _ffn_kernel(x_ref, wu_ref, wd_ref, g_ref, b_ref, y_ref, *rest, alpha):
    yb_ref = rest[0] if len(rest) == 3 else None
    xb_ref, acc_ref = rest[-2:]
    j = pl.program_id(1)

    @pl.when(j == 0)
    def _():
        xb_ref[...] = x_ref[...].astype(BF16)
        acc_ref[...] = jnp.zeros_like(acc_ref)

    u = jnp.dot(xb_ref[...], wu_ref[...], preferred_element_type=F32)
    u = jnp.maximum(u, 0.0)
    acc_ref[...] += jnp.dot((u * u).astype(BF16), wd_ref[...], preferred_element_type=F32)

    @pl.when(j == pl.num_programs(1) - 1)
    def _():
        y = _layer_norm(alpha * x_ref[...] + acc_ref[...], g_ref[...], b_ref[...])
        y_ref[...] = y
        if yb_ref is not None:
            yb_ref[...] = y.astype(BF16)


def _ffn(x2, w_up, w_down, ln_g, ln_b, layer, alpha, tm_pref, tf_pref, with_bf16):
    m, d = x2.shape
    f = w_up.shape[1]
    tm = _tile(m, tm_pref)
    tf = _tile(f, tf_pref)
    vec = pl.BlockSpec((None, 1, d), lambda i, j: (layer, 0, 0))
    rows = pl.BlockSpec((tm, d), lambda i, j: (i, 0))
    n_out = 2 if with_bf16 else 1
    outs = pl.pallas_call(
        functools.partial(_ffn_kernel, alpha=alpha),
        grid=(m // tm, f // tf),
        in_specs=[rows,
                  pl.BlockSpec((d, tf), lambda i, j: (0, j)),
                  pl.BlockSpec((tf, d), lambda i, j: (j, 0)),
                  vec, vec],
        out_specs=[rows, rows][:n_out],
        out_shape=[jax.ShapeDtypeStruct((m, d), F32), jax.ShapeDtypeStruct((m, d), BF16)][:n_out],
        scratch_shapes=[pltpu.VMEM((tm, d), BF16), pltpu.VMEM((tm, d), F32)],
        compiler_params=_cparams(("parallel", "arbitrary")),
        name="ffn_ln",
    )(x2, w_up, w_down, ln_g, ln_b)
    return (outs[0], outs[1]) if with_bf16 else (outs[0], None)


def _layer(x, xb, layer, last, cache, kv_prev, mlp_w, wts, dims, tiles):
    h_a, dk, dv, h_b, dh = dims
    (w_main, w_gate, col_scale, b_gate, g_head, w_out, ln1_g, ln1_b, ln2_g, ln2_b, alpha) = wts
    b, l, d = x.shape
    m = b * l
    d_b = h_b * dh
    x2 = x.reshape(m, d)

    x_in, tm_in = (x2, tiles["tm_in"]) if xb is None else (xb, tiles["tm_in_bf16"])
    slab, k_all, v_all, gpre = _inproj(x_in, w_main, w_gate, col_scale, layer, d_b, kv_prev, tm_in)
    slab3 = slab.reshape(b, l, slab.shape[1])
    n_rows = 2 * h_a + h_b
    lf, gc, gr, qb_all, kb_all = _gates(gpre.reshape(b, l, GATE_COLS), b_gate, layer, h_a, n_rows,
                                        _bias_placement(2 * h_a, h_b))
    q_col = (2 * h_a * dk + 2 * h_a * dv) // dh

    if cache is None:
        c0 = jnp.zeros((b, h_a, dv, dk), F32)
        n0 = jnp.zeros((b, h_a, dk), F32)
        m0 = jnp.zeros((b, h_a), F32)
        hb, mlp_w = _fox_prompt(slab3, qb_all, kb_all, h_b, dh, q_col, tiles["tq"], tiles["fox_heads"], mlp_w, layer)
    else:
        ck, cv, clf, c_all, n_all, m_all = cache
        c0, n0, m0 = c_all[layer], n_all[layer], m_all[layer]
        hb = _fox_sample(slab3, ck, cv, clf, layer, gc, h_b, dh, q_col, 2 * h_a)
    ha, c, n, mm = _mlstm(slab3, gc, gr, g_head, layer, c0, n0, m0, h_a, dk, dv, tiles["lc"])

    x1 = _outproj(ha.reshape(m, -1), hb.reshape(m, -1), x2, w_out, ln1_g, ln1_b, layer, alpha, tiles["tm_out"])
    y, yb = _ffn(x1, mlp_w[0], mlp_w[1], ln2_g, ln2_b, layer, alpha, tiles["tm_ffn"], tiles["tf_ffn"], not last)
    lfb = lf[:, :, 2 * h_a:n_rows]
    return y.reshape(b, l, d), yb, (k_all, v_all), (lfb, c, n, mm), mlp_w


def kernel(x_prompt, x_sample, cache_fox_k, cache_fox_v, cache_fox_logf, state_mlstm_c, state_mlstm_n,
           state_mlstm_m, w_in, b_gates, g_mlstm, w_out, ln1_g, ln1_b, w_up, w_down, ln2_g, ln2_b):
    depth, d_model, _ = w_in.shape
    _, _, _, h_b, dh = cache_fox_k.shape
    _, _, h_a, dv, dk = state_mlstm_c.shape
    d_a = h_a * dv
    d_b = h_b * dh
    dims = (h_a, dk, dv, h_b, dh)
    alpha = (2 * depth) ** 0.25
    widths = (h_a * dk, h_a * dk, d_a, d_a, h_a, h_a, d_b, d_b, d_b, h_b)
    offs = [0]
    for w in widths:
        offs.append(offs[-1] + w)
    n_gates = 2 * h_a + h_b

    tiles = dict(tm_in=512, tm_in_bf16=1024, tn_in=1024, tq=1024, fox_heads=2, lc=256, tm_out=512, tm_ffn=512, tf_ffn=1024)

    w_main, w_gate = _win_prep(w_in, ((offs[0], offs[4]), (offs[6], offs[9])), ((offs[4], offs[6]), (offs[9], offs[10])),
                               tiles["tn_in"])
    b_gate = jnp.pad(b_gates, ((0, 0), (0, GATE_COLS - n_gates))).reshape(depth, 1, GATE_COLS)
    n_main = w_main.shape[1] * w_main.shape[3]
    col_scale = jnp.concatenate([jnp.full((h_a * dk,), dk ** -0.5, F32),
                                 jnp.ones((offs[4] - offs[1],), F32),
                                 jnp.full((d_b,), dh ** -0.5 * LOG2E, F32),
                                 jnp.ones((2 * d_b,), F32)]).reshape(1, n_main)
    wts = (w_main, w_gate, col_scale, b_gate, g_mlstm.reshape(depth, 1, d_a), w_out.astype(BF16),
           ln1_g.reshape(depth, 1, d_model), ln1_b.reshape(depth, 1, d_model),
           ln2_g.reshape(depth, 1, d_model), ln2_b.reshape(depth, 1, d_model), alpha)
    cache = (cache_fox_k, cache_fox_v, cache_fox_logf, state_mlstm_c, state_mlstm_n, state_mlstm_m)


    yp, ys = x_prompt, x_sample
    ypb, ysb = x_prompt.reshape(-1, d_model).astype(BF16), None
    kv_p = kv_s = None
    small_p, small_s = [], []
    for layer in range(depth):
        last = layer == depth - 1
        yp, ypb, kv_p, st_p, mlp_bf16 = _layer(yp, ypb, layer, last, None, kv_p, (w_up, w_down), wts, dims, tiles)
        ys, ysb, kv_s, st_s, _ = _layer(ys, ysb, layer, last, cache, kv_s, mlp_bf16, wts, dims, tiles)
        small_p.append(st_p)
        small_s.append(st_s)
    stack = lambda states, i: jnp.stack([s[i] for s in states], axis=0)
    bp, lp, _ = x_prompt.shape
    bs, ls, _ = x_sample.shape
    return ((yp, ys)
            + tuple(a.reshape(depth, bp, lp, h_b, dh) for a in kv_p) + tuple(stack(small_p, i) for i in range(4))
            + tuple(a.reshape(depth, bs, ls, h_b, dh) for a in kv_s) + tuple(stack(small_s, i) for i in range(4)))
```

```python
import functools
import math

import jax
import jax.numpy as jnp
from jax import lax
from jax.experimental import pallas as pl
from jax.experimental.pallas import tpu as pltpu

F32 = jnp.float32
BF16 = jnp.bfloat16

LN_EPS = 1e-5
HEAD_NORM_EPS = 1e-6
LOG2E = math.log2(math.e)
LANES = 128
GATE_COLS = LANES
VMEM_LIMIT = 56 * 1024 * 1024


def _cparams(sem):
    return pltpu.CompilerParams(dimension_semantics=sem, vmem_limit_bytes=VMEM_LIMIT)


def _tile(n, pref):
    t = min(n, pref)
    assert n % t == 0, (n, pref)
    return t


def _log_sigmoid(g):
    return jnp.minimum(g, 0.0) - jnp.log1p(jnp.exp(-jnp.abs(g)))


def _dot_nt(a, b):
    return lax.dot_general(a, b, (((1,), (1,)), ((), ())), preferred_element_type=F32)


def _dot_tn(a, b):
    return lax.dot_general(a, b, (((0,), (0,)), ((), ())), preferred_element_type=F32)


def _layer_norm(z, g, b):
    mu = jnp.mean(z, axis=-1, keepdims=True)
    zc = z - mu
    var = jnp.mean(zc * zc, axis=-1, keepdims=True)
    return zc * lax.rsqrt(var + LN_EPS) * g + b


def _prefix_sum_rows(v):
    row = lax.broadcasted_iota(jnp.int32, v.shape, 0)
    s = 1
    while s < v.shape[0]:
        v = v + jnp.where(row >= s, pltpu.roll(v, s, 0), 0.0)
        s *= 2
    return v


def _split3(f):
    hi = f.astype(BF16).astype(F32)
    r = f - hi
    mid = r.astype(BF16).astype(F32)
    lo = (r - mid).astype(BF16).astype(F32)
    return hi, mid, lo


def _bias_lanes(f, query_side):
    hi, mid, lo = _split3(f)
    lane = lax.broadcasted_iota(jnp.int32, (f.shape[0], LANES), 1)
    if query_side:
        a = jnp.where(lane == 0, hi, jnp.where(lane == 1, mid, jnp.where(lane == 2, lo, jnp.where(lane < 6, 1.0, 0.0))))
    else:
        a = jnp.where(lane == 3, -hi, jnp.where(lane == 4, -mid, jnp.where(lane == 5, -lo, jnp.where(lane < 3, 1.0, 0.0))))
    return a.astype(BF16)


def _pick_lane(x, idx):
    lane = lax.broadcasted_iota(jnp.int32, x.shape, 1)
    return jnp.sum(jnp.where(lane == idx, x, 0.0), axis=1, keepdims=True)


def _win_prep_kernel(wt_ref, main_ref, gate_ref, *, main_cols, gate_cols):
    nj, tk, tn = main_ref.shape
    starts = [a + off for a, b in main_cols for off in range(0, b - a, tn)]
    for j in range(nj):
        main_ref[j] = wt_ref[starts[j]:starts[j] + tn, :].T.astype(BF16)
    gates = [wt_ref[a:b, :] for a, b in gate_cols]
    pad = GATE_COLS - sum(b - a for a, b in gate_cols)
    gate_ref[...] = jnp.concatenate(gates + [jnp.zeros((pad, tk), F32)], axis=0).T.astype(BF16)


def _win_prep(w_in, main_cols, gate_cols, tn):
    depth, d, n_in = w_in.shape
    assert all((b - a) % tn == 0 and a % 8 == 0 for a, b in main_cols)
    assert all((b - a) % 8 == 0 and a % 8 == 0 for a, b in gate_cols)
    n_main = sum(b - a for a, b in main_cols)
    nj = n_main // tn
    tk = _tile(d, 256)
    return pl.pallas_call(
        functools.partial(_win_prep_kernel, main_cols=main_cols, gate_cols=gate_cols),
        grid=(depth, d // tk),
        in_specs=[pl.BlockSpec((None, n_in, tk), lambda l, t: (l, 0, t))],
        out_specs=[pl.BlockSpec((None, nj, tk, tn), lambda l, t: (l, 0, t, 0)),
                   pl.BlockSpec((None, tk, GATE_COLS), lambda l, t: (l, t, 0))],
        out_shape=[jax.ShapeDtypeStruct((depth, nj, d, tn), BF16),
                   jax.ShapeDtypeStruct((depth, d, GATE_COLS), BF16)],
        compiler_params=_cparams(("parallel", "parallel")),
        name="win_prep",
    )(jnp.swapaxes(w_in, 1, 2))


def _inproj_kernel(*refs, jk0, jv0, aliased):
    x_ref, w_ref, wg_ref, cs_ref = refs[:4]
    del aliased
    j = pl.program_id(1)
    if x_ref.dtype == BF16:
        slab_ref, k_ref, v_ref, g_ref = refs[-4:]
        xb_ref = x_ref

        @pl.when(j == 0)
        def _():
            g_ref[...] = jnp.dot(x_ref[...], wg_ref[...], preferred_element_type=F32)
    else:
        slab_ref, k_ref, v_ref, g_ref, xb_ref = refs[-5:]

        @pl.when(j == 0)
        def _():
            xb = x_ref[...].astype(BF16)
            xb_ref[...] = xb
            g_ref[...] = jnp.dot(xb, wg_ref[...], preferred_element_type=F32)

    def tile(f32_ref):
        acc = jnp.dot(xb_ref[...], w_ref[...], preferred_element_type=F32)
        slab_ref[...] = (acc * cs_ref[...]).astype(BF16)
        if f32_ref is not None:
            f32_ref[...] = acc

    pl.when(j < jk0)(lambda: tile(None))
    pl.when((j >= jk0) & (j < jv0))(lambda: tile(k_ref))
    pl.when(j >= jv0)(lambda: tile(v_ref))


def _inproj(x2, w_main, w_gate, col_scale, layer, d_b, kv_prev, tm_pref):
    m, d = x2.shape
    depth, nj, _, tn = w_main.shape
    n = nj * tn
    tm = _tile(m, tm_pref)
    assert d_b % tn == 0
    nkv = d_b // tn
    jk0 = nj - 2 * nkv
    jv0 = nj - nkv
    aliased = kv_prev is not None
    in_specs = [pl.BlockSpec((tm, d), lambda i, j: (i, 0)),
                pl.BlockSpec((None, None, d, tn), lambda i, j: (layer, j, 0, 0)),
                pl.BlockSpec((None, d, GATE_COLS), lambda i, j: (layer, 0, 0)),
                pl.BlockSpec((1, tn), lambda i, j: (0, j))]
    args = [x2, w_main, w_gate, col_scale]
    aliases = {}
    if aliased:
        in_specs += [pl.BlockSpec(memory_space=pl.ANY), pl.BlockSpec(memory_space=pl.ANY)]
        args += list(kv_prev)
        aliases = {4: 1, 5: 2}
    return pl.pallas_call(
        functools.partial(_inproj_kernel, jk0=jk0, jv0=jv0, aliased=aliased),
        grid=(m // tm, nj),
        in_specs=in_specs,
        out_specs=[pl.BlockSpec((tm, tn), lambda i, j: (i, j)),
                   pl.BlockSpec((None, tm, tn), lambda i, j: (layer, i, jnp.clip(j - jk0, 0, nkv - 1))),
                   pl.BlockSpec((None, tm, tn), lambda i, j: (layer, i, jnp.clip(j - jv0, 0, nkv - 1))),
                   pl.BlockSpec((tm, GATE_COLS), lambda i, j: (i, 0))],
        out_shape=[jax.ShapeDtypeStruct((m, n), BF16),
                   jax.ShapeDtypeStruct((depth, m, d_b), F32),
                   jax.ShapeDtypeStruct((depth, m, d_b), F32),
                   jax.ShapeDtypeStruct((m, GATE_COLS), F32)],
        scratch_shapes=[] if x2.dtype == BF16 else [pltpu.VMEM((tm, d), BF16)],
        input_output_aliases=aliases,
        compiler_params=_cparams(("parallel", "arbitrary")),
        name="inproj",
    )(*args)


BIAS_LANES = 6


def _bias_placement(lane0, n_heads):
    import numpy as np
    place = np.zeros((3 * LANES, 2 * LANES), np.float32)
    ones = np.zeros((2, LANES), np.float32)
    for h in range(n_heads):
        for t in range(3):
            place[t * LANES + lane0 + h, BIAS_LANES * h + t] = 1.0
            place[t * LANES + lane0 + h, LANES + BIAS_LANES * h + 3 + t] = -1.0
            ones[0, BIAS_LANES * h + 3 + t] = 1.0
            ones[1, BIAS_LANES * h + t] = 1.0
    return jnp.asarray(place, BF16), jnp.asarray(ones, F32)


def _gates_kernel(gp_ref, bg_ref, place_ref, ones_ref, lf_ref, gc_ref, gr_ref, qb_ref, kb_ref, carry_ref,
                  *, n_raw, n_rows):
    t = pl.program_id(1)

    @pl.when(t == 0)
    def _():
        carry_ref[...] = jnp.zeros_like(carry_ref)

    g = gp_ref[...] + bg_ref[...]
    tl = g.shape[0]
    lane = lax.broadcasted_iota(jnp.int32, g.shape, 1)
    raw = lane < n_raw
    val = jnp.where(raw, g, _log_sigmoid(g))
    lf_ref[...] = val
    c = _prefix_sum_rows(val) + carry_ref[...]
    carry_ref[...] = c[tl - 1:tl, :]
    out = jnp.where(raw, g, c)
    gc_ref[...] = out
    gr_ref[...] = out.T[:n_rows, :]
    split = jnp.concatenate(_split3(c * LOG2E), axis=1).astype(BF16)
    moved = jnp.dot(split, place_ref[...], preferred_element_type=F32)
    qb_ref[...] = (moved[:, :LANES] + ones_ref[0:1, :]).astype(BF16)
    kb_ref[...] = (moved[:, LANES:] + ones_ref[1:2, :]).astype(BF16)


def _gates(gpre3, b_gate, layer, n_raw, n_rows, placement):
    b, l, _ = gpre3.shape
    tl = _tile(l, 512)
    place, ones = placement
    const = lambda i, t: (0, 0)
    tile = pl.BlockSpec((None, tl, GATE_COLS), lambda i, t: (i, t, 0))
    return pl.pallas_call(
        functools.partial(_gates_kernel, n_raw=n_raw, n_rows=n_rows),
        grid=(b, l // tl),
        in_specs=[tile,
                  pl.BlockSpec((None, 1, GATE_COLS), lambda i, t: (layer, 0, 0)),
                  pl.BlockSpec(place.shape, const),
                  pl.BlockSpec(ones.shape, const)],
        out_specs=[tile, tile, pl.BlockSpec((None, n_rows, tl), lambda i, t: (i, 0, t)), tile, tile],
        out_shape=[jax.ShapeDtypeStruct((b, l, GATE_COLS), F32),
                   jax.ShapeDtypeStruct((b, l, GATE_COLS), F32),
                   jax.ShapeDtypeStruct((b, n_rows, l), F32),
                   jax.ShapeDtypeStruct((b, l, GATE_COLS), BF16),
                   jax.ShapeDtypeStruct((b, l, GATE_COLS), BF16)],
        scratch_shapes=[pltpu.VMEM((1, GATE_COLS), F32)],
        compiler_params=_cparams(("parallel", "arbitrary")),
        name="gates",
    )(gpre3, b_gate, place, ones)


def _mlstm_kernel(q_ref, k_ref, v_ref, o_ref, gc_ref, gr_ref, gh_ref, c0_ref, n0_ref, m0_ref,
                  h_ref, c_ref, n_ref, m_ref, cta_s, m_s, fp_s, *, n_heads, dk, dv):
    c_idx = pl.program_id(1)
    n_chunks = pl.num_programs(1)
    lc = q_ref.shape[0]

    @pl.when(c_idx == 0)
    def _():
        lane = lax.broadcasted_iota(jnp.int32, (dk, LANES), 1)
        for h in range(n_heads):
            cta_s[h, :, :dv] = c0_ref[h].T
            n_cols = jnp.broadcast_to(n0_ref[h:h + 1, :], (LANES, dk)).T
            cta_s[h, :, dv:] = jnp.where(lane == 0, n_cols, 0.0)
        m_s[...] = m0_ref[...]
        fp_s[...] = jnp.zeros_like(fp_s)

    gc = gc_ref[...]
    gr = gr_ref[...]
    fp = fp_s[...]
    m_prev = m_s[...]
    row1 = lax.broadcasted_iota(jnp.int32, gc.shape, 0)
    b = gc - fp
    ig = pltpu.roll(gc, n_heads, 1)
    run = ig - b
    step = 1
    while step < lc:
        run = jnp.maximum(run, jnp.where(row1 >= step, pltpu.roll(run, step, 0), -jnp.inf))
        step *= 2
    g = b + m_prev
    m_t = jnp.maximum(g, b + run)
    w_inter = jnp.exp(g - m_t)
    e_neg = jnp.exp(-m_t)
    m_new = m_t[lc - 1:lc, :]
    b_last = b[lc - 1:lc, :]
    a_all = jnp.exp(b_last - b + ig - m_new)
    decay = jnp.exp(b_last + m_prev - m_new)
    m_s[...] = m_new
    fp_s[...] = gc[lc - 1:lc, :]

    row = lax.broadcasted_iota(jnp.int32, (lc, lc), 0)
    col = lax.broadcasted_iota(jnp.int32, (lc, lc), 1)
    causal = col <= row
    lane = lax.broadcasted_iota(jnp.int32, (lc, LANES), 1)
    ones_col = jnp.where(lane == 0, 1.0, 0.0).astype(BF16)
    for h in range(n_heads):
        hl = n_heads + h
        q = q_ref[:, h * dk:(h + 1) * dk]
        k = k_ref[:, h * dk:(h + 1) * dk]
        va = jnp.concatenate([v_ref[:, h * dv:(h + 1) * dv], ones_col], axis=1)
        ig_row = gr[h:h + 1, :]
        b_row = gr[hl:hl + 1, :] - fp[:, hl:hl + 1]
        d = jnp.where(causal, b[:, hl:hl + 1] - b_row + ig_row, -jnp.inf)
        s = _dot_nt(q, k) * jnp.exp(d - m_t[:, hl:hl + 1])
        cta = cta_s[h]
        both = (w_inter[:, hl:hl + 1] * jnp.dot(q, cta.astype(BF16), preferred_element_type=F32)
                + jnp.dot(s.astype(BF16), va, preferred_element_type=F32))
        hh = both[:, :dv] / jnp.maximum(jnp.abs(both[:, dv:dv + 1]), e_neg[:, hl:hl + 1])

        ka = (k.astype(F32) * a_all[:, hl:hl + 1]).astype(BF16)
        cta_s[h] = decay[:, hl:hl + 1] * cta + _dot_tn(ka, va)

        hn = hh * lax.rsqrt(jnp.mean(hh * hh, axis=1, keepdims=True) + HEAD_NORM_EPS)
        og = o_ref[:, h * dv:(h + 1) * dv].astype(F32)
        hn = hn * gh_ref[:, h * dv:(h + 1) * dv] * jax.nn.sigmoid(og)
        h_ref[:, h * dv:(h + 1) * dv] = hn.astype(h_ref.dtype)

    @pl.when(c_idx == n_chunks - 1)
    def _():
        for h in range(n_heads):
            c_ref[h] = cta_s[h, :, :dv].T
            n_ref[h:h + 1, :] = cta_s[h, :, dv:].T[0:1, :]
        m_ref[...] = m_s[...]


def _mlstm(slab3, gc, gr, g_head, layer, c0, n0, m0, n_heads, dk, dv, lc_pref):
    b, l, _ = slab3.shape
    lc = _tile(l, lc_pref)
    dq = n_heads * dk
    da = n_heads * dv
    assert da == 2 * dq and dk == LANES
    m0r = jnp.pad(m0, ((0, 0), (n_heads, GATE_COLS - 2 * n_heads))).reshape(b, 1, GATE_COLS)
    bmap = lambda i, c: (i, 0, 0)
    outs = pl.pallas_call(
        functools.partial(_mlstm_kernel, n_heads=n_heads, dk=dk, dv=dv),
        grid=(b, l // lc),
        in_specs=[pl.BlockSpec((None, lc, dq), lambda i, c: (i, c, 0)),
                  pl.BlockSpec((None, lc, dq), lambda i, c: (i, c, 1)),
                  pl.BlockSpec((None, lc, da), lambda i, c: (i, c, 1)),
                  pl.BlockSpec((None, lc, da), lambda i, c: (i, c, 2)),
                  pl.BlockSpec((None, lc, GATE_COLS), lambda i, c: (i, c, 0)),
                  pl.BlockSpec((None, gr.shape[1], lc), lambda i, c: (i, 0, c)),
                  pl.BlockSpec((None, 1, da), lambda i, c: (layer, 0, 0)),
                  pl.BlockSpec((None, n_heads, dv, dk), lambda i, c: (i, 0, 0, 0)),
                  pl.BlockSpec((None, n_heads, dk), bmap),
                  pl.BlockSpec((None, 1, GATE_COLS), bmap)],
        out_specs=[pl.BlockSpec((None, lc, da), lambda i, c: (i, c, 0)),
                   pl.BlockSpec((None, n_heads, dv, dk), lambda i, c: (i, 0, 0, 0)),
                   pl.BlockSpec((None, n_heads, dk), bmap),
                   pl.BlockSpec((None, 1, GATE_COLS), bmap)],
        out_shape=[jax.ShapeDtypeStruct((b, l, da), BF16),
                   jax.ShapeDtypeStruct((b, n_heads, dv, dk), F32),
                   jax.ShapeDtypeStruct((b, n_heads, dk), F32),
                   jax.ShapeDtypeStruct((b, 1, GATE_COLS), F32)],
        scratch_shapes=[pltpu.VMEM((n_heads, dk, dv + LANES), F32),
                        pltpu.VMEM((1, GATE_COLS), F32),
                        pltpu.VMEM((1, GATE_COLS), F32)],
        compiler_params=_cparams(("parallel", "arbitrary")),
        name="mlstm",
    )(slab3, slab3, slab3, slab3, gc, gr, g_head, c0, n0, m0r)
    h_a, c, n, m = outs
    return h_a, c, n, m[:, 0, n_heads:2 * n_heads]


def _fox_prompt_kernel(q_ref, k_ref, v_ref, qb_ref, kb_ref, *rest, tq, hp, cast_weights):
    if cast_weights:
        wu_ref, wd_ref, o_ref, wub_ref, wdb_ref = rest[:5]
        wub_ref[...] = wu_ref[...].astype(BF16)
        wdb_ref[...] = wd_ref[...].astype(BF16)
        rest = rest[5:]
    else:
        o_ref, rest = rest[0], rest[1:]
    qa_s, ka_s, va_s, p_s, m_s, al_s, acc_s = rest
    hg = pl.program_id(1)
    i = pl.program_id(2)
    dh = q_ref.shape[1] // hp

    @pl.when(i == 0)
    def _():
        kb = kb_ref[...]
        lane = lax.broadcasted_iota(jnp.int32, kb.shape, 1)
        ones_lane = jnp.where(lane == 0, 1.0, 0.0).astype(BF16)
        for u in range(hp):
            first = BIAS_LANES * (hg * hp + u)
            ka_s[u, :, :dh] = k_ref[:, u * dh:(u + 1) * dh]
            ka_s[u, :, dh:] = jnp.where((lane >= first) & (lane < first + BIAS_LANES), kb, jnp.zeros_like(kb))
            va_s[u, :, :dh] = v_ref[:, u * dh:(u + 1) * dh]
            va_s[u, :, dh:] = ones_lane

    for u in range(hp):
        qa_s[u, :, :dh] = q_ref[:, u * dh:(u + 1) * dh]
        qa_s[u, :, dh:] = qb_ref[...]

    def logits(u, j):
        start = pl.multiple_of(j * tq, tq)
        return _dot_nt(qa_s[u], ka_s[u, pl.ds(start, tq), :])

    def softmax(u, s, diagonal):
        if diagonal:
            row = lax.broadcasted_iota(jnp.int32, s.shape, 0)
            col = lax.broadcasted_iota(jnp.int32, s.shape, 1)
            s = jnp.where(col <= row, s, -jnp.inf)
            m_new = jnp.broadcast_to(jnp.max(s, axis=1, keepdims=True), (tq, LANES))
            al_s[u] = jnp.zeros_like(m_new)
        else:
            m = m_s[u]
            m_new = jnp.maximum(m, jnp.max(s, axis=1, keepdims=True))
            al_s[u] = jnp.exp2(m - m_new)
        m_s[u] = m_new
        p_s[u] = jnp.exp2(s - jnp.concatenate([m_new] * (tq // LANES), axis=1)).astype(BF16)

    def values(u, j):
        start = pl.multiple_of(j * tq, tq)
        pv = jnp.dot(p_s[u], va_s[u, pl.ds(start, tq), :], preferred_element_type=F32)
        return jnp.concatenate([al_s[u]] * (2 * dh // LANES), axis=1) * acc_s[u] + pv

    for u in range(hp):
        acc_s[u] = jnp.zeros(acc_s.shape[1:], F32)
        softmax(u, logits(u, i), True)

    def trip(j, _):
        prev = jnp.where(j == 0, i, j - 1)
        s = [logits(u, j) for u in range(hp)]
        for u in range(hp):
            acc_s[u] = values(u, prev)
        for u in range(hp):
            softmax(u, s[u], False)
        return 0

    lax.fori_loop(0, i, trip, 0)
    last = jnp.where(i == 0, i, i - 1)
    for u in range(hp):
        acc = values(u, last)
        o_ref[:, u * dh:(u + 1) * dh] = (acc[:, :dh] / acc[:, dh:dh + 1]).astype(o_ref.dtype)


def _fox_prompt(slab3, qb_all, kb_all, n_heads, dh, q_col, tq_pref, hp, mlp_w=None, layer=0):
    b, l, _ = slab3.shape
    tq = _tile(l, tq_pref)
    assert n_heads % hp == 0 and q_col % hp == 0
    qb, ng, w = q_col // hp, n_heads // hp, hp * dh
    nq = l // tq
    n_steps = b * ng * nq
    in_specs = [pl.BlockSpec((None, tq, w), lambda bi, h, i: (bi, i, qb + h)),
                pl.BlockSpec((None, l, w), lambda bi, h, i: (bi, 0, qb + ng + h)),
                pl.BlockSpec((None, l, w), lambda bi, h, i: (bi, 0, qb + 2 * ng + h)),
                pl.BlockSpec((None, tq, GATE_COLS), lambda bi, h, i: (bi, i, 0)),
                pl.BlockSpec((None, l, GATE_COLS), lambda bi, h, i: (bi, 0, 0))]
    out_specs = [pl.BlockSpec((None, tq, w), lambda bi, h, i: (bi, i, h))]
    out_shape = [jax.ShapeDtypeStruct((b, l, n_heads * dh), BF16)]
    args = [slab3, slab3, slab3, qb_all, kb_all]
    cast_weights = False
    if mlp_w is not None:
        w_up, w_down = mlp_w
        _, d, f = w_up.shape
        cast_weights = d % (16 * n_steps) == 0 and f % (16 * n_steps) == 0
    if cast_weights:
        ru, rd = d // n_steps, f // n_steps
        step = lambda bi, h, i: (bi * ng + h) * nq + i
        in_specs += [pl.BlockSpec((None, ru, f), lambda bi, h, i: (layer, step(bi, h, i), 0)),
                     pl.BlockSpec((None, rd, d), lambda bi, h, i: (layer, step(bi, h, i), 0))]
        out_specs += [pl.BlockSpec((ru, f), lambda bi, h, i: (step(bi, h, i), 0)),
                      pl.BlockSpec((rd, d), lambda bi, h, i: (step(bi, h, i), 0))]
        out_shape += [jax.ShapeDtypeStruct((d, f), BF16), jax.ShapeDtypeStruct((f, d), BF16)]
        args += [w_up, w_down]
    outs = pl.pallas_call(
        functools.partial(_fox_prompt_kernel, tq=tq, hp=hp, cast_weights=cast_weights),
        grid=(b, ng, nq),
        in_specs=in_specs,
        out_specs=out_specs,
        out_shape=out_shape,
        scratch_shapes=[pltpu.VMEM((hp, tq, 2 * dh), BF16), pltpu.VMEM((hp, l, 2 * dh), BF16),
                        pltpu.VMEM((hp, l, 2 * dh), BF16), pltpu.VMEM((hp, tq, tq), BF16),
                        pltpu.VMEM((hp, tq, LANES), F32), pltpu.VMEM((hp, tq, LANES), F32),
                        pltpu.VMEM((hp, tq, 2 * dh), F32)],
        compiler_params=_cparams(("parallel", "parallel", "arbitrary")),
        name="fox_prompt",
    )(*args)
    if cast_weights:
        return outs[0], (outs[1], outs[2])
    if mlp_w is not None:
        return outs[0], (mlp_w[0][layer].astype(BF16), mlp_w[1][layer].astype(BF16))
    return outs[0], None


def _fox_sample_kernel(q_ref, k_ref, v_ref, ck_ref, cv_ref, clf_ref, hcol_ref, gc_ref, o_ref, *, n_heads, dh, lane0):
    l = q_ref.shape[0]
    ph = ck_ref.shape[0]
    rows = n_heads * l
    cum = clf_ref[...]
    lane = lax.broadcasted_iota(jnp.int32, cum.shape, 1)
    step = n_heads
    while step < ph:
        cum = cum + jnp.where(lane >= step, pltpu.roll(cum, step, 1), 0.0)
        step *= 2

    gc = gc_ref[...]
    frame = lax.broadcasted_iota(jnp.int32, (l, 1), 0).astype(F32)
    stack = lambda pieces: jnp.concatenate(pieces, axis=0)
    qa = stack([q_ref[:, h * dh:(h + 1) * dh] for h in range(n_heads)])
    ka = stack([k_ref[:, h * dh:(h + 1) * dh] for h in range(n_heads)])
    va = stack([v_ref[:, h * dh:(h + 1) * dh] for h in range(n_heads)])
    f_col = stack([gc[:, lane0 + h:lane0 + h + 1] + cum[:, ph - n_heads + h:ph - n_heads + h + 1]
                   for h in range(n_heads)]) * LOG2E
    head_col = stack([jnp.full((l, 1), float(h), F32) for h in range(n_heads)])
    frame_col = stack([frame] * n_heads)
    as_row = lambda c: jnp.broadcast_to(c, (rows, rows)).T[0:1, :]
    f_row, head_row, frame_row = as_row(f_col), as_row(head_col), as_row(frame_col)

    s_old = _dot_nt(qa, ck_ref[...].astype(BF16)) + f_col - cum * LOG2E
    s_old = jnp.where(hcol_ref[...] == head_col, s_old, -jnp.inf)
    s_new = _dot_nt(qa, ka) + f_col - f_row
    s_new = jnp.where(head_row == head_col, jnp.where(frame_row <= frame_col, s_new, -jnp.inf), -jnp.inf)
    m = jnp.maximum(jnp.max(s_old, axis=1, keepdims=True), jnp.max(s_new, axis=1, keepdims=True))
    p_old = jnp.exp2(s_old - m)
    p_new = jnp.exp2(s_new - m)
    den = jnp.sum(p_old, axis=1, keepdims=True) + jnp.sum(p_new, axis=1, keepdims=True)
    acc = (jnp.dot(p_old.astype(BF16), cv_ref[...].astype(BF16), preferred_element_type=F32)
           + jnp.dot(p_new.astype(BF16), va, preferred_element_type=F32))
    out = (acc / den).astype(o_ref.dtype)
    for h in range(n_heads):
        o_ref[:, h * dh:(h + 1) * dh] = out[h * l:(h + 1) * l, :]


def _fox_sample(slab3, ck, cv, clf, layer, gc, n_heads, dh, q_col, lane0):
    b, l, _ = slab3.shape
    depth, _, past = ck.shape[:3]
    ph = past * n_heads
    dd = n_heads * dh
    qb = q_col // n_heads
    hcol = jnp.tile(jnp.arange(n_heads, dtype=F32), past).reshape(1, ph)
    cache_rows = pl.BlockSpec((None, None, ph, dh), lambda bi: (layer, bi, 0, 0))
    return pl.pallas_call(
        functools.partial(_fox_sample_kernel, n_heads=n_heads, dh=dh, lane0=lane0),
        grid=(b,),
        in_specs=[pl.BlockSpec((None, l, dd), lambda bi: (bi, 0, qb)),
                  pl.BlockSpec((None, l, dd), lambda bi: (bi, 0, qb + 1)),
                  pl.BlockSpec((None, l, dd), lambda bi: (bi, 0, qb + 2)),
                  cache_rows, cache_rows,
                  pl.BlockSpec((None, None, 1, ph), lambda bi: (layer, bi, 0, 0)),
                  pl.BlockSpec((1, ph), lambda bi: (0, 0)),
                  pl.BlockSpec((None, l, GATE_COLS), lambda bi: (bi, 0, 0))],
        out_specs=pl.BlockSpec((None, l, dd), lambda bi: (bi, 0, 0)),
        out_shape=jax.ShapeDtypeStruct((b, l, dd), BF16),
        compiler_params=_cparams(("parallel",)),
        name="fox_sample",
    )(slab3, slab3, slab3, ck.reshape(depth, b, ph, dh), cv.reshape(depth, b, ph, dh),
      clf.reshape(depth, b, 1, ph), hcol, gc)


def _outproj_kernel(ha_ref, hb_ref, x_ref, wa_ref, wb_ref, g_ref, b_ref, y_ref, *, alpha):
    tm = x_ref.shape[0]
    tr = min(tm, 256)
    for r in range(tm // tr):
        rows = slice(r * tr, (r + 1) * tr)
        mix = (jnp.dot(ha_ref[rows, :], wa_ref[...], preferred_element_type=F32)
               + jnp.dot(hb_ref[rows, :], wb_ref[...], preferred_element_type=F32))
        y_ref[rows, :] = _layer_norm(alpha * x_ref[rows, :] + mix, g_ref[...], b_ref[...])


def _outproj(ha2, hb2, x2, w_out, ln_g, ln_b, layer, alpha, tm_pref):
    m, d = x2.shape
    da = ha2.shape[1]
    db = hb2.shape[1]
    assert da == db
    tm = _tile(m, tm_pref)
    vec = pl.BlockSpec((None, 1, d), lambda i: (layer, 0, 0))
    return pl.pallas_call(
        functools.partial(_outproj_kernel, alpha=alpha),
        grid=(m // tm,),
        in_specs=[pl.BlockSpec((tm, da), lambda i: (i, 0)),
                  pl.BlockSpec((tm, db), lambda i: (i, 0)),
                  pl.BlockSpec((tm, d), lambda i: (i, 0)),
                  pl.BlockSpec((None, da, d), lambda i: (layer, 0, 0)),
                  pl.BlockSpec((None, db, d), lambda i: (layer, 1, 0)),
                  vec, vec],
        out_specs=pl.BlockSpec((tm, d), lambda i: (i, 0)),
        out_shape=jax.ShapeDtypeStruct((m, d), F32),
        compiler_params=_cparams(("parallel",)),
        name="outproj_ln",
    )(ha2, hb2, x2, w_out, w_out, ln_g, ln_b)


def _ffn_kernel(x_ref, wu_ref, wd_ref, g_ref, b_ref, y_ref, *rest, alpha):
    yb_ref = rest[0] if len(rest) == 3 else None
    xb_ref, acc_ref = rest[-2:]
    j = pl.program_id(1)

    @pl.when(j == 0)
    def _():
        xb_ref[...] = x_ref[...].astype(BF16)
        acc_ref[...] = jnp.zeros_like(acc_ref)

    u = jnp.dot(xb_ref[...], wu_ref[...], preferred_element_type=F32)
    u = jnp.maximum(u, 0.0)
    acc_ref[...] += jnp.dot((u * u).astype(BF16), wd_ref[...], preferred_element_type=F32)

    @pl.when(j == pl.num_programs(1) - 1)
    def _():
        y = _layer_norm(alpha * x_ref[...] + acc_ref[...], g_ref[...], b_ref[...])
        y_ref[...] = y
        if yb_ref is not None:
            yb_ref[...] = y.astype(BF16)


def _ffn(x2, w_up, w_down, ln_g, ln_b, layer, alpha, tm_pref, tf_pref, with_bf16):
    m, d = x2.shape
    f = w_up.shape[1]
    tm = _tile(m, tm_pref)
    tf = _tile(f, tf_pref)
    vec = pl.BlockSpec((None, 1, d), lambda i, j: (layer, 0, 0))
    rows = pl.BlockSpec((tm, d), lambda i, j: (i, 0))
    n_out = 2 if with_bf16 else 1
    outs = pl.pallas_call(
        functools.partial(_ffn_kernel, alpha=alpha),
        grid=(m // tm, f // tf),
        in_specs=[rows,
                  pl.BlockSpec((d, tf), lambda i, j: (0, j)),
                  pl.BlockSpec((tf, d), lambda i, j: (j, 0)),
                  vec, vec],
        out_specs=[rows, rows][:n_out],
        out_shape=[jax.ShapeDtypeStruct((m, d), F32), jax.ShapeDtypeStruct((m, d), BF16)][:n_out],
        scratch_shapes=[pltpu.VMEM((tm, d), BF16), pltpu.VMEM((tm, d), F32)],
        compiler_params=_cparams(("parallel", "arbitrary")),
        name="ffn_ln",
    )(x2, w_up, w_down, ln_g, ln_b)
    return (outs[0], outs[1]) if with_bf16 else (outs[0], None)


def _layer(x, xb, layer, last, cache, kv_prev, mlp_w, wts, dims, tiles):
    h_a, dk, dv, h_b, dh = dims
    (w_main, w_gate, col_scale, b_gate, g_head, w_out, ln1_g, ln1_b, ln2_g, ln2_b, alpha) = wts
    b, l, d = x.shape
    m = b * l
    d_b = h_b * dh
    x2 = x.reshape(m, d)

    x_in, tm_in = (x2, tiles["tm_in"]) if xb is None else (xb, tiles["tm_in_bf16"])
    slab, k_all, v_all, gpre = _inproj(x_in, w_main, w_gate, col_scale, layer, d_b, kv_prev, tm_in)
    slab3 = slab.reshape(b, l, slab.shape[1])
    n_rows = 2 * h_a + h_b
    lf, gc, gr, qb_all, kb_all = _gates(gpre.reshape(b, l, GATE_COLS), b_gate, layer, h_a, n_rows,
                                        _bias_placement(2 * h_a, h_b))
    q_col = (2 * h_a * dk + 2 * h_a * dv) // dh

    if cache is None:
        c0 = jnp.zeros((b, h_a, dv, dk), F32)
        n0 = jnp.zeros((b, h_a, dk), F32)
        m0 = jnp.zeros((b, h_a), F32)
        hb, mlp_w = _fox_prompt(slab3, qb_all, kb_all, h_b, dh, q_col, tiles["tq"], tiles["fox_heads"], mlp_w, layer)
    else:
        ck, cv, clf, c_all, n_all, m_all = cache
        c0, n0, m0 = c_all[layer], n_all[layer], m_all[layer]
        hb = _fox_sample(slab3, ck, cv, clf, layer, gc, h_b, dh, q_col, 2 * h_a)
    ha, c, n, mm = _mlstm(slab3, gc, gr, g_head, layer, c0, n0, m0, h_a, dk, dv, tiles["lc"])

    x1 = _outproj(ha.reshape(m, -1), hb.reshape(m, -1), x2, w_out, ln1_g, ln1_b, layer, alpha, tiles["tm_out"])
    y, yb = _ffn(x1, mlp_w[0], mlp_w[1], ln2_g, ln2_b, layer, alpha, tiles["tm_ffn"], tiles["tf_ffn"], not last)
    lfb = lf[:, :, 2 * h_a:n_rows]
    return y.reshape(b, l, d), yb, (k_all, v_all), (lfb, c, n, mm), mlp_w


def kernel(x_prompt, x_sample, cache_fox_k, cache_fox_v, cache_fox_logf, state_mlstm_c, state_mlstm_n,
           state_mlstm_m, w_in, b_gates, g_mlstm, w_out, ln1_g, ln1_b, w_up, w_down, ln2_g, ln2_b):
    depth, d_model, _ = w_in.shape
    _, _, _, h_b, dh = cache_fox_k.shape
    _, _, h_a, dv, dk = state_mlstm_c.shape
    d_a = h_a * dv
    d_b = h_b * dh
    dims = (h_a, dk, dv, h_b, dh)
    alpha = (2 * depth) ** 0.25
    widths = (h_a * dk, h_a * dk, d_a, d_a, h_a, h_a, d_b, d_b, d_b, h_b)
    offs = [0]
    for w in widths:
        offs.append(offs[-1] + w)
    n_gates = 2 * h_a + h_b

    tiles = dict(tm_in=512, tm_in_bf16=1024, tn_in=1024, tq=1024, fox_heads=2, lc=256, tm_out=512, tm_ffn=512, tf_ffn=1024)

    w_main, w_gate = _win_prep(w_in, ((offs[0], offs[4]), (offs[6], offs[9])), ((offs[4], offs[6]), (offs[9], offs[10])),
                               tiles["tn_in"])
    b_gate = jnp.pad(b_gates, ((0, 0), (0, GATE_COLS - n_gates))).reshape(depth, 1, GATE_COLS)
    n_main = w_main.shape[1] * w_main.shape[3]
    col_scale = jnp.concatenate([jnp.full((h_a * dk,), dk ** -0.5, F32),
                                 jnp.ones((offs[4] - offs[1],), F32),
                                 jnp.full((d_b,), dh ** -0.5 * LOG2E, F32),
                                 jnp.ones((2 * d_b,), F32)]).reshape(1, n_main)
    wts = (w_main, w_gate, col_scale, b_gate, g_mlstm.reshape(depth, 1, d_a), w_out.astype(BF16),
           ln1_g.reshape(depth, 1, d_model), ln1_b.reshape(depth, 1, d_model),
           ln2_g.reshape(depth, 1, d_model), ln2_b.reshape(depth, 1, d_model), alpha)
    cache = (cache_fox_k, cache_fox_v, cache_fox_logf, state_mlstm_c, state_mlstm_n, state_mlstm_m)


    yp, ys = x_prompt, x_sample
    ypb, ysb = x_prompt.reshape(-1, d_model).astype(BF16), None
    kv_p = kv_s = None
    small_p, small_s = [], []
    for layer in range(depth):
        last = layer == depth - 1
        yp, ypb, kv_p, st_p, mlp_bf16 = _layer(yp, ypb, layer, last, None, kv_p, (w_up, w_down), wts, dims, tiles)
        ys, ysb, kv_s, st_s, _ = _layer(ys, ysb, layer, last, cache, kv_s, mlp_bf16, wts, dims, tiles)
        small_p.append(st_p)
        small_s.append(st_s)
    stack = lambda states, i: jnp.stack([s[i] for s in states], axis=0)
    bp, lp, _ = x_prompt.shape
    bs, ls, _ = x_sample.shape
    return ((yp, ys)
            + tuple(a.reshape(depth, bp, lp, h_b, dh) for a in kv_p) + tuple(stack(small_p, i) for i in range(4))
            + tuple(a.reshape(depth, bs, ls, h_b, dh) for a in kv_s) + tuple(stack(small_s, i) for i in range(4)))
```

```python
import functools
import math

import jax
import jax.numpy as jnp
from jax import lax
from jax.experimental import pallas as pl
from jax.experimental.pallas import tpu as pltpu

F32 = jnp.float32
BF16 = jnp.bfloat16

LN_EPS = 1e-5
HEAD_NORM_EPS = 1e-6
LOG2E = math.log2(math.e)
LANES = 128
GATE_COLS = LANES
VMEM_LIMIT = 56 * 1024 * 1024


def _cparams(sem):
    return pltpu.CompilerParams(dimension_semantics=sem, vmem_limit_bytes=VMEM_LIMIT)


def _tile(n, pref):
    t = min(n, pref)
    assert n % t == 0, (n, pref)
    return t


def _log_sigmoid(g):
    return jnp.minimum(g, 0.0) - jnp.log1p(jnp.exp(-jnp.abs(g)))


def _dot_nt(a, b):
    return lax.dot_general(a, b, (((1,), (1,)), ((), ())), preferred_element_type=F32)


def _dot_tn(a, b):
    return lax.dot_general(a, b, (((0,), (0,)), ((), ())), preferred_element_type=F32)


def _layer_norm(z, g, b):
    mu = jnp.mean(z, axis=-1, keepdims=True)
    zc = z - mu
    var = jnp.mean(zc * zc, axis=-1, keepdims=True)
    return zc * lax.rsqrt(var + LN_EPS) * g + b


def _prefix_sum_rows(v):
    row = lax.broadcasted_iota(jnp.int32, v.shape, 0)
    s = 1
    while s < v.shape[0]:
        v = v + jnp.where(row >= s, pltpu.roll(v, s, 0), 0.0)
        s *= 2
    return v


def _split3(f):
    hi = f.astype(BF16).astype(F32)
    r = f - hi
    mid = r.astype(BF16).astype(F32)
    lo = (r - mid).astype(BF16).astype(F32)
    return hi, mid, lo


def _win_prep_kernel(wt_ref, main_ref, gate_ref, *, main_cols, gate_cols):
    nj, tk, tn = main_ref.shape
    starts = [a + off for a, b in main_cols for off in range(0, b - a, tn)]
    for j in range(nj):
        main_ref[j] = wt_ref[starts[j]:starts[j] + tn, :].T.astype(BF16)
    gates = [wt_ref[a:b, :] for a, b in gate_cols]
    pad = GATE_COLS - sum(b - a for a, b in gate_cols)
    gate_ref[...] = jnp.concatenate(gates + [jnp.zeros((pad, tk), F32)], axis=0).T.astype(BF16)


def _win_prep(w_in, main_cols, gate_cols, tn):
    depth, d, n_in = w_in.shape
    assert all((b - a) % tn == 0 and a % 8 == 0 for a, b in main_cols)
    assert all((b - a) % 8 == 0 and a % 8 == 0 for a, b in gate_cols)
    n_main = sum(b - a for a, b in main_cols)
    nj = n_main // tn
    tk = _tile(d, 256)
    return pl.pallas_call(
        functools.partial(_win_prep_kernel, main_cols=main_cols, gate_cols=gate_cols),
        grid=(depth, d // tk),
        in_specs=[pl.BlockSpec((None, n_in, tk), lambda l, t: (l, 0, t))],
        out_specs=[pl.BlockSpec((None, nj, tk, tn), lambda l, t: (l, 0, t, 0)),
                   pl.BlockSpec((None, tk, GATE_COLS), lambda l, t: (l, t, 0))],
        out_shape=[jax.ShapeDtypeStruct((depth, nj, d, tn), BF16),
                   jax.ShapeDtypeStruct((depth, d, GATE_COLS), BF16)],
        compiler_params=_cparams(("parallel", "parallel")),
        name="win_prep",
    )(jnp.swapaxes(w_in, 1, 2))


def _inproj_kernel(*refs, jk0, jv0, aliased):
    x_ref, w_ref, wg_ref, cs_ref = refs[:4]
    del aliased
    j = pl.program_id(1)
    if x_ref.dtype == BF16:
        slab_ref, k_ref, v_ref, g_ref = refs[-4:]
        xb_ref = x_ref

        @pl.when(j == 0)
        def _():
            g_ref[...] = jnp.dot(x_ref[...], wg_ref[...], preferred_element_type=F32)
    else:
        slab_ref, k_ref, v_ref, g_ref, xb_ref = refs[-5:]

        @pl.when(j == 0)
        def _():
            xb = x_ref[...].astype(BF16)
            xb_ref[...] = xb
            g_ref[...] = jnp.dot(xb, wg_ref[...], preferred_element_type=F32)

    def tile(f32_ref):
        acc = jnp.dot(xb_ref[...], w_ref[...], preferred_element_type=F32)
        slab_ref[...] = (acc * cs_ref[...]).astype(BF16)
        if f32_ref is not None:
            f32_ref[...] = acc

    pl.when(j < jk0)(lambda: tile(None))
    pl.when((j >= jk0) & (j < jv0))(lambda: tile(k_ref))
    pl.when(j >= jv0)(lambda: tile(v_ref))


def _inproj(x2, w_main, w_gate, col_scale, layer, d_b, kv_prev, tm_pref):
    m, d = x2.shape
    depth, nj, _, tn = w_main.shape
    n = nj * tn
    tm = _tile(m, tm_pref)
    assert d_b % tn == 0
    nkv = d_b // tn
    jk0 = nj - 2 * nkv
    jv0 = nj - nkv
    aliased = kv_prev is not None
    in_specs = [pl.BlockSpec((tm, d), lambda i, j: (i, 0)),
                pl.BlockSpec((None, None, d, tn), lambda i, j: (layer, j, 0, 0)),
                pl.BlockSpec((None, d, GATE_COLS), lambda i, j: (layer, 0, 0)),
                pl.BlockSpec((1, tn), lambda i, j: (0, j))]
    args = [x2, w_main, w_gate, col_scale]
    aliases = {}
    if aliased:
        in_specs += [pl.BlockSpec(memory_space=pl.ANY), pl.BlockSpec(memory_space=pl.ANY)]
        args += list(kv_prev)
        aliases = {4: 1, 5: 2}
    return pl.pallas_call(
        functools.partial(_inproj_kernel, jk0=jk0, jv0=jv0, aliased=aliased),
        grid=(m // tm, nj),
        in_specs=in_specs,
        out_specs=[pl.BlockSpec((tm, tn), lambda i, j: (i, j)),
                   pl.BlockSpec((None, tm, tn), lambda i, j: (layer, i, jnp.clip(j - jk0, 0, nkv - 1))),
                   pl.BlockSpec((None, tm, tn), lambda i, j: (layer, i, jnp.clip(j - jv0, 0, nkv - 1))),
                   pl.BlockSpec((tm, GATE_COLS), lambda i, j: (i, 0))],
        out_shape=[jax.ShapeDtypeStruct((m, n), BF16),
                   jax.ShapeDtypeStruct((depth, m, d_b), F32),
                   jax.ShapeDtypeStruct((depth, m, d_b), F32),
                   jax.ShapeDtypeStruct((m, GATE_COLS), F32)],
        scratch_shapes=[] if x2.dtype == BF16 else [pltpu.VMEM((tm, d), BF16)],
        input_output_aliases=aliases,
        compiler_params=_cparams(("parallel", "arbitrary")),
        name="inproj",
    )(*args)


BIAS_LANES = 6


def _bias_placement(lane0, n_heads):
    import numpy as np
    place = np.zeros((3 * LANES, 2 * LANES), np.float32)
    ones = np.zeros((2, LANES), np.float32)
    for h in range(n_heads):
        for t in range(3):
            place[t * LANES + lane0 + h, BIAS_LANES * h + t] = 1.0
            place[t * LANES + lane0 + h, LANES + BIAS_LANES * h + 3 + t] = -1.0
            ones[0, BIAS_LANES * h + 3 + t] = 1.0
            ones[1, BIAS_LANES * h + t] = 1.0
    return jnp.asarray(place, BF16), jnp.asarray(ones, F32)


def _gates_kernel(gp_ref, bg_ref, place_ref, ones_ref, gc_ref, gr_ref, qb_ref, kb_ref, carry_ref,
                  *, n_raw, n_rows, n_keep):
    t = pl.program_id(1)

    @pl.when(t == 0)
    def _():
        carry_ref[...] = jnp.zeros_like(carry_ref)

    g = gp_ref[...] + bg_ref[...]
    tl = g.shape[0]
    lane = lax.broadcasted_iota(jnp.int32, g.shape, 1)
    raw = lane < n_raw
    val = jnp.where(raw, g, _log_sigmoid(g))
    c = _prefix_sum_rows(val) + carry_ref[...]
    carry_ref[...] = c[tl - 1:tl, :]
    out = jnp.where(raw, g, c)
    gc_ref[...] = out
    gr_ref[...] = jnp.concatenate([out.T[:n_rows, :], val.T[n_rows - n_keep:n_rows, :]], axis=0)
    split = jnp.concatenate(_split3(c * LOG2E), axis=1).astype(BF16)
    moved = jnp.dot(split, place_ref[...], preferred_element_type=F32)
    qb_ref[...] = (moved[:, :LANES] + ones_ref[0:1, :]).astype(BF16)
    kb_ref[...] = (moved[:, LANES:] + ones_ref[1:2, :]).astype(BF16)


def _gates(gpre3, b_gate, layer, n_raw, n_rows, n_keep, placement):
    b, l, _ = gpre3.shape
    tl = _tile(l, 512)
    place, ones = placement
    const = lambda i, t: (0, 0)
    tile = pl.BlockSpec((None, tl, GATE_COLS), lambda i, t: (i, t, 0))
    return pl.pallas_call(
        functools.partial(_gates_kernel, n_raw=n_raw, n_rows=n_rows, n_keep=n_keep),
        grid=(b, l // tl),
        in_specs=[tile,
                  pl.BlockSpec((None, 1, GATE_COLS), lambda i, t: (layer, 0, 0)),
                  pl.BlockSpec(place.shape, const),
                  pl.BlockSpec(ones.shape, const)],
        out_specs=[tile, pl.BlockSpec((None, n_rows + n_keep, tl), lambda i, t: (i, 0, t)), tile, tile],
        out_shape=[jax.ShapeDtypeStruct((b, l, GATE_COLS), F32),
                   jax.ShapeDtypeStruct((b, n_rows + n_keep, l), F32),
                   jax.ShapeDtypeStruct((b, l, GATE_COLS), BF16),
                   jax.ShapeDtypeStruct((b, l, GATE_COLS), BF16)],
        scratch_shapes=[pltpu.VMEM((1, GATE_COLS), F32)],
        compiler_params=_cparams(("parallel", "arbitrary")),
        name="gates",
    )(gpre3, b_gate, place, ones)


def _mlstm_kernel(q_ref, k_ref, v_ref, o_ref, gc_ref, gr_ref, gh_ref, c0_ref, n0_ref, m0_ref,
                  h_ref, c_ref, n_ref, m_ref, cta_s, m_s, fp_s, *, n_heads, dk, dv):
    c_idx = pl.program_id(1)
    n_chunks = pl.num_programs(1)
    lc = q_ref.shape[0]

    @pl.when(c_idx == 0)
    def _():
        lane = lax.broadcasted_iota(jnp.int32, (dk, LANES), 1)
        for h in range(n_heads):
            cta_s[h, :, :dv] = c0_ref[h].T
            n_cols = jnp.broadcast_to(n0_ref[h:h + 1, :], (LANES, dk)).T
            cta_s[h, :, dv:] = jnp.where(lane == 0, n_cols, 0.0)
        m_s[...] = m0_ref[...]
        fp_s[...] = jnp.zeros_like(fp_s)

    gc = gc_ref[...]
    gr = gr_ref[...]
    fp = fp_s[...]
    m_prev = m_s[...]
    row1 = lax.broadcasted_iota(jnp.int32, gc.shape, 0)
    b = gc - fp
    ig = pltpu.roll(gc, n_heads, 1)
    run = ig - b
    step = 1
    while step < lc:
        run = jnp.maximum(run, jnp.where(row1 >= step, pltpu.roll(run, step, 0), -jnp.inf))
        step *= 2
    g = b + m_prev
    m_t = jnp.maximum(g, b + run)
    w_inter = jnp.exp(g - m_t)
    e_neg = jnp.exp(-m_t)
    m_new = m_t[lc - 1:lc, :]
    b_last = b[lc - 1:lc, :]
    a_all = jnp.exp(b_last - b + ig - m_new)
    decay = jnp.exp(b_last + m_prev - m_new)
    m_s[...] = m_new
    fp_s[...] = gc[lc - 1:lc, :]

    row = lax.broadcasted_iota(jnp.int32, (lc, lc), 0)
    col = lax.broadcasted_iota(jnp.int32, (lc, lc), 1)
    causal = col <= row
    lane = lax.broadcasted_iota(jnp.int32, (lc, LANES), 1)
    ones_col = jnp.where(lane == 0, 1.0, 0.0).astype(BF16)
    for h in range(n_heads):
        hl = n_heads + h
        q = q_ref[:, h * dk:(h + 1) * dk]
        k = k_ref[:, h * dk:(h + 1) * dk]
        va = jnp.concatenate([v_ref[:, h * dv:(h + 1) * dv], ones_col], axis=1)
        ig_row = gr[h:h + 1, :]
        b_row = gr[hl:hl + 1, :] - fp[:, hl:hl + 1]
        d = jnp.where(causal, b[:, hl:hl + 1] - b_row + ig_row, -jnp.inf)
        s = _dot_nt(q, k) * jnp.exp(d - m_t[:, hl:hl + 1])
        cta = cta_s[h]
        both = (w_inter[:, hl:hl + 1] * jnp.dot(q, cta.astype(BF16), preferred_element_type=F32)
                + jnp.dot(s.astype(BF16), va, preferred_element_type=F32))
        hh = both[:, :dv] / jnp.maximum(jnp.abs(both[:, dv:dv + 1]), e_neg[:, hl:hl + 1])

        ka = (k.astype(F32) * a_all[:, hl:hl + 1]).astype(BF16)
        cta_s[h] = decay[:, hl:hl + 1] * cta + _dot_tn(ka, va)

        hn = hh * lax.rsqrt(jnp.mean(hh * hh, axis=1, keepdims=True) + HEAD_NORM_EPS)
        og = o_ref[:, h * dv:(h + 1) * dv].astype(F32)
        hn = hn * gh_ref[:, h * dv:(h + 1) * dv] * jax.nn.sigmoid(og)
        h_ref[:, h * dv:(h + 1) * dv] = hn.astype(h_ref.dtype)

    @pl.when(c_idx == n_chunks - 1)
    def _():
        for h in range(n_heads):
            c_ref[h] = cta_s[h, :, :dv].T
            n_ref[h:h + 1, :] = cta_s[h, :, dv:].T[0:1, :]
        m_ref[...] = m_s[...]


def _mlstm(slab3, gc, gr, g_head, layer, c0, n0, m0, n_heads, dk, dv, lc_pref):
    b, l, _ = slab3.shape
    lc = _tile(l, lc_pref)
    dq = n_heads * dk
    da = n_heads * dv
    assert da == 2 * dq and dk == LANES
    m0r = jnp.pad(m0, ((0, 0), (n_heads, GATE_COLS - 2 * n_heads))).reshape(b, 1, GATE_COLS)
    bmap = lambda i, c: (i, 0, 0)
    outs = pl.pallas_call(
        functools.partial(_mlstm_kernel, n_heads=n_heads, dk=dk, dv=dv),
        grid=(b, l // lc),
        in_specs=[pl.BlockSpec((None, lc, dq), lambda i, c: (i, c, 0)),
                  pl.BlockSpec((None, lc, dq), lambda i, c: (i, c, 1)),
                  pl.BlockSpec((None, lc, da), lambda i, c: (i, c, 1)),
                  pl.BlockSpec((None, lc, da), lambda i, c: (i, c, 2)),
                  pl.BlockSpec((None, lc, GATE_COLS), lambda i, c: (i, c, 0)),
                  pl.BlockSpec((None, gr.shape[1], lc), lambda i, c: (i, 0, c)),
                  pl.BlockSpec((None, 1, da), lambda i, c: (layer, 0, 0)),
                  pl.BlockSpec((None, n_heads, dv, dk), lambda i, c: (i, 0, 0, 0)),
                  pl.BlockSpec((None, n_heads, dk), bmap),
                  pl.BlockSpec((None, 1, GATE_COLS), bmap)],
        out_specs=[pl.BlockSpec((None, lc, da), lambda i, c: (i, c, 0)),
                   pl.BlockSpec((None, n_heads, dv, dk), lambda i, c: (i, 0, 0, 0)),
                   pl.BlockSpec((None, n_heads, dk), bmap),
                   pl.BlockSpec((None, 1, GATE_COLS), bmap)],
        out_shape=[jax.ShapeDtypeStruct((b, l, da), BF16),
                   jax.ShapeDtypeStruct((b, n_heads, dv, dk), F32),
                   jax.ShapeDtypeStruct((b, n_heads, dk), F32),
                   jax.ShapeDtypeStruct((b, 1, GATE_COLS), F32)],
        scratch_shapes=[pltpu.VMEM((n_heads, dk, dv + LANES), F32),
                        pltpu.VMEM((1, GATE_COLS), F32),
                        pltpu.VMEM((1, GATE_COLS), F32)],
        compiler_params=_cparams(("parallel", "arbitrary")),
        name="mlstm",
    )(slab3, slab3, slab3, slab3, gc, gr, g_head, c0, n0, m0r)
    h_a, c, n, m = outs
    return h_a, c, n, m[:, 0, n_heads:2 * n_heads]


def _fox_prompt_kernel(q_ref, k_ref, v_ref, qb_ref, kb_ref, *rest, tq, hp, cast_weights):
    if cast_weights:
        wu_ref, wd_ref, o_ref, wub_ref, wdb_ref = rest[:5]
        wub_ref[...] = wu_ref[...].astype(BF16)
        wdb_ref[...] = wd_ref[...].astype(BF16)
        rest = rest[5:]
    else:
        o_ref, rest = rest[0], rest[1:]
    qa_s, ka_s, va_s, p_s, m_s, al_s, acc_s = rest
    hg = pl.program_id(1)
    i = pl.program_id(2)
    dh = q_ref.shape[1] // hp

    @pl.when(i == 0)
    def _():
        kb = kb_ref[...]
        lane = lax.broadcasted_iota(jnp.int32, kb.shape, 1)
        ones_lane = jnp.where(lane == 0, 1.0, 0.0).astype(BF16)
        for u in range(hp):
            first = BIAS_LANES * (hg * hp + u)
            ka_s[u, :, :dh] = k_ref[:, u * dh:(u + 1) * dh]
            ka_s[u, :, dh:] = jnp.where((lane >= first) & (lane < first + BIAS_LANES), kb, jnp.zeros_like(kb))
            va_s[u, :, :dh] = v_ref[:, u * dh:(u + 1) * dh]
            va_s[u, :, dh:] = ones_lane

    for u in range(hp):
        qa_s[u, :, :dh] = q_ref[:, u * dh:(u + 1) * dh]
        qa_s[u, :, dh:] = qb_ref[...]

    def logits(u, j):
        start = pl.multiple_of(j * tq, tq)
        return _dot_nt(qa_s[u], ka_s[u, pl.ds(start, tq), :])

    def softmax(u, s, diagonal):
        if diagonal:
            row = lax.broadcasted_iota(jnp.int32, s.shape, 0)
            col = lax.broadcasted_iota(jnp.int32, s.shape, 1)
            s = jnp.where(col <= row, s, -jnp.inf)
            m_new = jnp.broadcast_to(jnp.max(s, axis=1, keepdims=True), (tq, LANES))
            al_s[u] = jnp.zeros_like(m_new)
        else:
            m = m_s[u]
            m_new = jnp.maximum(m, jnp.max(s, axis=1, keepdims=True))
            al_s[u] = jnp.exp2(m - m_new)
        m_s[u] = m_new
        p_s[u] = jnp.exp2(s - jnp.concatenate([m_new] * (tq // LANES), axis=1)).astype(BF16)

    def values(u, j):
        start = pl.multiple_of(j * tq, tq)
        pv = jnp.dot(p_s[u], va_s[u, pl.ds(start, tq), :], preferred_element_type=F32)
        return jnp.concatenate([al_s[u]] * (2 * dh // LANES), axis=1) * acc_s[u] + pv

    for u in range(hp):
        acc_s[u] = jnp.zeros(acc_s.shape[1:], F32)
        softmax(u, logits(u, i), True)

    def trip(j, _):
        prev = jnp.where(j == 0, i, j - 1)
        s = [logits(u, j) for u in range(hp)]
        for u in range(hp):
            acc_s[u] = values(u, prev)
        for u in range(hp):
            softmax(u, s[u], False)
        return 0

    lax.fori_loop(0, i, trip, 0)
    last = jnp.where(i == 0, i, i - 1)
    for u in range(hp):
        acc = values(u, last)
        o_ref[:, u * dh:(u + 1) * dh] = (acc[:, :dh] / acc[:, dh:dh + 1]).astype(o_ref.dtype)


def _fox_prompt(slab3, qb_all, kb_all, n_heads, dh, q_col, tq_pref, hp, mlp_w=None, layer=0):
    b, l, _ = slab3.shape
    tq = _tile(l, tq_pref)
    assert n_heads % hp == 0 and q_col % hp == 0
    qb, ng, w = q_col // hp, n_heads // hp, hp * dh
    nq = l // tq
    n_steps = b * ng * nq
    in_specs = [pl.BlockSpec((None, tq, w), lambda bi, h, i: (bi, i, qb + h)),
                pl.BlockSpec((None, l, w), lambda bi, h, i: (bi, 0, qb + ng + h)),
                pl.BlockSpec((None, l, w), lambda bi, h, i: (bi, 0, qb + 2 * ng + h)),
                pl.BlockSpec((None, tq, GATE_COLS), lambda bi, h, i: (bi, i, 0)),
                pl.BlockSpec((None, l, GATE_COLS), lambda bi, h, i: (bi, 0, 0))]
    out_specs = [pl.BlockSpec((None, tq, w), lambda bi, h, i: (bi, i, h))]
    out_shape = [jax.ShapeDtypeStruct((b, l, n_heads * dh), BF16)]
    args = [slab3, slab3, slab3, qb_all, kb_all]
    cast_weights = False
    if mlp_w is not None:
        w_up, w_down = mlp_w
        _, d, f = w_up.shape
        cast_weights = d % (16 * n_steps) == 0 and f % (16 * n_steps) == 0
    if cast_weights:
        ru, rd = d // n_steps, f // n_steps
        step = lambda bi, h, i: (bi * ng + h) * nq + i
        in_specs += [pl.BlockSpec((None, ru, f), lambda bi, h, i: (layer, step(bi, h, i), 0)),
                     pl.BlockSpec((None, rd, d), lambda bi, h, i: (layer, step(bi, h, i), 0))]
        out_specs += [pl.BlockSpec((ru, f), lambda bi, h, i: (step(bi, h, i), 0)),
                      pl.BlockSpec((rd, d), lambda bi, h, i: (step(bi, h, i), 0))]
        out_shape += [jax.ShapeDtypeStruct((d, f), BF16), jax.ShapeDtypeStruct((f, d), BF16)]
        args += [w_up, w_down]
    outs = pl.pallas_call(
        functools.partial(_fox_prompt_kernel, tq=tq, hp=hp, cast_weights=cast_weights),
        grid=(b, ng, nq),
        in_specs=in_specs,
        out_specs=out_specs,
        out_shape=out_shape,
        scratch_shapes=[pltpu.VMEM((hp, tq, 2 * dh), BF16), pltpu.VMEM((hp, l, 2 * dh), BF16),
                        pltpu.VMEM((hp, l, 2 * dh), BF16), pltpu.VMEM((hp, tq, tq), BF16),
                        pltpu.VMEM((hp, tq, LANES), F32), pltpu.VMEM((hp, tq, LANES), F32),
                        pltpu.VMEM((hp, tq, 2 * dh), F32)],
        compiler_params=_cparams(("parallel", "parallel", "arbitrary")),
        name="fox_prompt",
    )(*args)
    if cast_weights:
        return outs[0], (outs[1], outs[2])
    if mlp_w is not None:
        return outs[0], (mlp_w[0][layer].astype(BF16), mlp_w[1][layer].astype(BF16))
    return outs[0], None


def _fox_sample_kernel(q_ref, k_ref, v_ref, ck_ref, cv_ref, clf_ref, hcol_ref, gc_ref, o_ref, *, n_heads, dh, lane0):
    l = q_ref.shape[0]
    ph = ck_ref.shape[0]
    rows = n_heads * l
    cum = clf_ref[...]
    lane = lax.broadcasted_iota(jnp.int32, cum.shape, 1)
    step = n_heads
    while step < ph:
        cum = cum + jnp.where(lane >= step, pltpu.roll(cum, step, 1), 0.0)
        step *= 2

    gc = gc_ref[...]
    frame = lax.broadcasted_iota(jnp.int32, (l, 1), 0).astype(F32)
    stack = lambda pieces: jnp.concatenate(pieces, axis=0)
    qa = stack([q_ref[:, h * dh:(h + 1) * dh] for h in range(n_heads)])
    ka = stack([k_ref[:, h * dh:(h + 1) * dh] for h in range(n_heads)])
    va = stack([v_ref[:, h * dh:(h + 1) * dh] for h in range(n_heads)])
    f_col = stack([gc[:, lane0 + h:lane0 + h + 1] + cum[:, ph - n_heads + h:ph - n_heads + h + 1]
                   for h in range(n_heads)]) * LOG2E
    head_col = stack([jnp.full((l, 1), float(h), F32) for h in range(n_heads)])
    frame_col = stack([frame] * n_heads)
    as_row = lambda c: jnp.broadcast_to(c, (rows, rows)).T[0:1, :]
    f_row, head_row, frame_row = as_row(f_col), as_row(head_col), as_row(frame_col)

    s_old = _dot_nt(qa, ck_ref[...].astype(BF16)) + f_col - cum * LOG2E
    s_old = jnp.where(hcol_ref[...] == head_col, s_old, -jnp.inf)
    s_new = _dot_nt(qa, ka) + f_col - f_row
    s_new = jnp.where(head_row == head_col, jnp.where(frame_row <= frame_col, s_new, -jnp.inf), -jnp.inf)
    m = jnp.maximum(jnp.max(s_old, axis=1, keepdims=True), jnp.max(s_new, axis=1, keepdims=True))
    p_old = jnp.exp2(s_old - m)
    p_new = jnp.exp2(s_new - m)
    den = jnp.sum(p_old, axis=1, keepdims=True) + jnp.sum(p_new, axis=1, keepdims=True)
    acc = (jnp.dot(p_old.astype(BF16), cv_ref[...].astype(BF16), preferred_element_type=F32)
           + jnp.dot(p_new.astype(BF16), va, preferred_element_type=F32))
    out = (acc / den).astype(o_ref.dtype)
    for h in range(n_heads):
        o_ref[:, h * dh:(h + 1) * dh] = out[h * l:(h + 1) * l, :]


def _fox_sample(slab3, ck, cv, clf, layer, gc, n_heads, dh, q_col, lane0):
    b, l, _ = slab3.shape
    depth, _, past = ck.shape[:3]
    ph = past * n_heads
    dd = n_heads * dh
    qb = q_col // n_heads
    hcol = jnp.tile(jnp.arange(n_heads, dtype=F32), past).reshape(1, ph)
    cache_rows = pl.BlockSpec((None, None, ph, dh), lambda bi: (layer, bi, 0, 0))
    return pl.pallas_call(
        functools.partial(_fox_sample_kernel, n_heads=n_heads, dh=dh, lane0=lane0),
        grid=(b,),
        in_specs=[pl.BlockSpec((None, l, dd), lambda bi: (bi, 0, qb)),
                  pl.BlockSpec((None, l, dd), lambda bi: (bi, 0, qb + 1)),
                  pl.BlockSpec((None, l, dd), lambda bi: (bi, 0, qb + 2)),
                  cache_rows, cache_rows,
                  pl.BlockSpec((None, None, 1, ph), lambda bi: (layer, bi, 0, 0)),
                  pl.BlockSpec((1, ph), lambda bi: (0, 0)),
                  pl.BlockSpec((None, l, GATE_COLS), lambda bi: (bi, 0, 0))],
        out_specs=pl.BlockSpec((None, l, dd), lambda bi: (bi, 0, 0)),
        out_shape=jax.ShapeDtypeStruct((b, l, dd), BF16),
        compiler_params=_cparams(("parallel",)),
        name="fox_sample",
    )(slab3, slab3, slab3, ck.reshape(depth, b, ph, dh), cv.reshape(depth, b, ph, dh),
      clf.reshape(depth, b, 1, ph), hcol, gc)


def _outproj_kernel(ha_ref, hb_ref, x_ref, wa_ref, wb_ref, g_ref, b_ref, y_ref, *, alpha):
    tm = x_ref.shape[0]
    tr = min(tm, 256)
    for r in range(tm // tr):
        rows = slice(r * tr, (r + 1) * tr)
        mix = (jnp.dot(ha_ref[rows, :], wa_ref[...], preferred_element_type=F32)
               + jnp.dot(hb_ref[rows, :], wb_ref[...], preferred_element_type=F32))
        y_ref[rows, :] = _layer_norm(alpha * x_ref[rows, :] + mix, g_ref[...], b_ref[...])


def _outproj(ha2, hb2, x2, w_out, ln_g, ln_b, layer, alpha, tm_pref):
    m, d = x2.shape
    da = ha2.shape[1]
    db = hb2.shape[1]
    assert da == db
    tm = _tile(m, tm_pref)
    vec = pl.BlockSpec((None, 1, d), lambda i: (layer, 0, 0))
    return pl.pallas_call(
        functools.partial(_outproj_kernel, alpha=alpha),
        grid=(m // tm,),
        in_specs=[pl.BlockSpec((tm, da), lambda i: (i, 0)),
                  pl.BlockSpec((tm, db), lambda i: (i, 0)),
                  pl.BlockSpec((tm, d), lambda i: (i, 0)),
                  pl.BlockSpec((None, da, d), lambda i: (layer, 0, 0)),
                  pl.BlockSpec((None, db, d), lambda i: (layer, 1, 0)),
                  vec, vec],
        out_specs=pl.BlockSpec((tm, d), lambda i: (i, 0)),
        out_shape=jax.ShapeDtypeStruct((m, d), F32),
        compiler_params=_cparams(("parallel",)),
        name="outproj_ln",
    )(ha2, hb2, x2, w_out, w_out, ln_g, ln_b)


def _ffn_kernel(x_ref, wu_ref, wd_ref, g_ref, b_ref, y_ref, *rest, alpha):
    yb_ref = rest[0] if len(rest) == 3 else None
    xb_ref, acc_ref = rest[-2:]
    j = pl.program_id(1)

    @pl.when(j == 0)
    def _():
        xb_ref[...] = x_ref[...].astype(BF16)
        acc_ref[...] = jnp.zeros_like(acc_ref)

    u = jnp.dot(xb_ref[...], wu_ref[...], preferred_element_type=F32)
    u = jnp.maximum(u, 0.0)
    acc_ref[...] += jnp.dot((u * u).astype(BF16), wd_ref[...], preferred_element_type=F32)

    @pl.when(j == pl.num_programs(1) - 1)
    def _():
        y = _layer_norm(alpha * x_ref[...] + acc_ref[...], g_ref[...], b_ref[...])
        y_ref[...] = y
        if yb_ref is not None:
            yb_ref[...] = y.astype(BF16)


def _ffn(x2, w_up, w_down, ln_g, ln_b, layer, alpha, tm_pref, tf_pref, with_bf16):
    m, d = x2.shape
    f = w_up.shape[1]
    tm = _tile(m, tm_pref)
    tf = _tile(f, tf_pref)
    vec = pl.BlockSpec((None, 1, d), lambda i, j: (layer, 0, 0))
    rows = pl.BlockSpec((tm, d), lambda i, j: (i, 0))
    n_out = 2 if with_bf16 else 1
    outs = pl.pallas_call(
        functools.partial(_ffn_kernel, alpha=alpha),
        grid=(m // tm, f // tf),
        in_specs=[rows,
                  pl.BlockSpec((d, tf), lambda i, j: (0, j)),
                  pl.BlockSpec((tf, d), lambda i, j: (j, 0)),
                  vec, vec],
        out_specs=[rows, rows][:n_out],
        out_shape=[jax.ShapeDtypeStruct((m, d), F32), jax.ShapeDtypeStruct((m, d), BF16)][:n_out],
        scratch_shapes=[pltpu.VMEM((tm, d), BF16), pltpu.VMEM((tm, d), F32)],
        compiler_params=_cparams(("parallel", "arbitrary")),
        name="ffn_ln",
    )(x2, w_up, w_down, ln_g, ln_b)
    return (outs[0], outs[1]) if with_bf16 else (outs[0], None)


def _layer(x, xb, layer, last, cache, kv_prev, mlp_w, wts, dims, tiles):
    h_a, dk, dv, h_b, dh = dims
    (w_main, w_gate, col_scale, b_gate, g_head, w_out, ln1_g, ln1_b, ln2_g, ln2_b, alpha) = wts
    b, l, d = x.shape
    m = b * l
    d_b = h_b * dh
    x2 = x.reshape(m, d)

    x_in, tm_in = (x2, tiles["tm_in"]) if xb is None else (xb, tiles["tm_in_bf16"])
    slab, k_all, v_all, gpre = _inproj(x_in, w_main, w_gate, col_scale, layer, d_b, kv_prev, tm_in)
    slab3 = slab.reshape(b, l, slab.shape[1])
    n_rows = 2 * h_a + h_b
    gc, gr, qb_all, kb_all = _gates(gpre.reshape(b, l, GATE_COLS), b_gate, layer, h_a, n_rows, h_b,
                                    _bias_placement(2 * h_a, h_b))
    q_col = (2 * h_a * dk + 2 * h_a * dv) // dh

    if cache is None:
        c0 = jnp.zeros((b, h_a, dv, dk), F32)
        n0 = jnp.zeros((b, h_a, dk), F32)
        m0 = jnp.zeros((b, h_a), F32)
        hb, mlp_w = _fox_prompt(slab3, qb_all, kb_all, h_b, dh, q_col, tiles["tq"], tiles["fox_heads"], mlp_w, layer)
    else:
        ck, cv, clf, c_all, n_all, m_all = cache
        c0, n0, m0 = c_all[layer], n_all[layer], m_all[layer]
        hb = _fox_sample(slab3, ck, cv, clf, layer, gc, h_b, dh, q_col, 2 * h_a)
    ha, c, n, mm = _mlstm(slab3, gc, gr, g_head, layer, c0, n0, m0, h_a, dk, dv, tiles["lc"])

    x1 = _outproj(ha.reshape(m, -1), hb.reshape(m, -1), x2, w_out, ln1_g, ln1_b, layer, alpha, tiles["tm_out"])
    y, yb = _ffn(x1, mlp_w[0], mlp_w[1], ln2_g, ln2_b, layer, alpha, tiles["tm_ffn"], tiles["tf_ffn"], not last)
    lfb = jnp.transpose(gr[:, n_rows:, :], (0, 2, 1))
    return y.reshape(b, l, d), yb, (k_all, v_all), (lfb, c, n, mm), mlp_w


def kernel(x_prompt, x_sample, cache_fox_k, cache_fox_v, cache_fox_logf, state_mlstm_c, state_mlstm_n,
           state_mlstm_m, w_in, b_gates, g_mlstm, w_out, ln1_g, ln1_b, w_up, w_down, ln2_g, ln2_b):
    depth, d_model, _ = w_in.shape
    _, _, _, h_b, dh = cache_fox_k.shape
    _, _, h_a, dv, dk = state_mlstm_c.shape
    d_a = h_a * dv
    d_b = h_b * dh
    dims = (h_a, dk, dv, h_b, dh)
    alpha = (2 * depth) ** 0.25
    widths = (h_a * dk, h_a * dk, d_a, d_a, h_a, h_a, d_b, d_b, d_b, h_b)
    offs = [0]
    for w in widths:
        offs.append(offs[-1] + w)
    n_gates = 2 * h_a + h_b

    tiles = dict(tm_in=512, tm_in_bf16=1024, tn_in=1024, tq=1024, fox_heads=2, lc=256, tm_out=512, tm_ffn=512, tf_ffn=1024)

    w_main, w_gate = _win_prep(w_in, ((offs[0], offs[4]), (offs[6], offs[9])), ((offs[4], offs[6]), (offs[9], offs[10])),
                               tiles["tn_in"])
    b_gate = jnp.pad(b_gates, ((0, 0), (0, GATE_COLS - n_gates))).reshape(depth, 1, GATE_COLS)
    n_main = w_main.shape[1] * w_main.shape[3]
    col_scale = jnp.concatenate([jnp.full((h_a * dk,), dk ** -0.5, F32),
                                 jnp.ones((offs[4] - offs[1],), F32),
                                 jnp.full((d_b,), dh ** -0.5 * LOG2E, F32),
                                 jnp.ones((2 * d_b,), F32)]).reshape(1, n_main)
    wts = (w_main, w_gate, col_scale, b_gate, g_mlstm.reshape(depth, 1, d_a), w_out.astype(BF16),
           ln1_g.reshape(depth, 1, d_model), ln1_b.reshape(depth, 1, d_model),
           ln2_g.reshape(depth, 1, d_model), ln2_b.reshape(depth, 1, d_model), alpha)
    cache = (cache_fox_k, cache_fox_v, cache_fox_logf, state_mlstm_c, state_mlstm_n, state_mlstm_m)


    yp, ys = x_prompt, x_sample
    ypb, ysb = x_prompt.reshape(-1, d_model).astype(BF16), x_sample.reshape(-1, d_model).astype(BF16)
    kv_p = kv_s = None
    small_p, small_s = [], []
    for layer in range(depth):
        last = layer == depth - 1
        yp, ypb, kv_p, st_p, mlp_bf16 = _layer(yp, ypb, layer, last, None, kv_p, (w_up, w_down), wts, dims, tiles)
        ys, ysb, kv_s, st_s, _ = _layer(ys, ysb, layer, last, cache, kv_s, mlp_bf16, wts, dims, tiles)
        small_p.append(st_p)
        small_s.append(st_s)
    stack = lambda states, i: jnp.stack([s[i] for s in states], axis=0)
    bp, lp, _ = x_prompt.shape
    bs, ls, _ = x_sample.shape
    return ((yp, ys)
            + tuple(a.reshape(depth, bp, lp, h_b, dh) for a in kv_p) + tuple(stack(small_p, i) for i in range(4))
            + tuple(a.reshape(depth, bs, ls, h_b, dh) for a in kv_s) + tuple(stack(small_s, i) for i in range(4)))
```

```python
import functools
import math

import jax
import jax.numpy as jnp
from jax import lax
from jax.experimental import pallas as pl
from jax.experimental.pallas import tpu as pltpu

F32 = jnp.float32
BF16 = jnp.bfloat16

LN_EPS = 1e-5
HEAD_NORM_EPS = 1e-6
LOG2E = math.log2(math.e)
LANES = 128
GATE_COLS = LANES
VMEM_LIMIT = 56 * 1024 * 1024


def _cparams(sem):
    return pltpu.CompilerParams(dimension_semantics=sem, vmem_limit_bytes=VMEM_LIMIT)


def _tile(n, pref):
    t = min(n, pref)
    assert n % t == 0, (n, pref)
    return t


def _log_sigmoid(g):
    return jnp.minimum(g, 0.0) - jnp.log1p(jnp.exp(-jnp.abs(g)))


def _dot_nt(a, b):
    return lax.dot_general(a, b, (((1,), (1,)), ((), ())), preferred_element_type=F32)


def _dot_tn(a, b):
    return lax.dot_general(a, b, (((0,), (0,)), ((), ())), preferred_element_type=F32)


def _layer_norm(z, g, b):
    mu = jnp.mean(z, axis=-1, keepdims=True)
    zc = z - mu
    var = jnp.mean(zc * zc, axis=-1, keepdims=True)
    return zc * lax.rsqrt(var + LN_EPS) * g + b


def _prefix_sum_rows(v):
    row = lax.broadcasted_iota(jnp.int32, v.shape, 0)
    s = 1
    while s < v.shape[0]:
        v = v + jnp.where(row >= s, pltpu.roll(v, s, 0), 0.0)
        s *= 2
    return v


def _split3(f):
    hi = f.astype(BF16).astype(F32)
    r = f - hi
    mid = r.astype(BF16).astype(F32)
    lo = (r - mid).astype(BF16).astype(F32)
    return hi, mid, lo


def _win_prep_kernel(wt_ref, main_ref, gate_ref, *, main_cols, gate_cols):
    nj, tk, tn = main_ref.shape
    starts = [a + off for a, b in main_cols for off in range(0, b - a, tn)]
    for j in range(nj):
        main_ref[j] = wt_ref[starts[j]:starts[j] + tn, :].T.astype(BF16)
    gates = [wt_ref[a:b, :] for a, b in gate_cols]
    pad = GATE_COLS - sum(b - a for a, b in gate_cols)
    gate_ref[...] = jnp.concatenate(gates + [jnp.zeros((pad, tk), F32)], axis=0).T.astype(BF16)


def _win_prep(w_in, main_cols, gate_cols, tn):
    depth, d, n_in = w_in.shape
    assert all((b - a) % tn == 0 and a % 8 == 0 for a, b in main_cols)
    assert all((b - a) % 8 == 0 and a % 8 == 0 for a, b in gate_cols)
    n_main = sum(b - a for a, b in main_cols)
    nj = n_main // tn
    tk = _tile(d, 256)
    return pl.pallas_call(
        functools.partial(_win_prep_kernel, main_cols=main_cols, gate_cols=gate_cols),
        grid=(depth, d // tk),
        in_specs=[pl.BlockSpec((None, n_in, tk), lambda l, t: (l, 0, t))],
        out_specs=[pl.BlockSpec((None, nj, tk, tn), lambda l, t: (l, 0, t, 0)),
                   pl.BlockSpec((None, tk, GATE_COLS), lambda l, t: (l, t, 0))],
        out_shape=[jax.ShapeDtypeStruct((depth, nj, d, tn), BF16),
                   jax.ShapeDtypeStruct((depth, d, GATE_COLS), BF16)],
        compiler_params=_cparams(("parallel", "parallel")),
        name="win_prep",
    )(jnp.swapaxes(w_in, 1, 2))


def _inproj_kernel(*refs, jk0, jv0):
    x_ref, w_ref, wg_ref, cs_ref = refs[:4]
    j = pl.program_id(1)
    if x_ref.dtype == BF16:
        slab_ref, k_ref, v_ref, g_ref = refs[-4:]
        xb_ref = x_ref

        @pl.when(j == 0)
        def _():
            g_ref[...] = jnp.dot(x_ref[...], wg_ref[...], preferred_element_type=F32)
    else:
        slab_ref, k_ref, v_ref, g_ref, xb_ref = refs[-5:]

        @pl.when(j == 0)
        def _():
            xb = x_ref[...].astype(BF16)
            xb_ref[...] = xb
            g_ref[...] = jnp.dot(xb, wg_ref[...], preferred_element_type=F32)

    def tile(f32_ref):
        acc = jnp.dot(xb_ref[...], w_ref[...], preferred_element_type=F32)
        slab_ref[...] = (acc * cs_ref[...]).astype(BF16)
        if f32_ref is not None:
            f32_ref[...] = acc

    pl.when(j < jk0)(lambda: tile(None))
    pl.when((j >= jk0) & (j < jv0))(lambda: tile(k_ref))
    pl.when(j >= jv0)(lambda: tile(v_ref))


def _inproj(x2, w_main, w_gate, col_scale, layer, d_b, kv_buf, tm_pref):
    m, d = x2.shape
    depth, nj, _, tn = w_main.shape
    n = nj * tn
    tm = _tile(m, tm_pref)
    assert d_b % tn == 0
    nkv = d_b // tn
    jk0 = nj - 2 * nkv
    jv0 = nj - nkv
    return pl.pallas_call(
        functools.partial(_inproj_kernel, jk0=jk0, jv0=jv0),
        grid=(m // tm, nj),
        in_specs=[pl.BlockSpec((tm, d), lambda i, j: (i, 0)),
                  pl.BlockSpec((None, None, d, tn), lambda i, j: (layer, j, 0, 0)),
                  pl.BlockSpec((None, d, GATE_COLS), lambda i, j: (layer, 0, 0)),
                  pl.BlockSpec((1, tn), lambda i, j: (0, j)),
                  pl.BlockSpec(memory_space=pl.ANY), pl.BlockSpec(memory_space=pl.ANY)],
        out_specs=[pl.BlockSpec((tm, tn), lambda i, j: (i, j)),
                   pl.BlockSpec((None, tm, tn), lambda i, j: (layer, i, jnp.clip(j - jk0, 0, nkv - 1))),
                   pl.BlockSpec((None, tm, tn), lambda i, j: (layer, i, jnp.clip(j - jv0, 0, nkv - 1))),
                   pl.BlockSpec((tm, GATE_COLS), lambda i, j: (i, 0))],
        out_shape=[jax.ShapeDtypeStruct((m, n), BF16),
                   jax.ShapeDtypeStruct((depth, m, d_b), F32),
                   jax.ShapeDtypeStruct((depth, m, d_b), F32),
                   jax.ShapeDtypeStruct((m, GATE_COLS), F32)],
        scratch_shapes=[] if x2.dtype == BF16 else [pltpu.VMEM((tm, d), BF16)],
        input_output_aliases={4: 1, 5: 2},
        compiler_params=_cparams(("parallel", "arbitrary")),
        name="inproj",
    )(x2, w_main, w_gate, col_scale, *kv_buf)


BIAS_LANES = 6


def _bias_placement(lane0, n_heads):
    import numpy as np
    place = np.zeros((3 * LANES, 2 * LANES), np.float32)
    ones = np.zeros((2, LANES), np.float32)
    for h in range(n_heads):
        for t in range(3):
            place[t * LANES + lane0 + h, BIAS_LANES * h + t] = 1.0
            place[t * LANES + lane0 + h, LANES + BIAS_LANES * h + 3 + t] = -1.0
            ones[0, BIAS_LANES * h + 3 + t] = 1.0
            ones[1, BIAS_LANES * h + t] = 1.0
    return jnp.asarray(place, BF16), jnp.asarray(ones, F32)


def _gates_kernel(gp_ref, bg_ref, place_ref, ones_ref, gc_ref, gr_ref, qb_ref, kb_ref, carry_ref,
                  *, n_raw, n_rows, n_keep):
    t = pl.program_id(1)

    @pl.when(t == 0)
    def _():
        carry_ref[...] = jnp.zeros_like(carry_ref)

    g = gp_ref[...] + bg_ref[...]
    tl = g.shape[0]
    lane = lax.broadcasted_iota(jnp.int32, g.shape, 1)
    raw = lane < n_raw
    val = jnp.where(raw, g, _log_sigmoid(g))
    c = _prefix_sum_rows(val) + carry_ref[...]
    carry_ref[...] = c[tl - 1:tl, :]
    out = jnp.where(raw, g, c)
    gc_ref[...] = out
    gr_ref[...] = jnp.concatenate([out.T[:n_rows, :], val.T[n_rows - n_keep:n_rows, :]], axis=0)
    split = jnp.concatenate(_split3(c * LOG2E), axis=1).astype(BF16)
    moved = jnp.dot(split, place_ref[...], preferred_element_type=F32)
    qb_ref[...] = (moved[:, :LANES] + ones_ref[0:1, :]).astype(BF16)
    kb_ref[...] = (moved[:, LANES:] + ones_ref[1:2, :]).astype(BF16)


def _gates(gpre3, b_gate, layer, n_raw, n_rows, n_keep, placement):
    b, l, _ = gpre3.shape
    tl = _tile(l, 512)
    place, ones = placement
    const = lambda i, t: (0, 0)
    tile = pl.BlockSpec((None, tl, GATE_COLS), lambda i, t: (i, t, 0))
    return pl.pallas_call(
        functools.partial(_gates_kernel, n_raw=n_raw, n_rows=n_rows, n_keep=n_keep),
        grid=(b, l // tl),
        in_specs=[tile,
                  pl.BlockSpec((None, 1, GATE_COLS), lambda i, t: (layer, 0, 0)),
                  pl.BlockSpec(place.shape, const),
                  pl.BlockSpec(ones.shape, const)],
        out_specs=[tile, pl.BlockSpec((None, n_rows + n_keep, tl), lambda i, t: (i, 0, t)), tile, tile],
        out_shape=[jax.ShapeDtypeStruct((b, l, GATE_COLS), F32),
                   jax.ShapeDtypeStruct((b, n_rows + n_keep, l), F32),
                   jax.ShapeDtypeStruct((b, l, GATE_COLS), BF16),
                   jax.ShapeDtypeStruct((b, l, GATE_COLS), BF16)],
        scratch_shapes=[pltpu.VMEM((1, GATE_COLS), F32)],
        compiler_params=_cparams(("parallel", "arbitrary")),
        name="gates",
    )(gpre3, b_gate, place, ones)


def _mlstm_kernel(q_ref, k_ref, v_ref, o_ref, gc_ref, gr_ref, gh_ref, c0_ref, n0_ref, m0_ref,
                  h_ref, c_ref, n_ref, m_ref, cta_s, m_s, fp_s, *, n_heads, dk, dv):
    c_idx = pl.program_id(1)
    n_chunks = pl.num_programs(1)
    lc = q_ref.shape[0]

    @pl.when(c_idx == 0)
    def _():
        lane = lax.broadcasted_iota(jnp.int32, (dk, LANES), 1)
        for h in range(n_heads):
            cta_s[h, :, :dv] = c0_ref[h].T
            n_cols = jnp.broadcast_to(n0_ref[h:h + 1, :], (LANES, dk)).T
            cta_s[h, :, dv:] = jnp.where(lane == 0, n_cols, 0.0)
        m_s[...] = m0_ref[...]
        fp_s[...] = jnp.zeros_like(fp_s)

    gc = gc_ref[...]
    gr = gr_ref[...]
    fp = fp_s[...]
    m_prev = m_s[...]
    row1 = lax.broadcasted_iota(jnp.int32, gc.shape, 0)
    b = gc - fp
    ig = pltpu.roll(gc, n_heads, 1)
    run = ig - b
    step = 1
    while step < lc:
        run = jnp.maximum(run, jnp.where(row1 >= step, pltpu.roll(run, step, 0), -jnp.inf))
        step *= 2
    g = b + m_prev
    m_t = jnp.maximum(g, b + run)
    w_inter = jnp.exp(g - m_t)
    e_neg = jnp.exp(-m_t)
    m_new = m_t[lc - 1:lc, :]
    b_last = b[lc - 1:lc, :]
    a_all = jnp.exp(b_last - b + ig - m_new)
    decay = jnp.exp(b_last + m_prev - m_new)
    m_s[...] = m_new
    fp_s[...] = gc[lc - 1:lc, :]

    row = lax.broadcasted_iota(jnp.int32, (lc, lc), 0)
    col = lax.broadcasted_iota(jnp.int32, (lc, lc), 1)
    causal = col <= row
    lane = lax.broadcasted_iota(jnp.int32, (lc, LANES), 1)
    ones_col = jnp.where(lane == 0, 1.0, 0.0).astype(BF16)
    for h in range(n_heads):
        hl = n_heads + h
        q = q_ref[:, h * dk:(h + 1) * dk]
        k = k_ref[:, h * dk:(h + 1) * dk]
        va = jnp.concatenate([v_ref[:, h * dv:(h + 1) * dv], ones_col], axis=1)
        ig_row = gr[h:h + 1, :]
        b_row = gr[hl:hl + 1, :] - fp[:, hl:hl + 1]
        d = jnp.where(causal, b[:, hl:hl + 1] - b_row + ig_row, -jnp.inf)
        s = _dot_nt(q, k) * jnp.exp(d - m_t[:, hl:hl + 1])
        cta = cta_s[h]
        both = (w_inter[:, hl:hl + 1] * jnp.dot(q, cta.astype(BF16), preferred_element_type=F32)
                + jnp.dot(s.astype(BF16), va, preferred_element_type=F32))
        hh = both[:, :dv] / jnp.maximum(jnp.abs(both[:, dv:dv + 1]), e_neg[:, hl:hl + 1])

        ka = (k.astype(F32) * a_all[:, hl:hl + 1]).astype(BF16)
        cta_s[h] = decay[:, hl:hl + 1] * cta + _dot_tn(ka, va)

        hn = hh * lax.rsqrt(jnp.mean(hh * hh, axis=1, keepdims=True) + HEAD_NORM_EPS)
        og = o_ref[:, h * dv:(h + 1) * dv].astype(F32)
        hn = hn * gh_ref[:, h * dv:(h + 1) * dv] * jax.nn.sigmoid(og)
        h_ref[:, h * dv:(h + 1) * dv] = hn.astype(h_ref.dtype)

    @pl.when(c_idx == n_chunks - 1)
    def _():
        for h in range(n_heads):
            c_ref[h] = cta_s[h, :, :dv].T
            n_ref[h:h + 1, :] = cta_s[h, :, dv:].T[0:1, :]
        m_ref[...] = m_s[...]


def _mlstm(slab3, gc, gr, g_head, layer, c0, n0, m0, n_heads, dk, dv, lc_pref):
    b, l, _ = slab3.shape
    lc = _tile(l, lc_pref)
    dq = n_heads * dk
    da = n_heads * dv
    assert da == 2 * dq and dk == LANES
    m0r = jnp.pad(m0, ((0, 0), (n_heads, GATE_COLS - 2 * n_heads))).reshape(b, 1, GATE_COLS)
    bmap = lambda i, c: (i, 0, 0)
    outs = pl.pallas_call(
        functools.partial(_mlstm_kernel, n_heads=n_heads, dk=dk, dv=dv),
        grid=(b, l // lc),
        in_specs=[pl.BlockSpec((None, lc, dq), lambda i, c: (i, c, 0)),
                  pl.BlockSpec((None, lc, dq), lambda i, c: (i, c, 1)),
                  pl.BlockSpec((None, lc, da), lambda i, c: (i, c, 1)),
                  pl.BlockSpec((None, lc, da), lambda i, c: (i, c, 2)),
                  pl.BlockSpec((None, lc, GATE_COLS), lambda i, c: (i, c, 0)),
                  pl.BlockSpec((None, gr.shape[1], lc), lambda i, c: (i, 0, c)),
                  pl.BlockSpec((None, 1, da), lambda i, c: (layer, 0, 0)),
                  pl.BlockSpec((None, n_heads, dv, dk), lambda i, c: (i, 0, 0, 0)),
                  pl.BlockSpec((None, n_heads, dk), bmap),
                  pl.BlockSpec((None, 1, GATE_COLS), bmap)],
        out_specs=[pl.BlockSpec((None, lc, da), lambda i, c: (i, c, 0)),
                   pl.BlockSpec((None, n_heads, dv, dk), lambda i, c: (i, 0, 0, 0)),
                   pl.BlockSpec((None, n_heads, dk), bmap),
                   pl.BlockSpec((None, 1, GATE_COLS), bmap)],
        out_shape=[jax.ShapeDtypeStruct((b, l, da), BF16),
                   jax.ShapeDtypeStruct((b, n_heads, dv, dk), F32),
                   jax.ShapeDtypeStruct((b, n_heads, dk), F32),
                   jax.ShapeDtypeStruct((b, 1, GATE_COLS), F32)],
        scratch_shapes=[pltpu.VMEM((n_heads, dk, dv + LANES), F32),
                        pltpu.VMEM((1, GATE_COLS), F32),
                        pltpu.VMEM((1, GATE_COLS), F32)],
        compiler_params=_cparams(("parallel", "arbitrary")),
        name="mlstm",
    )(slab3, slab3, slab3, slab3, gc, gr, g_head, c0, n0, m0r)
    h_a, c, n, m = outs
    return h_a, c, n, m[:, 0, n_heads:2 * n_heads]


def _fox_prompt_kernel(q_ref, k_ref, v_ref, qb_ref, kb_ref, *rest, tq, hp, cast_weights):
    if cast_weights:
        wu_ref, wd_ref, o_ref, wub_ref, wdb_ref = rest[:5]
        wub_ref[...] = wu_ref[...].astype(BF16)
        wdb_ref[...] = wd_ref[...].astype(BF16)
        rest = rest[5:]
    else:
        o_ref, rest = rest[0], rest[1:]
    qa_s, ka_s, va_s, p_s, m_s, al_s, acc_s = rest
    hg = pl.program_id(1)
    i = pl.program_id(2)
    dh = q_ref.shape[1] // hp

    @pl.when(i == 0)
    def _():
        kb = kb_ref[...]
        lane = lax.broadcasted_iota(jnp.int32, kb.shape, 1)
        ones_lane = jnp.where(lane == 0, 1.0, 0.0).astype(BF16)
        for u in range(hp):
            first = BIAS_LANES * (hg * hp + u)
            ka_s[u, :, :dh] = k_ref[:, u * dh:(u + 1) * dh]
            ka_s[u, :, dh:] = jnp.where((lane >= first) & (lane < first + BIAS_LANES), kb, jnp.zeros_like(kb))
            va_s[u, :, :dh] = v_ref[:, u * dh:(u + 1) * dh]
            va_s[u, :, dh:] = ones_lane

    for u in range(hp):
        qa_s[u, :, :dh] = q_ref[:, u * dh:(u + 1) * dh]
        qa_s[u, :, dh:] = qb_ref[...]

    def logits(u, j):
        start = pl.multiple_of(j * tq, tq)
        return _dot_nt(qa_s[u], ka_s[u, pl.ds(start, tq), :])

    def softmax(u, s, diagonal):
        if diagonal:
            row = lax.broadcasted_iota(jnp.int32, s.shape, 0)
            col = lax.broadcasted_iota(jnp.int32, s.shape, 1)
            s = jnp.where(col <= row, s, -jnp.inf)
            m_new = jnp.broadcast_to(jnp.max(s, axis=1, keepdims=True), (tq, LANES))
            al_s[u] = jnp.zeros_like(m_new)
        else:
            m = m_s[u]
            m_new = jnp.maximum(m, jnp.max(s, axis=1, keepdims=True))
            al_s[u] = jnp.exp2(m - m_new)
        m_s[u] = m_new
        p_s[u] = jnp.exp2(s - jnp.concatenate([m_new] * (tq // LANES), axis=1)).astype(BF16)

    def values(u, j):
        start = pl.multiple_of(j * tq, tq)
        pv = jnp.dot(p_s[u], va_s[u, pl.ds(start, tq), :], preferred_element_type=F32)
        return jnp.concatenate([al_s[u]] * (2 * dh // LANES), axis=1) * acc_s[u] + pv

    for u in range(hp):
        acc_s[u] = jnp.zeros(acc_s.shape[1:], F32)
        softmax(u, logits(u, i), True)

    def trip(j, _):
        prev = jnp.where(j == 0, i, j - 1)
        s = [logits(u, j) for u in range(hp)]
        for u in range(hp):
            acc_s[u] = values(u, prev)
        for u in range(hp):
            softmax(u, s[u], False)
        return 0

    lax.fori_loop(0, i, trip, 0)
    last = jnp.where(i == 0, i, i - 1)
    for u in range(hp):
        acc = values(u, last)
        o_ref[:, u * dh:(u + 1) * dh] = (acc[:, :dh] / acc[:, dh:dh + 1]).astype(o_ref.dtype)


def _fox_prompt(slab3, qb_all, kb_all, n_heads, dh, q_col, tq_pref, hp, mlp_w=None, layer=0):
    b, l, _ = slab3.shape
    tq = _tile(l, tq_pref)
    assert n_heads % hp == 0 and q_col % hp == 0
    qb, ng, w = q_col // hp, n_heads // hp, hp * dh
    nq = l // tq
    n_steps = b * ng * nq
    in_specs = [pl.BlockSpec((None, tq, w), lambda bi, h, i: (bi, i, qb + h)),
                pl.BlockSpec((None, l, w), lambda bi, h, i: (bi, 0, qb + ng + h)),
                pl.BlockSpec((None, l, w), lambda bi, h, i: (bi, 0, qb + 2 * ng + h)),
                pl.BlockSpec((None, tq, GATE_COLS), lambda bi, h, i: (bi, i, 0)),
                pl.BlockSpec((None, l, GATE_COLS), lambda bi, h, i: (bi, 0, 0))]
    out_specs = [pl.BlockSpec((None, tq, w), lambda bi, h, i: (bi, i, h))]
    out_shape = [jax.ShapeDtypeStruct((b, l, n_heads * dh), BF16)]
    args = [slab3, slab3, slab3, qb_all, kb_all]
    cast_weights = False
    if mlp_w is not None:
        w_up, w_down = mlp_w
        _, d, f = w_up.shape
        cast_weights = d % (16 * n_steps) == 0 and f % (16 * n_steps) == 0
    if cast_weights:
        ru, rd = d // n_steps, f // n_steps
        step = lambda bi, h, i: (bi * ng + h) * nq + i
        in_specs += [pl.BlockSpec((None, ru, f), lambda bi, h, i: (layer, step(bi, h, i), 0)),
                     pl.BlockSpec((None, rd, d), lambda bi, h, i: (layer, step(bi, h, i), 0))]
        out_specs += [pl.BlockSpec((ru, f), lambda bi, h, i: (step(bi, h, i), 0)),
                      pl.BlockSpec((rd, d), lambda bi, h, i: (step(bi, h, i), 0))]
        out_shape += [jax.ShapeDtypeStruct((d, f), BF16), jax.ShapeDtypeStruct((f, d), BF16)]
        args += [w_up, w_down]
    outs = pl.pallas_call(
        functools.partial(_fox_prompt_kernel, tq=tq, hp=hp, cast_weights=cast_weights),
        grid=(b, ng, nq),
        in_specs=in_specs,
        out_specs=out_specs,
        out_shape=out_shape,
        scratch_shapes=[pltpu.VMEM((hp, tq, 2 * dh), BF16), pltpu.VMEM((hp, l, 2 * dh), BF16),
                        pltpu.VMEM((hp, l, 2 * dh), BF16), pltpu.VMEM((hp, tq, tq), BF16),
                        pltpu.VMEM((hp, tq, LANES), F32), pltpu.VMEM((hp, tq, LANES), F32),
                        pltpu.VMEM((hp, tq, 2 * dh), F32)],
        compiler_params=_cparams(("parallel", "parallel", "arbitrary")),
        name="fox_prompt",
    )(*args)
    if cast_weights:
        return outs[0], (outs[1], outs[2])
    if mlp_w is not None:
        return outs[0], (mlp_w[0][layer].astype(BF16), mlp_w[1][layer].astype(BF16))
    return outs[0], None


def _fox_sample_kernel(q_ref, k_ref, v_ref, ck_ref, cv_ref, clf_ref, hcol_ref, gc_ref, o_ref, *, n_heads, dh, lane0):
    l = q_ref.shape[0]
    ph = ck_ref.shape[0]
    rows = n_heads * l
    cum = clf_ref[...]
    lane = lax.broadcasted_iota(jnp.int32, cum.shape, 1)
    step = n_heads
    while step < ph:
        cum = cum + jnp.where(lane >= step, pltpu.roll(cum, step, 1), 0.0)
        step *= 2

    gc = gc_ref[...]
    frame = lax.broadcasted_iota(jnp.int32, (l, 1), 0).astype(F32)
    stack = lambda pieces: jnp.concatenate(pieces, axis=0)
    qa = stack([q_ref[:, h * dh:(h + 1) * dh] for h in range(n_heads)])
    ka = stack([k_ref[:, h * dh:(h + 1) * dh] for h in range(n_heads)])
    va = stack([v_ref[:, h * dh:(h + 1) * dh] for h in range(n_heads)])
    f_col = stack([gc[:, lane0 + h:lane0 + h + 1] + cum[:, ph - n_heads + h:ph - n_heads + h + 1]
                   for h in range(n_heads)]) * LOG2E
    head_col = stack([jnp.full((l, 1), float(h), F32) for h in range(n_heads)])
    frame_col = stack([frame] * n_heads)
    as_row = lambda c: jnp.broadcast_to(c, (rows, rows)).T[0:1, :]
    f_row, head_row, frame_row = as_row(f_col), as_row(head_col), as_row(frame_col)

    s_old = _dot_nt(qa, ck_ref[...].astype(BF16)) + f_col - cum * LOG2E
    s_old = jnp.where(hcol_ref[...] == head_col, s_old, -jnp.inf)
    s_new = _dot_nt(qa, ka) + f_col - f_row
    s_new = jnp.where(head_row == head_col, jnp.where(frame_row <= frame_col, s_new, -jnp.inf), -jnp.inf)
    m = jnp.maximum(jnp.max(s_old, axis=1, keepdims=True), jnp.max(s_new, axis=1, keepdims=True))
    p_old = jnp.exp2(s_old - m)
    p_new = jnp.exp2(s_new - m)
    den = jnp.sum(p_old, axis=1, keepdims=True) + jnp.sum(p_new, axis=1, keepdims=True)
    acc = (jnp.dot(p_old.astype(BF16), cv_ref[...].astype(BF16), preferred_element_type=F32)
           + jnp.dot(p_new.astype(BF16), va, preferred_element_type=F32))
    out = (acc / den).astype(o_ref.dtype)
    for h in range(n_heads):
        o_ref[:, h * dh:(h + 1) * dh] = out[h * l:(h + 1) * l, :]


def _fox_sample(slab3, ck, cv, clf, layer, gc, n_heads, dh, q_col, lane0):
    b, l, _ = slab3.shape
    depth, _, past = ck.shape[:3]
    ph = past * n_heads
    dd = n_heads * dh
    qb = q_col // n_heads
    hcol = jnp.tile(jnp.arange(n_heads, dtype=F32), past).reshape(1, ph)
    cache_rows = pl.BlockSpec((None, None, ph, dh), lambda bi: (layer, bi, 0, 0))
    return pl.pallas_call(
        functools.partial(_fox_sample_kernel, n_heads=n_heads, dh=dh, lane0=lane0),
        grid=(b,),
        in_specs=[pl.BlockSpec((None, l, dd), lambda bi: (bi, 0, qb)),
                  pl.BlockSpec((None, l, dd), lambda bi: (bi, 0, qb + 1)),
                  pl.BlockSpec((None, l, dd), lambda bi: (bi, 0, qb + 2)),
                  cache_rows, cache_rows,
                  pl.BlockSpec((None, None, 1, ph), lambda bi: (layer, bi, 0, 0)),
                  pl.BlockSpec((1, ph), lambda bi: (0, 0)),
                  pl.BlockSpec((None, l, GATE_COLS), lambda bi: (bi, 0, 0))],
        out_specs=pl.BlockSpec((None, l, dd), lambda bi: (bi, 0, 0)),
        out_shape=jax.ShapeDtypeStruct((b, l, dd), BF16),
        compiler_params=_cparams(("parallel",)),
        name="fox_sample",
    )(slab3, slab3, slab3, ck.reshape(depth, b, ph, dh), cv.reshape(depth, b, ph, dh),
      clf.reshape(depth, b, 1, ph), hcol, gc)


def _outproj_kernel(ha_ref, hb_ref, x_ref, wa_ref, wb_ref, g_ref, b_ref, y_ref, *, alpha):
    tm = x_ref.shape[0]
    tr = min(tm, 256)
    for r in range(tm // tr):
        rows = slice(r * tr, (r + 1) * tr)
        mix = (jnp.dot(ha_ref[rows, :], wa_ref[...], preferred_element_type=F32)
               + jnp.dot(hb_ref[rows, :], wb_ref[...], preferred_element_type=F32))
        y_ref[rows, :] = _layer_norm(alpha * x_ref[rows, :] + mix, g_ref[...], b_ref[...])


def _outproj(ha2, hb2, x2, w_out, ln_g, ln_b, layer, alpha, tm_pref):
    m, d = x2.shape
    da = ha2.shape[1]
    db = hb2.shape[1]
    assert da == db
    tm = _tile(m, tm_pref)
    vec = pl.BlockSpec((None, 1, d), lambda i: (layer, 0, 0))
    return pl.pallas_call(
        functools.partial(_outproj_kernel, alpha=alpha),
        grid=(m // tm,),
        in_specs=[pl.BlockSpec((tm, da), lambda i: (i, 0)),
                  pl.BlockSpec((tm, db), lambda i: (i, 0)),
                  pl.BlockSpec((tm, d), lambda i: (i, 0)),
                  pl.BlockSpec((None, da, d), lambda i: (layer, 0, 0)),
                  pl.BlockSpec((None, db, d), lambda i: (layer, 1, 0)),
                  vec, vec],
        out_specs=pl.BlockSpec((tm, d), lambda i: (i, 0)),
        out_shape=jax.ShapeDtypeStruct((m, d), F32),
        compiler_params=_cparams(("parallel",)),
        name="outproj_ln",
    )(ha2, hb2, x2, w_out, w_out, ln_g, ln_b)


def _ffn_kernel(x_ref, wu_ref, wd_ref, g_ref, b_ref, y_ref, *rest, alpha):
    yb_ref = rest[0] if len(rest) == 3 else None
    xb_ref, acc_ref = rest[-2:]
    j = pl.program_id(1)

    @pl.when(j == 0)
    def _():
        xb_ref[...] = x_ref[...].astype(BF16)
        acc_ref[...] = jnp.zeros_like(acc_ref)

    u = jnp.dot(xb_ref[...], wu_ref[...], preferred_element_type=F32)
    u = jnp.maximum(u, 0.0)
    acc_ref[...] += jnp.dot((u * u).astype(BF16), wd_ref[...], preferred_element_type=F32)

    @pl.when(j == pl.num_programs(1) - 1)
    def _():
        y = _layer_norm(alpha * x_ref[...] + acc_ref[...], g_ref[...], b_ref[...])
        y_ref[...] = y
        if yb_ref is not None:
            yb_ref[...] = y.astype(BF16)


def _ffn(x2, w_up, w_down, ln_g, ln_b, layer, alpha, tm_pref, tf_pref, with_bf16):
    m, d = x2.shape
    f = w_up.shape[1]
    tm = _tile(m, tm_pref)
    tf = _tile(f, tf_pref)
    vec = pl.BlockSpec((None, 1, d), lambda i, j: (layer, 0, 0))
    rows = pl.BlockSpec((tm, d), lambda i, j: (i, 0))
    n_out = 2 if with_bf16 else 1
    outs = pl.pallas_call(
        functools.partial(_ffn_kernel, alpha=alpha),
        grid=(m // tm, f // tf),
        in_specs=[rows,
                  pl.BlockSpec((d, tf), lambda i, j: (0, j)),
                  pl.BlockSpec((tf, d), lambda i, j: (j, 0)),
                  vec, vec],
        out_specs=[rows, rows][:n_out],
        out_shape=[jax.ShapeDtypeStruct((m, d), F32), jax.ShapeDtypeStruct((m, d), BF16)][:n_out],
        scratch_shapes=[pltpu.VMEM((tm, d), BF16), pltpu.VMEM((tm, d), F32)],
        compiler_params=_cparams(("parallel", "arbitrary")),
        name="ffn_ln",
    )(x2, w_up, w_down, ln_g, ln_b)
    return (outs[0], outs[1]) if with_bf16 else (outs[0], None)


def _layer(x, xb, layer, last, cache, kv_buf, mlp_w, wts, dims, tiles):
    h_a, dk, dv, h_b, dh = dims
    (w_main, w_gate, col_scale, b_gate, g_head, w_out, ln1_g, ln1_b, ln2_g, ln2_b, alpha) = wts
    b, l, d = x.shape
    m = b * l
    d_b = h_b * dh
    x2 = x.reshape(m, d)

    x_in, tm_in = (x2, tiles["tm_in"]) if xb is None else (xb, tiles["tm_in_bf16"])
    slab, k_all, v_all, gpre = _inproj(x_in, w_main, w_gate, col_scale, layer, d_b, kv_buf, tm_in)
    slab3 = slab.reshape(b, l, slab.shape[1])
    n_rows = 2 * h_a + h_b
    gc, gr, qb_all, kb_all = _gates(gpre.reshape(b, l, GATE_COLS), b_gate, layer, h_a, n_rows, h_b,
                                    _bias_placement(2 * h_a, h_b))
    q_col = (2 * h_a * dk + 2 * h_a * dv) // dh

    if cache is None:
        c0 = jnp.zeros((b, h_a, dv, dk), F32)
        n0 = jnp.zeros((b, h_a, dk), F32)
        m0 = jnp.zeros((b, h_a), F32)
        hb, mlp_w = _fox_prompt(slab3, qb_all, kb_all, h_b, dh, q_col, tiles["tq"], tiles["fox_heads"], mlp_w, layer)
    else:
        ck, cv, clf, c_all, n_all, m_all = cache
        c0, n0, m0 = c_all[layer], n_all[layer], m_all[layer]
        hb = _fox_sample(slab3, ck, cv, clf, layer, gc, h_b, dh, q_col, 2 * h_a)
    ha, c, n, mm = _mlstm(slab3, gc, gr, g_head, layer, c0, n0, m0, h_a, dk, dv, tiles["lc"])

    x1 = _outproj(ha.reshape(m, -1), hb.reshape(m, -1), x2, w_out, ln1_g, ln1_b, layer, alpha, tiles["tm_out"])
    y, yb = _ffn(x1, mlp_w[0], mlp_w[1], ln2_g, ln2_b, layer, alpha, tiles["tm_ffn"], tiles["tf_ffn"], not last)
    lfb = jnp.transpose(gr[:, n_rows:, :], (0, 2, 1))
    return y.reshape(b, l, d), yb, (k_all, v_all), (lfb, c, n, mm), mlp_w


def kernel(x_prompt, x_sample, cache_fox_k, cache_fox_v, cache_fox_logf, state_mlstm_c, state_mlstm_n,
           state_mlstm_m, w_in, b_gates, g_mlstm, w_out, ln1_g, ln1_b, w_up, w_down, ln2_g, ln2_b):
    depth, d_model, _ = w_in.shape
    _, _, _, h_b, dh = cache_fox_k.shape
    _, _, h_a, dv, dk = state_mlstm_c.shape
    d_a = h_a * dv
    d_b = h_b * dh
    dims = (h_a, dk, dv, h_b, dh)
    alpha = (2 * depth) ** 0.25
    widths = (h_a * dk, h_a * dk, d_a, d_a, h_a, h_a, d_b, d_b, d_b, h_b)
    offs = [0]
    for w in widths:
        offs.append(offs[-1] + w)
    n_gates = 2 * h_a + h_b

    tiles = dict(tm_in=512, tm_in_bf16=1024, tn_in=1024, tq=1024, fox_heads=2, lc=256, tm_out=512, tm_ffn=512, tf_ffn=1024)

    w_main, w_gate = _win_prep(w_in, ((offs[0], offs[4]), (offs[6], offs[9])), ((offs[4], offs[6]), (offs[9], offs[10])),
                               tiles["tn_in"])
    b_gate = jnp.pad(b_gates, ((0, 0), (0, GATE_COLS - n_gates))).reshape(depth, 1, GATE_COLS)
    n_main = w_main.shape[1] * w_main.shape[3]
    col_scale = jnp.concatenate([jnp.full((h_a * dk,), dk ** -0.5, F32),
                                 jnp.ones((offs[4] - offs[1],), F32),
                                 jnp.full((d_b,), dh ** -0.5 * LOG2E, F32),
                                 jnp.ones((2 * d_b,), F32)]).reshape(1, n_main)
    wts = (w_main, w_gate, col_scale, b_gate, g_mlstm.reshape(depth, 1, d_a), w_out.astype(BF16),
           ln1_g.reshape(depth, 1, d_model), ln1_b.reshape(depth, 1, d_model),
           ln2_g.reshape(depth, 1, d_model), ln2_b.reshape(depth, 1, d_model), alpha)
    cache = (cache_fox_k, cache_fox_v, cache_fox_logf, state_mlstm_c, state_mlstm_n, state_mlstm_m)


    yp, ys = x_prompt, x_sample
    ypb, ysb = x_prompt.reshape(-1, d_model).astype(BF16), x_sample.reshape(-1, d_model).astype(BF16)
    kv_p = tuple(jnp.zeros((depth, x_prompt.shape[0] * x_prompt.shape[1], d_b), F32) for _ in range(2))
    kv_s = tuple(jnp.zeros((depth, x_sample.shape[0] * x_sample.shape[1], d_b), F32) for _ in range(2))
    small_p, small_s = [], []
    for layer in range(depth):
        last = layer == depth - 1
        yp, ypb, kv_p, st_p, mlp_bf16 = _layer(yp, ypb, layer, last, None, kv_p, (w_up, w_down), wts, dims, tiles)
        ys, ysb, kv_s, st_s, _ = _layer(ys, ysb, layer, last, cache, kv_s, mlp_bf16, wts, dims, tiles)
        small_p.append(st_p)
        small_s.append(st_s)
    stack = lambda states, i: jnp.stack([s[i] for s in states], axis=0)
    bp, lp, _ = x_prompt.shape
    bs, ls, _ = x_sample.shape
    return ((yp, ys)
            + tuple(a.reshape(depth, bp, lp, h_b, dh) for a in kv_p) + tuple(stack(small_p, i) for i in range(4))
            + tuple(a.reshape(depth, bs, ls, h_b, dh) for a in kv_s) + tuple(stack(small_s, i) for i in range(4)))
```

```python
import functools
import math

import jax
import jax.numpy as jnp
from jax import lax
from jax.experimental import pallas as pl
from jax.experimental.pallas import tpu as pltpu

F32 = jnp.float32
BF16 = jnp.bfloat16

LN_EPS = 1e-5
HEAD_NORM_EPS = 1e-6
LOG2E = math.log2(math.e)
LANES = 128
GATE_COLS = LANES
VMEM_LIMIT = 60 * 1024 * 1024


def _cparams(sem):
    return pltpu.CompilerParams(dimension_semantics=sem, vmem_limit_bytes=VMEM_LIMIT)


def _tile(n, pref):
    t = min(n, pref)
    assert n % t == 0, (n, pref)
    return t


def _log_sigmoid(g):
    return jnp.minimum(g, 0.0) - jnp.log1p(jnp.exp(-jnp.abs(g)))


def _dot_nt(a, b):
    return lax.dot_general(a, b, (((1,), (1,)), ((), ())), preferred_element_type=F32)


def _dot_tn(a, b):
    return lax.dot_general(a, b, (((0,), (0,)), ((), ())), preferred_element_type=F32)


def _layer_norm(z, g, b):
    mu = jnp.mean(z, axis=-1, keepdims=True)
    zc = z - mu
    var = jnp.mean(zc * zc, axis=-1, keepdims=True)
    return zc * lax.rsqrt(var + LN_EPS) * g + b


def _prefix_sum_rows(v):
    row = lax.broadcasted_iota(jnp.int32, v.shape, 0)
    s = 1
    while s < v.shape[0]:
        v = v + jnp.where(row >= s, pltpu.roll(v, s, 0), 0.0)
        s *= 2
    return v


def _split3(f):
    hi = f.astype(BF16).astype(F32)
    r = f - hi
    mid = r.astype(BF16).astype(F32)
    lo = (r - mid).astype(BF16).astype(F32)
    return hi, mid, lo


def _win_prep_kernel(wt_ref, main_ref, gate_ref, *, main_cols, gate_cols):
    nj, tk, tn = main_ref.shape
    starts = [a + off for a, b in main_cols for off in range(0, b - a, tn)]
    for j in range(nj):
        main_ref[j] = wt_ref[starts[j]:starts[j] + tn, :].T.astype(BF16)
    gates = [wt_ref[a:b, :] for a, b in gate_cols]
    pad = GATE_COLS - sum(b - a for a, b in gate_cols)
    gate_ref[...] = jnp.concatenate(gates + [jnp.zeros((pad, tk), F32)], axis=0).T.astype(BF16)


def _win_prep(w_in, main_cols, gate_cols, tn):
    depth, d, n_in = w_in.shape
    assert all((b - a) % tn == 0 and a % 8 == 0 for a, b in main_cols)
    assert all((b - a) % 8 == 0 and a % 8 == 0 for a, b in gate_cols)
    n_main = sum(b - a for a, b in main_cols)
    nj = n_main // tn
    tk = _tile(d, 256)
    return pl.pallas_call(
        functools.partial(_win_prep_kernel, main_cols=main_cols, gate_cols=gate_cols),
        grid=(depth, d // tk),
        in_specs=[pl.BlockSpec((None, n_in, tk), lambda l, t: (l, 0, t))],
        out_specs=[pl.BlockSpec((None, nj, tk, tn), lambda l, t: (l, 0, t, 0)),
                   pl.BlockSpec((None, tk, GATE_COLS), lambda l, t: (l, t, 0))],
        out_shape=[jax.ShapeDtypeStruct((depth, nj, d, tn), BF16),
                   jax.ShapeDtypeStruct((depth, d, GATE_COLS), BF16)],
        compiler_params=_cparams(("parallel", "parallel")),
        name="win_prep",
    )(jnp.swapaxes(w_in, 1, 2))


def _inproj_kernel(*refs, jk0, jv0):
    x_ref, w_ref, wg_ref, cs_ref = refs[:4]
    j = pl.program_id(1)
    if x_ref.dtype == BF16:
        slab_ref, k_ref, v_ref, g_ref = refs[-4:]
        xb_ref = x_ref

        @pl.when(j == 0)
        def _():
            g_ref[...] = jnp.dot(x_ref[...], wg_ref[...], preferred_element_type=F32)
    else:
        slab_ref, k_ref, v_ref, g_ref, xb_ref = refs[-5:]

        @pl.when(j == 0)
        def _():
            xb = x_ref[...].astype(BF16)
            xb_ref[...] = xb
            g_ref[...] = jnp.dot(xb, wg_ref[...], preferred_element_type=F32)

    def tile(f32_ref):
        acc = jnp.dot(xb_ref[...], w_ref[...], preferred_element_type=F32)
        slab_ref[...] = (acc * cs_ref[...]).astype(BF16)
        if f32_ref is not None:
            f32_ref[...] = acc

    pl.when(j < jk0)(lambda: tile(None))
    pl.when((j >= jk0) & (j < jv0))(lambda: tile(k_ref))
    pl.when(j >= jv0)(lambda: tile(v_ref))


def _inproj(x2, w_main, w_gate, col_scale, layer, d_b, kv_buf, tm_pref):
    m, d = x2.shape
    depth, nj, _, tn = w_main.shape
    n = nj * tn
    tm = _tile(m, tm_pref)
    assert d_b % tn == 0
    nkv = d_b // tn
    jk0 = nj - 2 * nkv
    jv0 = nj - nkv
    return pl.pallas_call(
        functools.partial(_inproj_kernel, jk0=jk0, jv0=jv0),
        grid=(m // tm, nj),
        in_specs=[pl.BlockSpec((tm, d), lambda i, j: (i, 0)),
                  pl.BlockSpec((None, None, d, tn), lambda i, j: (layer, j, 0, 0)),
                  pl.BlockSpec((None, d, GATE_COLS), lambda i, j: (layer, 0, 0)),
                  pl.BlockSpec((1, tn), lambda i, j: (0, j)),
                  pl.BlockSpec(memory_space=pl.ANY), pl.BlockSpec(memory_space=pl.ANY)],
        out_specs=[pl.BlockSpec((tm, tn), lambda i, j: (i, j)),
                   pl.BlockSpec((None, tm, tn), lambda i, j: (layer, i, jnp.clip(j - jk0, 0, nkv - 1))),
                   pl.BlockSpec((None, tm, tn), lambda i, j: (layer, i, jnp.clip(j - jv0, 0, nkv - 1))),
                   pl.BlockSpec((tm, GATE_COLS), lambda i, j: (i, 0))],
        out_shape=[jax.ShapeDtypeStruct((m, n), BF16),
                   jax.ShapeDtypeStruct((depth, m, d_b), F32),
                   jax.ShapeDtypeStruct((depth, m, d_b), F32),
                   jax.ShapeDtypeStruct((m, GATE_COLS), F32)],
        scratch_shapes=[] if x2.dtype == BF16 else [pltpu.VMEM((tm, d), BF16)],
        input_output_aliases={4: 1, 5: 2},
        compiler_params=_cparams(("parallel", "arbitrary")),
        name="inproj",
    )(x2, w_main, w_gate, col_scale, *kv_buf)


BIAS_LANES = 6


def _bias_placement(lane0, n_heads):
    import numpy as np
    place = np.zeros((3 * LANES, 2 * LANES), np.float32)
    ones = np.zeros((2, LANES), np.float32)
    for h in range(n_heads):
        for t in range(3):
            place[t * LANES + lane0 + h, BIAS_LANES * h + t] = 1.0
            place[t * LANES + lane0 + h, LANES + BIAS_LANES * h + 3 + t] = -1.0
            ones[0, BIAS_LANES * h + 3 + t] = 1.0
            ones[1, BIAS_LANES * h + t] = 1.0
    return jnp.asarray(place, BF16), jnp.asarray(ones, F32)


def _gates_kernel(gp_ref, bg_ref, place_ref, ones_ref, gc_ref, gr_ref, qb_ref, kb_ref, carry_ref,
                  *, n_raw, n_rows, n_keep):
    t = pl.program_id(1)

    @pl.when(t == 0)
    def _():
        carry_ref[...] = jnp.zeros_like(carry_ref)

    g = gp_ref[...] + bg_ref[...]
    tl = g.shape[0]
    lane = lax.broadcasted_iota(jnp.int32, g.shape, 1)
    raw = lane < n_raw
    val = jnp.where(raw, g, _log_sigmoid(g))
    c = _prefix_sum_rows(val) + carry_ref[...]
    carry_ref[...] = c[tl - 1:tl, :]
    out = jnp.where(raw, g, c)
    gc_ref[...] = out
    gr_ref[...] = jnp.concatenate([out.T[:n_rows, :], val.T[n_rows - n_keep:n_rows, :]], axis=0)
    split = jnp.concatenate(_split3(c * LOG2E), axis=1).astype(BF16)
    moved = jnp.dot(split, place_ref[...], preferred_element_type=F32)
    qb_ref[...] = (moved[:, :LANES] + ones_ref[0:1, :]).astype(BF16)
    kb_ref[...] = (moved[:, LANES:] + ones_ref[1:2, :]).astype(BF16)


def _gates(gpre3, b_gate, layer, n_raw, n_rows, n_keep, placement):
    b, l, _ = gpre3.shape
    tl = _tile(l, 512)
    place, ones = placement
    const = lambda i, t: (0, 0)
    tile = pl.BlockSpec((None, tl, GATE_COLS), lambda i, t: (i, t, 0))
    return pl.pallas_call(
        functools.partial(_gates_kernel, n_raw=n_raw, n_rows=n_rows, n_keep=n_keep),
        grid=(b, l // tl),
        in_specs=[tile,
                  pl.BlockSpec((None, 1, GATE_COLS), lambda i, t: (layer, 0, 0)),
                  pl.BlockSpec(place.shape, const),
                  pl.BlockSpec(ones.shape, const)],
        out_specs=[tile, pl.BlockSpec((None, n_rows + n_keep, tl), lambda i, t: (i, 0, t)), tile, tile],
        out_shape=[jax.ShapeDtypeStruct((b, l, GATE_COLS), F32),
                   jax.ShapeDtypeStruct((b, n_rows + n_keep, l), F32),
                   jax.ShapeDtypeStruct((b, l, GATE_COLS), BF16),
                   jax.ShapeDtypeStruct((b, l, GATE_COLS), BF16)],
        scratch_shapes=[pltpu.VMEM((1, GATE_COLS), F32)],
        compiler_params=_cparams(("parallel", "arbitrary")),
        name="gates",
    )(gpre3, b_gate, place, ones)


def _mlstm_kernel(q_ref, k_ref, v_ref, o_ref, gc_ref, gr_ref, gh_ref, c0_ref, n0_ref, m0_ref,
                  h_ref, c_ref, n_ref, m_ref, cta_s, m_s, fp_s, *, n_heads, dk, dv):
    c_idx = pl.program_id(1)
    n_chunks = pl.num_programs(1)
    lc = q_ref.shape[0]

    @pl.when(c_idx == 0)
    def _():
        lane = lax.broadcasted_iota(jnp.int32, (dk, LANES), 1)
        for h in range(n_heads):
            cta_s[h, :, :dv] = c0_ref[h].T
            n_cols = jnp.broadcast_to(n0_ref[h:h + 1, :], (LANES, dk)).T
            cta_s[h, :, dv:] = jnp.where(lane == 0, n_cols, 0.0)
        m_s[...] = m0_ref[...]
        fp_s[...] = jnp.zeros_like(fp_s)

    gc = gc_ref[...]
    gr = gr_ref[...]
    fp = fp_s[...]
    m_prev = m_s[...]
    row1 = lax.broadcasted_iota(jnp.int32, gc.shape, 0)
    b = gc - fp
    ig = pltpu.roll(gc, n_heads, 1)
    run = ig - b
    step = 1
    while step < lc:
        run = jnp.maximum(run, jnp.where(row1 >= step, pltpu.roll(run, step, 0), -jnp.inf))
        step *= 2
    g = b + m_prev
    m_t = jnp.maximum(g, b + run)
    w_inter = jnp.exp(g - m_t)
    e_neg = jnp.exp(-m_t)
    m_new = m_t[lc - 1:lc, :]
    b_last = b[lc - 1:lc, :]
    a_all = jnp.exp(b_last - b + ig - m_new)
    decay = jnp.exp(b_last + m_prev - m_new)
    m_s[...] = m_new
    fp_s[...] = gc[lc - 1:lc, :]

    row = lax.broadcasted_iota(jnp.int32, (lc, lc), 0)
    col = lax.broadcasted_iota(jnp.int32, (lc, lc), 1)
    causal = col <= row
    lane = lax.broadcasted_iota(jnp.int32, (lc, LANES), 1)
    ones_col = jnp.where(lane == 0, 1.0, 0.0).astype(BF16)
    for h in range(n_heads):
        hl = n_heads + h
        q = q_ref[:, h * dk:(h + 1) * dk]
        k = k_ref[:, h * dk:(h + 1) * dk]
        va = jnp.concatenate([v_ref[:, h * dv:(h + 1) * dv], ones_col], axis=1)
        ig_row = gr[h:h + 1, :]
        b_row = gr[hl:hl + 1, :] - fp[:, hl:hl + 1]
        d = jnp.where(causal, b[:, hl:hl + 1] - b_row + ig_row, -jnp.inf)
        s = _dot_nt(q, k) * jnp.exp(d - m_t[:, hl:hl + 1])
        cta = cta_s[h]
        both = (w_inter[:, hl:hl + 1] * jnp.dot(q, cta.astype(BF16), preferred_element_type=F32)
                + jnp.dot(s.astype(BF16), va, preferred_element_type=F32))
        hh = both[:, :dv] / jnp.maximum(jnp.abs(both[:, dv:dv + 1]), e_neg[:, hl:hl + 1])

        ka = (k.astype(F32) * a_all[:, hl:hl + 1]).astype(BF16)
        cta_s[h] = decay[:, hl:hl + 1] * cta + _dot_tn(ka, va)

        hn = hh * lax.rsqrt(jnp.mean(hh * hh, axis=1, keepdims=True) + HEAD_NORM_EPS)
        og = o_ref[:, h * dv:(h + 1) * dv].astype(F32)
        hn = hn * gh_ref[:, h * dv:(h + 1) * dv] * jax.nn.sigmoid(og)
        h_ref[:, h * dv:(h + 1) * dv] = hn.astype(h_ref.dtype)

    @pl.when(c_idx == n_chunks - 1)
    def _():
        for h in range(n_heads):
            c_ref[h] = cta_s[h, :, :dv].T
            n_ref[h:h + 1, :] = cta_s[h, :, dv:].T[0:1, :]
        m_ref[...] = m_s[...]


def _mlstm(slab3, gc, gr, g_head, layer, c0, n0, m0, n_heads, dk, dv, lc_pref):
    b, l, _ = slab3.shape
    lc = _tile(l, lc_pref)
    dq = n_heads * dk
    da = n_heads * dv
    assert da == 2 * dq and dk == LANES
    m0r = jnp.pad(m0, ((0, 0), (n_heads, GATE_COLS - 2 * n_heads))).reshape(b, 1, GATE_COLS)
    bmap = lambda i, c: (i, 0, 0)
    outs = pl.pallas_call(
        functools.partial(_mlstm_kernel, n_heads=n_heads, dk=dk, dv=dv),
        grid=(b, l // lc),
        in_specs=[pl.BlockSpec((None, lc, dq), lambda i, c: (i, c, 0)),
                  pl.BlockSpec((None, lc, dq), lambda i, c: (i, c, 1)),
                  pl.BlockSpec((None, lc, da), lambda i, c: (i, c, 1)),
                  pl.BlockSpec((None, lc, da), lambda i, c: (i, c, 2)),
                  pl.BlockSpec((None, lc, GATE_COLS), lambda i, c: (i, c, 0)),
                  pl.BlockSpec((None, gr.shape[1], lc), lambda i, c: (i, 0, c)),
                  pl.BlockSpec((None, 1, da), lambda i, c: (layer, 0, 0)),
                  pl.BlockSpec((None, n_heads, dv, dk), lambda i, c: (i, 0, 0, 0)),
                  pl.BlockSpec((None, n_heads, dk), bmap),
                  pl.BlockSpec((None, 1, GATE_COLS), bmap)],
        out_specs=[pl.BlockSpec((None, lc, da), lambda i, c: (i, c, 0)),
                   pl.BlockSpec((None, n_heads, dv, dk), lambda i, c: (i, 0, 0, 0)),
                   pl.BlockSpec((None, n_heads, dk), bmap),
                   pl.BlockSpec((None, 1, GATE_COLS), bmap)],
        out_shape=[jax.ShapeDtypeStruct((b, l, da), BF16),
                   jax.ShapeDtypeStruct((b, n_heads, dv, dk), F32),
                   jax.ShapeDtypeStruct((b, n_heads, dk), F32),
                   jax.ShapeDtypeStruct((b, 1, GATE_COLS), F32)],
        scratch_shapes=[pltpu.VMEM((n_heads, dk, dv + LANES), F32),
                        pltpu.VMEM((1, GATE_COLS), F32),
                        pltpu.VMEM((1, GATE_COLS), F32)],
        compiler_params=_cparams(("parallel", "arbitrary")),
        name="mlstm",
    )(slab3, slab3, slab3, slab3, gc, gr, g_head, c0, n0, m0r)
    h_a, c, n, m = outs
    return h_a, c, n, m[:, 0, n_heads:2 * n_heads]


def _fox_prompt_kernel(q_ref, k_ref, v_ref, qb_ref, kb_ref, *rest, tq, hp, cast_weights):
    if cast_weights:
        wu_ref, wd_ref, o_ref, wub_ref, wdb_ref = rest[:5]
        wub_ref[...] = wu_ref[...].astype(BF16)
        wdb_ref[...] = wd_ref[...].astype(BF16)
        rest = rest[5:]
    else:
        o_ref, rest = rest[0], rest[1:]
    qa_s, ka_s, va_s, p_s, m_s, al_s, acc_s = rest
    hg = pl.program_id(1)
    i = pl.program_id(2)
    dh = q_ref.shape[1] // hp

    @pl.when(i == 0)
    def _():
        kb = kb_ref[...]
        lane = lax.broadcasted_iota(jnp.int32, kb.shape, 1)
        ones_lane = jnp.where(lane == 0, 1.0, 0.0).astype(BF16)
        for u in range(hp):
            first = BIAS_LANES * (hg * hp + u)
            ka_s[u, :, :dh] = k_ref[:, u * dh:(u + 1) * dh]
            ka_s[u, :, dh:] = jnp.where((lane >= first) & (lane < first + BIAS_LANES), kb, jnp.zeros_like(kb))
            va_s[u, :, :dh] = v_ref[:, u * dh:(u + 1) * dh]
            va_s[u, :, dh:] = ones_lane

    for u in range(hp):
        qa_s[u, :, :dh] = q_ref[:, u * dh:(u + 1) * dh]
        qa_s[u, :, dh:] = qb_ref[...]

    def logits(u, j):
        start = pl.multiple_of(j * tq, tq)
        return _dot_nt(qa_s[u], ka_s[u, pl.ds(start, tq), :])

    def softmax(u, s, diagonal):
        if diagonal:
            row = lax.broadcasted_iota(jnp.int32, s.shape, 0)
            col = lax.broadcasted_iota(jnp.int32, s.shape, 1)
            s = jnp.where(col <= row, s, -jnp.inf)
            m_new = jnp.broadcast_to(jnp.max(s, axis=1, keepdims=True), (tq, LANES))
            al_s[u] = jnp.zeros_like(m_new)
        else:
            m = m_s[u]
            m_new = jnp.maximum(m, jnp.max(s, axis=1, keepdims=True))
            al_s[u] = jnp.exp2(m - m_new)
        m_s[u] = m_new
        p_s[u] = jnp.exp2(s - jnp.concatenate([m_new] * (tq // LANES), axis=1)).astype(BF16)

    def values(u, j):
        start = pl.multiple_of(j * tq, tq)
        pv = jnp.dot(p_s[u], va_s[u, pl.ds(start, tq), :], preferred_element_type=F32)
        return jnp.concatenate([al_s[u]] * (2 * dh // LANES), axis=1) * acc_s[u] + pv

    for u in range(hp):
        acc_s[u] = jnp.zeros(acc_s.shape[1:], F32)
        softmax(u, logits(u, i), True)

    def trip(j, _):
        prev = jnp.where(j == 0, i, j - 1)
        s = [logits(u, j) for u in range(hp)]
        for u in range(hp):
            acc_s[u] = values(u, prev)
        for u in range(hp):
            softmax(u, s[u], False)
        return 0

    lax.fori_loop(0, i, trip, 0)
    last = jnp.where(i == 0, i, i - 1)
    for u in range(hp):
        acc = values(u, last)
        o_ref[:, u * dh:(u + 1) * dh] = (acc[:, :dh] / acc[:, dh:dh + 1]).astype(o_ref.dtype)


def _fox_prompt(slab3, qb_all, kb_all, n_heads, dh, q_col, tq_pref, hp, mlp_w=None, layer=0):
    b, l, _ = slab3.shape
    tq = _tile(l, tq_pref)
    assert n_heads % hp == 0 and q_col % hp == 0
    qb, ng, w = q_col // hp, n_heads // hp, hp * dh
    nq = l // tq
    n_steps = b * ng * nq
    in_specs = [pl.BlockSpec((None, tq, w), lambda bi, h, i: (bi, i, qb + h)),
                pl.BlockSpec((None, l, w), lambda bi, h, i: (bi, 0, qb + ng + h)),
                pl.BlockSpec((None, l, w), lambda bi, h, i: (bi, 0, qb + 2 * ng + h)),
                pl.BlockSpec((None, tq, GATE_COLS), lambda bi, h, i: (bi, i, 0)),
                pl.BlockSpec((None, l, GATE_COLS), lambda bi, h, i: (bi, 0, 0))]
    out_specs = [pl.BlockSpec((None, tq, w), lambda bi, h, i: (bi, i, h))]
    out_shape = [jax.ShapeDtypeStruct((b, l, n_heads * dh), BF16)]
    args = [slab3, slab3, slab3, qb_all, kb_all]
    cast_weights = False
    if mlp_w is not None:
        w_up, w_down = mlp_w
        _, d, f = w_up.shape
        cast_weights = d % (16 * n_steps) == 0 and f % (16 * n_steps) == 0
    if cast_weights:
        ru, rd = d // n_steps, f // n_steps
        step = lambda bi, h, i: (bi * ng + h) * nq + i
        in_specs += [pl.BlockSpec((None, ru, f), lambda bi, h, i: (layer, step(bi, h, i), 0)),
                     pl.BlockSpec((None, rd, d), lambda bi, h, i: (layer, step(bi, h, i), 0))]
        out_specs += [pl.BlockSpec((ru, f), lambda bi, h, i: (step(bi, h, i), 0)),
                      pl.BlockSpec((rd, d), lambda bi, h, i: (step(bi, h, i), 0))]
        out_shape += [jax.ShapeDtypeStruct((d, f), BF16), jax.ShapeDtypeStruct((f, d), BF16)]
        args += [w_up, w_down]
    outs = pl.pallas_call(
        functools.partial(_fox_prompt_kernel, tq=tq, hp=hp, cast_weights=cast_weights),
        grid=(b, ng, nq),
        in_specs=in_specs,
        out_specs=out_specs,
        out_shape=out_shape,
        scratch_shapes=[pltpu.VMEM((hp, tq, 2 * dh), BF16), pltpu.VMEM((hp, l, 2 * dh), BF16),
                        pltpu.VMEM((hp, l, 2 * dh), BF16), pltpu.VMEM((hp, tq, tq), BF16),
                        pltpu.VMEM((hp, tq, LANES), F32), pltpu.VMEM((hp, tq, LANES), F32),
                        pltpu.VMEM((hp, tq, 2 * dh), F32)],
        compiler_params=_cparams(("parallel", "parallel", "arbitrary")),
        name="fox_prompt",
    )(*args)
    if cast_weights:
        return outs[0], (outs[1], outs[2])
    if mlp_w is not None:
        return outs[0], (mlp_w[0][layer].astype(BF16), mlp_w[1][layer].astype(BF16))
    return outs[0], None


def _fox_sample_kernel(q_ref, k_ref, v_ref, ck_ref, cv_ref, clf_ref, hcol_ref, gc_ref, o_ref, *, n_heads, dh, lane0):
    l = q_ref.shape[0]
    ph = ck_ref.shape[0]
    rows = n_heads * l
    cum = clf_ref[...]
    lane = lax.broadcasted_iota(jnp.int32, cum.shape, 1)
    step = n_heads
    while step < ph:
        cum = cum + jnp.where(lane >= step, pltpu.roll(cum, step, 1), 0.0)
        step *= 2

    gc = gc_ref[...]
    frame = lax.broadcasted_iota(jnp.int32, (l, 1), 0).astype(F32)
    stack = lambda pieces: jnp.concatenate(pieces, axis=0)
    qa = stack([q_ref[:, h * dh:(h + 1) * dh] for h in range(n_heads)])
    ka = stack([k_ref[:, h * dh:(h + 1) * dh] for h in range(n_heads)])
    va = stack([v_ref[:, h * dh:(h + 1) * dh] for h in range(n_heads)])
    f_col = stack([gc[:, lane0 + h:lane0 + h + 1] + cum[:, ph - n_heads + h:ph - n_heads + h + 1]
                   for h in range(n_heads)]) * LOG2E
    head_col = stack([jnp.full((l, 1), float(h), F32) for h in range(n_heads)])
    frame_col = stack([frame] * n_heads)
    as_row = lambda c: jnp.broadcast_to(c, (rows, rows)).T[0:1, :]
    f_row, head_row, frame_row = as_row(f_col), as_row(head_col), as_row(frame_col)

    s_old = _dot_nt(qa, ck_ref[...].astype(BF16)) + f_col - cum * LOG2E
    s_old = jnp.where(hcol_ref[...] == head_col, s_old, -jnp.inf)
    s_new = _dot_nt(qa, ka) + f_col - f_row
    s_new = jnp.where(head_row == head_col, jnp.where(frame_row <= frame_col, s_new, -jnp.inf), -jnp.inf)
    m = jnp.maximum(jnp.max(s_old, axis=1, keepdims=True), jnp.max(s_new, axis=1, keepdims=True))
    p_old = jnp.exp2(s_old - m)
    p_new = jnp.exp2(s_new - m)
    den = jnp.sum(p_old, axis=1, keepdims=True) + jnp.sum(p_new, axis=1, keepdims=True)
    acc = (jnp.dot(p_old.astype(BF16), cv_ref[...].astype(BF16), preferred_element_type=F32)
           + jnp.dot(p_new.astype(BF16), va, preferred_element_type=F32))
    out = (acc / den).astype(o_ref.dtype)
    for h in range(n_heads):
        o_ref[:, h * dh:(h + 1) * dh] = out[h * l:(h + 1) * l, :]


def _fox_sample(slab3, ck, cv, clf, layer, gc, n_heads, dh, q_col, lane0):
    b, l, _ = slab3.shape
    depth, _, past = ck.shape[:3]
    ph = past * n_heads
    dd = n_heads * dh
    qb = q_col // n_heads
    hcol = jnp.tile(jnp.arange(n_heads, dtype=F32), past).reshape(1, ph)
    cache_rows = pl.BlockSpec((None, None, ph, dh), lambda bi: (layer, bi, 0, 0))
    return pl.pallas_call(
        functools.partial(_fox_sample_kernel, n_heads=n_heads, dh=dh, lane0=lane0),
        grid=(b,),
        in_specs=[pl.BlockSpec((None, l, dd), lambda bi: (bi, 0, qb)),
                  pl.BlockSpec((None, l, dd), lambda bi: (bi, 0, qb + 1)),
                  pl.BlockSpec((None, l, dd), lambda bi: (bi, 0, qb + 2)),
                  cache_rows, cache_rows,
                  pl.BlockSpec((None, None, 1, ph), lambda bi: (layer, bi, 0, 0)),
                  pl.BlockSpec((1, ph), lambda bi: (0, 0)),
                  pl.BlockSpec((None, l, GATE_COLS), lambda bi: (bi, 0, 0))],
        out_specs=pl.BlockSpec((None, l, dd), lambda bi: (bi, 0, 0)),
        out_shape=jax.ShapeDtypeStruct((b, l, dd), BF16),
        compiler_params=_cparams(("parallel",)),
        name="fox_sample",
    )(slab3, slab3, slab3, ck.reshape(depth, b, ph, dh), cv.reshape(depth, b, ph, dh),
      clf.reshape(depth, b, 1, ph), hcol, gc)


def _outproj_kernel(ha_ref, hb_ref, x_ref, wa_ref, wb_ref, g_ref, b_ref, y_ref, *, alpha):
    tm = x_ref.shape[0]
    tr = min(tm, 256)
    for r in range(tm // tr):
        rows = slice(r * tr, (r + 1) * tr)
        mix = (jnp.dot(ha_ref[rows, :], wa_ref[...], preferred_element_type=F32)
               + jnp.dot(hb_ref[rows, :], wb_ref[...], preferred_element_type=F32))
        y_ref[rows, :] = _layer_norm(alpha * x_ref[rows, :] + mix, g_ref[...], b_ref[...])


def _outproj(ha2, hb2, x2, w_out, ln_g, ln_b, layer, alpha, tm_pref):
    m, d = x2.shape
    da = ha2.shape[1]
    db = hb2.shape[1]
    assert da == db
    tm = _tile(m, tm_pref)
    vec = pl.BlockSpec((None, 1, d), lambda i: (layer, 0, 0))
    return pl.pallas_call(
        functools.partial(_outproj_kernel, alpha=alpha),
        grid=(m // tm,),
        in_specs=[pl.BlockSpec((tm, da), lambda i: (i, 0)),
                  pl.BlockSpec((tm, db), lambda i: (i, 0)),
                  pl.BlockSpec((tm, d), lambda i: (i, 0)),
                  pl.BlockSpec((None, da, d), lambda i: (layer, 0, 0)),
                  pl.BlockSpec((None, db, d), lambda i: (layer, 1, 0)),
                  vec, vec],
        out_specs=pl.BlockSpec((tm, d), lambda i: (i, 0)),
        out_shape=jax.ShapeDtypeStruct((m, d), F32),
        compiler_params=_cparams(("parallel",)),
        name="outproj_ln",
    )(ha2, hb2, x2, w_out, w_out, ln_g, ln_b)


def _ffn_kernel(x_ref, wu_ref, wd_ref, g_ref, b_ref, y_ref, *rest, alpha):
    yb_ref = rest[0] if len(rest) == 3 else None
    xb_ref, acc_ref = rest[-2:]
    j = pl.program_id(1)

    @pl.when(j == 0)
    def _():
        xb_ref[...] = x_ref[...].astype(BF16)
        acc_ref[...] = jnp.zeros_like(acc_ref)

    u = jnp.dot(xb_ref[...], wu_ref[...], preferred_element_type=F32)
    u = jnp.maximum(u, 0.0)
    acc_ref[...] += jnp.dot((u * u).astype(BF16), wd_ref[...], preferred_element_type=F32)

    @pl.when(j == pl.num_programs(1) - 1)
    def _():
        y = _layer_norm(alpha * x_ref[...] + acc_ref[...], g_ref[...], b_ref[...])
        y_ref[...] = y
        if yb_ref is not None:
            yb_ref[...] = y.astype(BF16)


def _ffn(x2, w_up, w_down, ln_g, ln_b, layer, alpha, tm_pref, tf_pref, with_bf16):
    m, d = x2.shape
    f = w_up.shape[1]
    tm = _tile(m, tm_pref)
    tf = _tile(f, tf_pref)
    vec = pl.BlockSpec((None, 1, d), lambda i, j: (layer, 0, 0))
    rows = pl.BlockSpec((tm, d), lambda i, j: (i, 0))
    n_out = 2 if with_bf16 else 1
    outs = pl.pallas_call(
        functools.partial(_ffn_kernel, alpha=alpha),
        grid=(m // tm, f // tf),
        in_specs=[rows,
                  pl.BlockSpec((d, tf), lambda i, j: (0, j)),
                  pl.BlockSpec((tf, d), lambda i, j: (j, 0)),
                  vec, vec],
        out_specs=[pl.BlockSpec((tm, d), lambda i, j: (i, 0), pipeline_mode=pl.Buffered(1))] * n_out,
        out_shape=[jax.ShapeDtypeStruct((m, d), F32), jax.ShapeDtypeStruct((m, d), BF16)][:n_out],
        scratch_shapes=[pltpu.VMEM((tm, d), BF16), pltpu.VMEM((tm, d), F32)],
        compiler_params=_cparams(("parallel", "arbitrary")),
        name="ffn_ln",
    )(x2, w_up, w_down, ln_g, ln_b)
    return (outs[0], outs[1]) if with_bf16 else (outs[0], None)


def _layer(x, xb, layer, last, cache, kv_buf, mlp_w, wts, dims, tiles):
    h_a, dk, dv, h_b, dh = dims
    (w_main, w_gate, col_scale, b_gate, g_head, w_out, ln1_g, ln1_b, ln2_g, ln2_b, alpha) = wts
    b, l, d = x.shape
    m = b * l
    d_b = h_b * dh
    x2 = x.reshape(m, d)

    x_in, tm_in = (x2, tiles["tm_in"]) if xb is None else (xb, tiles["tm_in_bf16"])
    slab, k_all, v_all, gpre = _inproj(x_in, w_main, w_gate, col_scale, layer, d_b, kv_buf, tm_in)
    slab3 = slab.reshape(b, l, slab.shape[1])
    n_rows = 2 * h_a + h_b
    gc, gr, qb_all, kb_all = _gates(gpre.reshape(b, l, GATE_COLS), b_gate, layer, h_a, n_rows, h_b,
                                    _bias_placement(2 * h_a, h_b))
    q_col = (2 * h_a * dk + 2 * h_a * dv) // dh

    if cache is None:
        c0 = jnp.zeros((b, h_a, dv, dk), F32)
        n0 = jnp.zeros((b, h_a, dk), F32)
        m0 = jnp.zeros((b, h_a), F32)
        hb, mlp_w = _fox_prompt(slab3, qb_all, kb_all, h_b, dh, q_col, tiles["tq"], tiles["fox_heads"], mlp_w, layer)
    else:
        ck, cv, clf, c_all, n_all, m_all = cache
        c0, n0, m0 = c_all[layer], n_all[layer], m_all[layer]
        hb = _fox_sample(slab3, ck, cv, clf, layer, gc, h_b, dh, q_col, 2 * h_a)
    ha, c, n, mm = _mlstm(slab3, gc, gr, g_head, layer, c0, n0, m0, h_a, dk, dv, tiles["lc"])

    x1 = _outproj(ha.reshape(m, -1), hb.reshape(m, -1), x2, w_out, ln1_g, ln1_b, layer, alpha, tiles["tm_out"])
    y, yb = _ffn(x1, mlp_w[0], mlp_w[1], ln2_g, ln2_b, layer, alpha, tiles["tm_ffn"], tiles["tf_ffn"], not last)
    lfb = jnp.transpose(gr[:, n_rows:, :], (0, 2, 1))
    return y.reshape(b, l, d), yb, (k_all, v_all), (lfb, c, n, mm), mlp_w


def kernel(x_prompt, x_sample, cache_fox_k, cache_fox_v, cache_fox_logf, state_mlstm_c, state_mlstm_n,
           state_mlstm_m, w_in, b_gates, g_mlstm, w_out, ln1_g, ln1_b, w_up, w_down, ln2_g, ln2_b):
    depth, d_model, _ = w_in.shape
    _, _, _, h_b, dh = cache_fox_k.shape
    _, _, h_a, dv, dk = state_mlstm_c.shape
    d_a = h_a * dv
    d_b = h_b * dh
    dims = (h_a, dk, dv, h_b, dh)
    alpha = (2 * depth) ** 0.25
    widths = (h_a * dk, h_a * dk, d_a, d_a, h_a, h_a, d_b, d_b, d_b, h_b)
    offs = [0]
    for w in widths:
        offs.append(offs[-1] + w)
    n_gates = 2 * h_a + h_b

    tiles = dict(tm_in=512, tm_in_bf16=1024, tn_in=1024, tq=1024, fox_heads=2, lc=256, tm_out=512, tm_ffn=1024, tf_ffn=512)

    w_main, w_gate = _win_prep(w_in, ((offs[0], offs[4]), (offs[6], offs[9])), ((offs[4], offs[6]), (offs[9], offs[10])),
                               tiles["tn_in"])
    b_gate = jnp.pad(b_gates, ((0, 0), (0, GATE_COLS - n_gates))).reshape(depth, 1, GATE_COLS)
    n_main = w_main.shape[1] * w_main.shape[3]
    col_scale = jnp.concatenate([jnp.full((h_a * dk,), dk ** -0.5, F32),
                                 jnp.ones((offs[4] - offs[1],), F32),
                                 jnp.full((d_b,), dh ** -0.5 * LOG2E, F32),
                                 jnp.ones((2 * d_b,), F32)]).reshape(1, n_main)
    wts = (w_main, w_gate, col_scale, b_gate, g_mlstm.reshape(depth, 1, d_a), w_out.astype(BF16),
           ln1_g.reshape(depth, 1, d_model), ln1_b.reshape(depth, 1, d_model),
           ln2_g.reshape(depth, 1, d_model), ln2_b.reshape(depth, 1, d_model), alpha)
    cache = (cache_fox_k, cache_fox_v, cache_fox_logf, state_mlstm_c, state_mlstm_n, state_mlstm_m)


    yp, ys = x_prompt, x_sample
    ypb, ysb = x_prompt.reshape(-1, d_model).astype(BF16), x_sample.reshape(-1, d_model).astype(BF16)
    kv_p = tuple(jnp.zeros((depth, x_prompt.shape[0] * x_prompt.shape[1], d_b), F32) for _ in range(2))
    kv_s = tuple(jnp.zeros((depth, x_sample.shape[0] * x_sample.shape[1], d_b), F32) for _ in range(2))
    small_p, small_s = [], []
    for layer in range(depth):
        last = layer == depth - 1
        yp, ypb, kv_p, st_p, mlp_bf16 = _layer(yp, ypb, layer, last, None, kv_p, (w_up, w_down), wts, dims, tiles)
        ys, ysb, kv_s, st_s, _ = _layer(ys, ysb, layer, last, cache, kv_s, mlp_bf16, wts, dims, tiles)
        small_p.append(st_p)
        small_s.append(st_s)
    stack = lambda states, i: jnp.stack([s[i] for s in states], axis=0)
    bp, lp, _ = x_prompt.shape
    bs, ls, _ = x_sample.shape
    return ((yp, ys)
            + tuple(a.reshape(depth, bp, lp, h_b, dh) for a in kv_p) + tuple(stack(small_p, i) for i in range(4))
            + tuple(a.reshape(depth, bs, ls, h_b, dh) for a in kv_s) + tuple(stack(small_s, i) for i in range(4)))
```

```python
import functools
import math

import jax
import jax.numpy as jnp
from jax import lax
from jax.experimental import pallas as pl
from jax.experimental.pallas import tpu as pltpu

F32 = jnp.float32
BF16 = jnp.bfloat16

LN_EPS = 1e-5
HEAD_NORM_EPS = 1e-6
LOG2E = math.log2(math.e)
LANES = 128
GATE_COLS = LANES
VMEM_LIMIT = 56 * 1024 * 1024


def _cparams(sem):
    return pltpu.CompilerParams(dimension_semantics=sem, vmem_limit_bytes=VMEM_LIMIT)


def _tile(n, pref):
    t = min(n, pref)
    assert n % t == 0, (n, pref)
    return t


def _log_sigmoid(g):
    return jnp.minimum(g, 0.0) - jnp.log1p(jnp.exp(-jnp.abs(g)))


def _dot_nt(a, b):
    return lax.dot_general(a, b, (((1,), (1,)), ((), ())), preferred_element_type=F32)


def _dot_tn(a, b):
    return lax.dot_general(a, b, (((0,), (0,)), ((), ())), preferred_element_type=F32)


def _layer_norm(z, g, b):
    mu = jnp.mean(z, axis=-1, keepdims=True)
    zc = z - mu
    var = jnp.mean(zc * zc, axis=-1, keepdims=True)
    return zc * lax.rsqrt(var + LN_EPS) * g + b


def _prefix_sum_rows(v):
    row = lax.broadcasted_iota(jnp.int32, v.shape, 0)
    s = 1
    while s < v.shape[0]:
        v = v + jnp.where(row >= s, pltpu.roll(v, s, 0), 0.0)
        s *= 2
    return v


def _split3(f):
    hi = f.astype(BF16).astype(F32)
    r = f - hi
    mid = r.astype(BF16).astype(F32)
    lo = (r - mid).astype(BF16).astype(F32)
    return hi, mid, lo


def _win_prep_kernel(wt_ref, main_ref, gate_ref, *, main_cols, gate_cols):
    nj, tk, tn = main_ref.shape
    starts = [a + off for a, b in main_cols for off in range(0, b - a, tn)]
    for j in range(nj):
        main_ref[j] = wt_ref[starts[j]:starts[j] + tn, :].T.astype(BF16)
    gates = [wt_ref[a:b, :] for a, b in gate_cols]
    pad = GATE_COLS - sum(b - a for a, b in gate_cols)
    gate_ref[...] = jnp.concatenate(gates + [jnp.zeros((pad, tk), F32)], axis=0).T.astype(BF16)


def _win_prep(w_in, main_cols, gate_cols, tn):
    depth, d, n_in = w_in.shape
    assert all((b - a) % tn == 0 and a % 8 == 0 for a, b in main_cols)
    assert all((b - a) % 8 == 0 and a % 8 == 0 for a, b in gate_cols)
    n_main = sum(b - a for a, b in main_cols)
    nj = n_main // tn
    tk = _tile(d, 256)
    return pl.pallas_call(
        functools.partial(_win_prep_kernel, main_cols=main_cols, gate_cols=gate_cols),
        grid=(depth, d // tk),
        in_specs=[pl.BlockSpec((None, n_in, tk), lambda l, t: (l, 0, t))],
        out_specs=[pl.BlockSpec((None, nj, tk, tn), lambda l, t: (l, 0, t, 0)),
                   pl.BlockSpec((None, tk, GATE_COLS), lambda l, t: (l, t, 0))],
        out_shape=[jax.ShapeDtypeStruct((depth, nj, d, tn), BF16),
                   jax.ShapeDtypeStruct((depth, d, GATE_COLS), BF16)],
        compiler_params=_cparams(("parallel", "parallel")),
        name="win_prep",
    )(jnp.swapaxes(w_in, 1, 2))


def _inproj_kernel(*refs, jk0, jv0):
    x_ref, w_ref, wg_ref, cs_ref = refs[:4]
    j = pl.program_id(1)
    if x_ref.dtype == BF16:
        slab_ref, k_ref, v_ref, g_ref = refs[-4:]
        xb_ref = x_ref

        @pl.when(j == 0)
        def _():
            g_ref[...] = jnp.dot(x_ref[...], wg_ref[...], preferred_element_type=F32)
    else:
        slab_ref, k_ref, v_ref, g_ref, xb_ref = refs[-5:]

        @pl.when(j == 0)
        def _():
            xb = x_ref[...].astype(BF16)
            xb_ref[...] = xb
            g_ref[...] = jnp.dot(xb, wg_ref[...], preferred_element_type=F32)

    def tile(f32_ref):
        acc = jnp.dot(xb_ref[...], w_ref[...], preferred_element_type=F32)
        slab_ref[...] = (acc * cs_ref[...]).astype(BF16)
        if f32_ref is not None:
            f32_ref[...] = acc

    pl.when(j < jk0)(lambda: tile(None))
    pl.when((j >= jk0) & (j < jv0))(lambda: tile(k_ref))
    pl.when(j >= jv0)(lambda: tile(v_ref))


def _inproj(x2, w_main, w_gate, col_scale, layer, d_b, kv_buf, tm_pref):
    m, d = x2.shape
    depth, nj, _, tn = w_main.shape
    n = nj * tn
    tm = _tile(m, tm_pref)
    assert d_b % tn == 0
    nkv = d_b // tn
    jk0 = nj - 2 * nkv
    jv0 = nj - nkv
    x_spec = (pl.BlockSpec((tm, d), lambda i, j: (i, 0)) if x2.dtype == BF16 else
              pl.BlockSpec((tm, d), lambda i, j: (i, 0), pipeline_mode=pl.Buffered(1)))
    return pl.pallas_call(
        functools.partial(_inproj_kernel, jk0=jk0, jv0=jv0),
        grid=(m // tm, nj),
        in_specs=[x_spec,
                  pl.BlockSpec((None, None, d, tn), lambda i, j: (layer, j, 0, 0)),
                  pl.BlockSpec((None, d, GATE_COLS), lambda i, j: (layer, 0, 0)),
                  pl.BlockSpec((1, tn), lambda i, j: (0, j)),
                  pl.BlockSpec(memory_space=pl.ANY), pl.BlockSpec(memory_space=pl.ANY)],
        out_specs=[pl.BlockSpec((tm, tn), lambda i, j: (i, j)),
                   pl.BlockSpec((None, tm, tn), lambda i, j: (layer, i, jnp.clip(j - jk0, 0, nkv - 1))),
                   pl.BlockSpec((None, tm, tn), lambda i, j: (layer, i, jnp.clip(j - jv0, 0, nkv - 1))),
                   pl.BlockSpec((tm, GATE_COLS), lambda i, j: (i, 0))],
        out_shape=[jax.ShapeDtypeStruct((m, n), BF16),
                   jax.ShapeDtypeStruct((depth, m, d_b), F32),
                   jax.ShapeDtypeStruct((depth, m, d_b), F32),
                   jax.ShapeDtypeStruct((m, GATE_COLS), F32)],
        scratch_shapes=[] if x2.dtype == BF16 else [pltpu.VMEM((tm, d), BF16)],
        input_output_aliases={4: 1, 5: 2},
        compiler_params=_cparams(("parallel", "arbitrary")),
        name="inproj",
    )(x2, w_main, w_gate, col_scale, *kv_buf)


BIAS_LANES = 6


def _bias_placement(lane0, n_heads):
    import numpy as np
    place = np.zeros((3 * LANES, 2 * LANES), np.float32)
    ones = np.zeros((2, LANES), np.float32)
    for h in range(n_heads):
        for t in range(3):
            place[t * LANES + lane0 + h, BIAS_LANES * h + t] = 1.0
            place[t * LANES + lane0 + h, LANES + BIAS_LANES * h + 3 + t] = -1.0
            ones[0, BIAS_LANES * h + 3 + t] = 1.0
            ones[1, BIAS_LANES * h + t] = 1.0
    return jnp.asarray(place, BF16), jnp.asarray(ones, F32)


def _gates_kernel(gp_ref, bg_ref, place_ref, ones_ref, gc_ref, gr_ref, qb_ref, kb_ref, carry_ref,
                  *, n_raw, n_rows, n_keep):
    t = pl.program_id(1)

    @pl.when(t == 0)
    def _():
        carry_ref[...] = jnp.zeros_like(carry_ref)

    g = gp_ref[...] + bg_ref[...]
    tl = g.shape[0]
    lane = lax.broadcasted_iota(jnp.int32, g.shape, 1)
    raw = lane < n_raw
    val = jnp.where(raw, g, _log_sigmoid(g))
    c = _prefix_sum_rows(val) + carry_ref[...]
    carry_ref[...] = c[tl - 1:tl, :]
    out = jnp.where(raw, g, c)
    gc_ref[...] = out
    gr_ref[...] = jnp.concatenate([out.T[:n_rows, :], val.T[n_rows - n_keep:n_rows, :]], axis=0)
    split = jnp.concatenate(_split3(c * LOG2E), axis=1).astype(BF16)
    moved = jnp.dot(split, place_ref[...], preferred_element_type=F32)
    qb_ref[...] = (moved[:, :LANES] + ones_ref[0:1, :]).astype(BF16)
    kb_ref[...] = (moved[:, LANES:] + ones_ref[1:2, :]).astype(BF16)


def _gates(gpre3, b_gate, layer, n_raw, n_rows, n_keep, placement):
    b, l, _ = gpre3.shape
    tl = _tile(l, 512)
    place, ones = placement
    const = lambda i, t: (0, 0)
    tile = pl.BlockSpec((None, tl, GATE_COLS), lambda i, t: (i, t, 0))
    return pl.pallas_call(
        functools.partial(_gates_kernel, n_raw=n_raw, n_rows=n_rows, n_keep=n_keep),
        grid=(b, l // tl),
        in_specs=[tile,
                  pl.BlockSpec((None, 1, GATE_COLS), lambda i, t: (layer, 0, 0)),
                  pl.BlockSpec(place.shape, const),
                  pl.BlockSpec(ones.shape, const)],
        out_specs=[tile, pl.BlockSpec((None, n_rows + n_keep, tl), lambda i, t: (i, 0, t)), tile, tile],
        out_shape=[jax.ShapeDtypeStruct((b, l, GATE_COLS), F32),
                   jax.ShapeDtypeStruct((b, n_rows + n_keep, l), F32),
                   jax.ShapeDtypeStruct((b, l, GATE_COLS), BF16),
                   jax.ShapeDtypeStruct((b, l, GATE_COLS), BF16)],
        scratch_shapes=[pltpu.VMEM((1, GATE_COLS), F32)],
        compiler_params=_cparams(("parallel", "arbitrary")),
        name="gates",
    )(gpre3, b_gate, place, ones)


def _mlstm_kernel(q_ref, k_ref, v_ref, o_ref, gc_ref, gr_ref, gh_ref, c0_ref, n0_ref, m0_ref,
                  h_ref, c_ref, n_ref, m_ref, cta_s, m_s, fp_s, *, n_heads, dk, dv):
    c_idx = pl.program_id(1)
    n_chunks = pl.num_programs(1)
    lc = q_ref.shape[0]

    @pl.when(c_idx == 0)
    def _():
        lane = lax.broadcasted_iota(jnp.int32, (dk, LANES), 1)
        for h in range(n_heads):
            cta_s[h, :, :dv] = c0_ref[h].T
            n_cols = jnp.broadcast_to(n0_ref[h:h + 1, :], (LANES, dk)).T
            cta_s[h, :, dv:] = jnp.where(lane == 0, n_cols, 0.0)
        m_s[...] = m0_ref[...]
        fp_s[...] = jnp.zeros_like(fp_s)

    gc = gc_ref[...]
    gr = gr_ref[...]
    fp = fp_s[...]
    m_prev = m_s[...]
    row1 = lax.broadcasted_iota(jnp.int32, gc.shape, 0)
    b = gc - fp
    ig = pltpu.roll(gc, n_heads, 1)
    run = ig - b
    step = 1
    while step < lc:
        run = jnp.maximum(run, jnp.where(row1 >= step, pltpu.roll(run, step, 0), -jnp.inf))
        step *= 2
    g = b + m_prev
    m_t = jnp.maximum(g, b + run)
    w_inter = jnp.exp(g - m_t)
    e_neg = jnp.exp(-m_t)
    m_new = m_t[lc - 1:lc, :]
    b_last = b[lc - 1:lc, :]
    a_all = jnp.exp(b_last - b + ig - m_new)
    decay = jnp.exp(b_last + m_prev - m_new)
    m_s[...] = m_new
    fp_s[...] = gc[lc - 1:lc, :]

    row = lax.broadcasted_iota(jnp.int32, (lc, lc), 0)
    col = lax.broadcasted_iota(jnp.int32, (lc, lc), 1)
    causal = col <= row
    lane = lax.broadcasted_iota(jnp.int32, (lc, LANES), 1)
    ones_col = jnp.where(lane == 0, 1.0, 0.0).astype(BF16)
    for h in range(n_heads):
        hl = n_heads + h
        q = q_ref[:, h * dk:(h + 1) * dk]
        k = k_ref[:, h * dk:(h + 1) * dk]
        va = jnp.concatenate([v_ref[:, h * dv:(h + 1) * dv], ones_col], axis=1)
        ig_row = gr[h:h + 1, :]
        b_row = gr[hl:hl + 1, :] - fp[:, hl:hl + 1]
        d = jnp.where(causal, b[:, hl:hl + 1] - b_row + ig_row, -jnp.inf)
        s = _dot_nt(q, k) * jnp.exp(d - m_t[:, hl:hl + 1])
        cta = cta_s[h]
        both = (w_inter[:, hl:hl + 1] * jnp.dot(q, cta.astype(BF16), preferred_element_type=F32)
                + jnp.dot(s.astype(BF16), va, preferred_element_type=F32))
        hh = both[:, :dv] / jnp.maximum(jnp.abs(both[:, dv:dv + 1]), e_neg[:, hl:hl + 1])

        ka = (k.astype(F32) * a_all[:, hl:hl + 1]).astype(BF16)
        cta_s[h] = decay[:, hl:hl + 1] * cta + _dot_tn(ka, va)

        hn = hh * lax.rsqrt(jnp.mean(hh * hh, axis=1, keepdims=True) + HEAD_NORM_EPS)
        og = o_ref[:, h * dv:(h + 1) * dv].astype(F32)
        hn = hn * gh_ref[:, h * dv:(h + 1) * dv] * jax.nn.sigmoid(og)
        h_ref[:, h * dv:(h + 1) * dv] = hn.astype(h_ref.dtype)

    @pl.when(c_idx == n_chunks - 1)
    def _():
        for h in range(n_heads):
            c_ref[h] = cta_s[h, :, :dv].T
            n_ref[h:h + 1, :] = cta_s[h, :, dv:].T[0:1, :]
        m_ref[...] = m_s[...]


def _mlstm(slab3, gc, gr, g_head, layer, c0, n0, m0, n_heads, dk, dv, lc_pref):
    b, l, _ = slab3.shape
    lc = _tile(l, lc_pref)
    dq = n_heads * dk
    da = n_heads * dv
    assert da == 2 * dq and dk == LANES
    m0r = jnp.pad(m0, ((0, 0), (n_heads, GATE_COLS - 2 * n_heads))).reshape(b, 1, GATE_COLS)
    bmap = lambda i, c: (i, 0, 0)
    outs = pl.pallas_call(
        functools.partial(_mlstm_kernel, n_heads=n_heads, dk=dk, dv=dv),
        grid=(b, l // lc),
        in_specs=[pl.BlockSpec((None, lc, dq), lambda i, c: (i, c, 0)),
                  pl.BlockSpec((None, lc, dq), lambda i, c: (i, c, 1)),
                  pl.BlockSpec((None, lc, da), lambda i, c: (i, c, 1)),
                  pl.BlockSpec((None, lc, da), lambda i, c: (i, c, 2)),
                  pl.BlockSpec((None, lc, GATE_COLS), lambda i, c: (i, c, 0)),
                  pl.BlockSpec((None, gr.shape[1], lc), lambda i, c: (i, 0, c)),
                  pl.BlockSpec((None, 1, da), lambda i, c: (layer, 0, 0)),
                  pl.BlockSpec((None, n_heads, dv, dk), lambda i, c: (i, 0, 0, 0)),
                  pl.BlockSpec((None, n_heads, dk), bmap),
                  pl.BlockSpec((None, 1, GATE_COLS), bmap)],
        out_specs=[pl.BlockSpec((None, lc, da), lambda i, c: (i, c, 0)),
                   pl.BlockSpec((None, n_heads, dv, dk), lambda i, c: (i, 0, 0, 0)),
                   pl.BlockSpec((None, n_heads, dk), bmap),
                   pl.BlockSpec((None, 1, GATE_COLS), bmap)],
        out_shape=[jax.ShapeDtypeStruct((b, l, da), BF16),
                   jax.ShapeDtypeStruct((b, n_heads, dv, dk), F32),
                   jax.ShapeDtypeStruct((b, n_heads, dk), F32),
                   jax.ShapeDtypeStruct((b, 1, GATE_COLS), F32)],
        scratch_shapes=[pltpu.VMEM((n_heads, dk, dv + LANES), F32),
                        pltpu.VMEM((1, GATE_COLS), F32),
                        pltpu.VMEM((1, GATE_COLS), F32)],
        compiler_params=_cparams(("parallel", "arbitrary")),
        name="mlstm",
    )(slab3, slab3, slab3, slab3, gc, gr, g_head, c0, n0, m0r)
    h_a, c, n, m = outs
    return h_a, c, n, m[:, 0, n_heads:2 * n_heads]


def _fox_prompt_kernel(q_ref, k_ref, v_ref, qb_ref, kb_ref, *rest, tq, hp, cast_weights):
    if cast_weights:
        wu_ref, wd_ref, o_ref, wub_ref, wdb_ref = rest[:5]
        wub_ref[...] = wu_ref[...].astype(BF16)
        wdb_ref[...] = wd_ref[...].astype(BF16)
        rest = rest[5:]
    else:
        o_ref, rest = rest[0], rest[1:]
    qa_s, ka_s, va_s, p_s, m_s, al_s, acc_s = rest
    hg = pl.program_id(1)
    i = pl.program_id(2)
    dh = q_ref.shape[1] // hp

    @pl.when(i == 0)
    def _():
        kb = kb_ref[...]
        lane = lax.broadcasted_iota(jnp.int32, kb.shape, 1)
        ones_lane = jnp.where(lane == 0, 1.0, 0.0).astype(BF16)
        for u in range(hp):
            first = BIAS_LANES * (hg * hp + u)
            ka_s[u, :, :dh] = k_ref[:, u * dh:(u + 1) * dh]
            ka_s[u, :, dh:] = jnp.where((lane >= first) & (lane < first + BIAS_LANES), kb, jnp.zeros_like(kb))
            va_s[u, :, :dh] = v_ref[:, u * dh:(u + 1) * dh]
            va_s[u, :, dh:] = ones_lane

    for u in range(hp):
        qa_s[u, :, :dh] = q_ref[:, u * dh:(u + 1) * dh]
        qa_s[u, :, dh:] = qb_ref[...]

    def logits(u, j):
        start = pl.multiple_of(j * tq, tq)
        return _dot_nt(qa_s[u], ka_s[u, pl.ds(start, tq), :])

    def softmax(u, s, diagonal):
        if diagonal:
            row = lax.broadcasted_iota(jnp.int32, s.shape, 0)
            col = lax.broadcasted_iota(jnp.int32, s.shape, 1)
            s = jnp.where(col <= row, s, -jnp.inf)
            m_new = jnp.broadcast_to(jnp.max(s, axis=1, keepdims=True), (tq, LANES))
            al_s[u] = jnp.zeros_like(m_new)
        else:
            m = m_s[u]
            m_new = jnp.maximum(m, jnp.max(s, axis=1, keepdims=True))
            al_s[u] = jnp.exp2(m - m_new)
        m_s[u] = m_new
        p_s[u] = jnp.exp2(s - jnp.concatenate([m_new] * (tq // LANES), axis=1)).astype(BF16)

    def values(u, j):
        start = pl.multiple_of(j * tq, tq)
        pv = jnp.dot(p_s[u], va_s[u, pl.ds(start, tq), :], preferred_element_type=F32)
        return jnp.concatenate([al_s[u]] * (2 * dh // LANES), axis=1) * acc_s[u] + pv

    for u in range(hp):
        acc_s[u] = jnp.zeros(acc_s.shape[1:], F32)
        softmax(u, logits(u, i), True)

    def trip(j, _):
        prev = jnp.where(j == 0, i, j - 1)
        s = [logits(u, j) for u in range(hp)]
        for u in range(hp):
            acc_s[u] = values(u, prev)
        for u in range(hp):
            softmax(u, s[u], False)
        return 0

    lax.fori_loop(0, i, trip, 0)
    last = jnp.where(i == 0, i, i - 1)
    for u in range(hp):
        acc = values(u, last)
        o_ref[:, u * dh:(u + 1) * dh] = (acc[:, :dh] / acc[:, dh:dh + 1]).astype(o_ref.dtype)


def _fox_prompt(slab3, qb_all, kb_all, n_heads, dh, q_col, tq_pref, hp, mlp_w=None, layer=0):
    b, l, _ = slab3.shape
    tq = _tile(l, tq_pref)
    assert n_heads % hp == 0 and q_col % hp == 0
    qb, ng, w = q_col // hp, n_heads // hp, hp * dh
    nq = l // tq
    n_steps = b * ng * nq
    in_specs = [pl.BlockSpec((None, tq, w), lambda bi, h, i: (bi, i, qb + h)),
                pl.BlockSpec((None, l, w), lambda bi, h, i: (bi, 0, qb + ng + h)),
                pl.BlockSpec((None, l, w), lambda bi, h, i: (bi, 0, qb + 2 * ng + h)),
                pl.BlockSpec((None, tq, GATE_COLS), lambda bi, h, i: (bi, i, 0)),
                pl.BlockSpec((None, l, GATE_COLS), lambda bi, h, i: (bi, 0, 0))]
    out_specs = [pl.BlockSpec((None, tq, w), lambda bi, h, i: (bi, i, h))]
    out_shape = [jax.ShapeDtypeStruct((b, l, n_heads * dh), BF16)]
    args = [slab3, slab3, slab3, qb_all, kb_all]
    cast_weights = False
    if mlp_w is not None:
        w_up, w_down = mlp_w
        _, d, f = w_up.shape
        cast_weights = d % (16 * n_steps) == 0 and f % (16 * n_steps) == 0
    if cast_weights:
        ru, rd = d // n_steps, f // n_steps
        step = lambda bi, h, i: (bi * ng + h) * nq + i
        in_specs += [pl.BlockSpec((None, ru, f), lambda bi, h, i: (layer, step(bi, h, i), 0)),
                     pl.BlockSpec((None, rd, d), lambda bi, h, i: (layer, step(bi, h, i), 0))]
        out_specs += [pl.BlockSpec((ru, f), lambda bi, h, i: (step(bi, h, i), 0)),
                      pl.BlockSpec((rd, d), lambda bi, h, i: (step(bi, h, i), 0))]
        out_shape += [jax.ShapeDtypeStruct((d, f), BF16), jax.ShapeDtypeStruct((f, d), BF16)]
        args += [w_up, w_down]
    outs = pl.pallas_call(
        functools.partial(_fox_prompt_kernel, tq=tq, hp=hp, cast_weights=cast_weights),
        grid=(b, ng, nq),
        in_specs=in_specs,
        out_specs=out_specs,
        out_shape=out_shape,
        scratch_shapes=[pltpu.VMEM((hp, tq, 2 * dh), BF16), pltpu.VMEM((hp, l, 2 * dh), BF16),
                        pltpu.VMEM((hp, l, 2 * dh), BF16), pltpu.VMEM((hp, tq, tq), BF16),
                        pltpu.VMEM((hp, tq, LANES), F32), pltpu.VMEM((hp, tq, LANES), F32),
                        pltpu.VMEM((hp, tq, 2 * dh), F32)],
        compiler_params=_cparams(("parallel", "parallel", "arbitrary")),
        name="fox_prompt",
    )(*args)
    if cast_weights:
        return outs[0], (outs[1], outs[2])
    if mlp_w is not None:
        return outs[0], (mlp_w[0][layer].astype(BF16), mlp_w[1][layer].astype(BF16))
    return outs[0], None


def _fox_sample_kernel(q_ref, k_ref, v_ref, ck_ref, cv_ref, clf_ref, hcol_ref, gc_ref, o_ref, *, n_heads, dh, lane0):
    l = q_ref.shape[0]
    ph = ck_ref.shape[0]
    rows = n_heads * l
    cum = clf_ref[...]
    lane = lax.broadcasted_iota(jnp.int32, cum.shape, 1)
    step = n_heads
    while step < ph:
        cum = cum + jnp.where(lane >= step, pltpu.roll(cum, step, 1), 0.0)
        step *= 2

    gc = gc_ref[...]
    frame = lax.broadcasted_iota(jnp.int32, (l, 1), 0).astype(F32)
    stack = lambda pieces: jnp.concatenate(pieces, axis=0)
    qa = stack([q_ref[:, h * dh:(h + 1) * dh] for h in range(n_heads)])
    ka = stack([k_ref[:, h * dh:(h + 1) * dh] for h in range(n_heads)])
    va = stack([v_ref[:, h * dh:(h + 1) * dh] for h in range(n_heads)])
    f_col = stack([gc[:, lane0 + h:lane0 + h + 1] + cum[:, ph - n_heads + h:ph - n_heads + h + 1]
                   for h in range(n_heads)]) * LOG2E
    head_col = stack([jnp.full((l, 1), float(h), F32) for h in range(n_heads)])
    frame_col = stack([frame] * n_heads)
    as_row = lambda c: jnp.broadcast_to(c, (rows, rows)).T[0:1, :]
    f_row, head_row, frame_row = as_row(f_col), as_row(head_col), as_row(frame_col)

    s_old = _dot_nt(qa, ck_ref[...].astype(BF16)) + f_col - cum * LOG2E
    s_old = jnp.where(hcol_ref[...] == head_col, s_old, -jnp.inf)
    s_new = _dot_nt(qa, ka) + f_col - f_row
    s_new = jnp.where(head_row == head_col, jnp.where(frame_row <= frame_col, s_new, -jnp.inf), -jnp.inf)
    m = jnp.maximum(jnp.max(s_old, axis=1, keepdims=True), jnp.max(s_new, axis=1, keepdims=True))
    p_old = jnp.exp2(s_old - m)
    p_new = jnp.exp2(s_new - m)
    den = jnp.sum(p_old, axis=1, keepdims=True) + jnp.sum(p_new, axis=1, keepdims=True)
    acc = (jnp.dot(p_old.astype(BF16), cv_ref[...].astype(BF16), preferred_element_type=F32)
           + jnp.dot(p_new.astype(BF16), va, preferred_element_type=F32))
    out = (acc / den).astype(o_ref.dtype)
    for h in range(n_heads):
        o_ref[:, h * dh:(h + 1) * dh] = out[h * l:(h + 1) * l, :]


def _fox_sample(slab3, ck, cv, clf, layer, gc, n_heads, dh, q_col, lane0):
    b, l, _ = slab3.shape
    depth, _, past = ck.shape[:3]
    ph = past * n_heads
    dd = n_heads * dh
    qb = q_col // n_heads
    hcol = jnp.tile(jnp.arange(n_heads, dtype=F32), past).reshape(1, ph)
    cache_rows = pl.BlockSpec((None, None, ph, dh), lambda bi: (layer, bi, 0, 0))
    return pl.pallas_call(
        functools.partial(_fox_sample_kernel, n_heads=n_heads, dh=dh, lane0=lane0),
        grid=(b,),
        in_specs=[pl.BlockSpec((None, l, dd), lambda bi: (bi, 0, qb)),
                  pl.BlockSpec((None, l, dd), lambda bi: (bi, 0, qb + 1)),
                  pl.BlockSpec((None, l, dd), lambda bi: (bi, 0, qb + 2)),
                  cache_rows, cache_rows,
                  pl.BlockSpec((None, None, 1, ph), lambda bi: (layer, bi, 0, 0)),
                  pl.BlockSpec((1, ph), lambda bi: (0, 0)),
                  pl.BlockSpec((None, l, GATE_COLS), lambda bi: (bi, 0, 0))],
        out_specs=pl.BlockSpec((None, l, dd), lambda bi: (bi, 0, 0)),
        out_shape=jax.ShapeDtypeStruct((b, l, dd), BF16),
        compiler_params=_cparams(("parallel",)),
        name="fox_sample",
    )(slab3, slab3, slab3, ck.reshape(depth, b, ph, dh), cv.reshape(depth, b, ph, dh),
      clf.reshape(depth, b, 1, ph), hcol, gc)


def _outproj_kernel(ha_ref, hb_ref, x_ref, wa_ref, wb_ref, g_ref, b_ref, y_ref, *, alpha):
    tm = x_ref.shape[0]
    tr = min(tm, 256)
    for r in range(tm // tr):
        rows = slice(r * tr, (r + 1) * tr)
        mix = (jnp.dot(ha_ref[rows, :], wa_ref[...], preferred_element_type=F32)
               + jnp.dot(hb_ref[rows, :], wb_ref[...], preferred_element_type=F32))
        y_ref[rows, :] = _layer_norm(alpha * x_ref[rows, :] + mix, g_ref[...], b_ref[...])


def _outproj(ha2, hb2, x2, w_out, ln_g, ln_b, layer, alpha, tm_pref):
    m, d = x2.shape
    da = ha2.shape[1]
    db = hb2.shape[1]
    assert da == db
    tm = _tile(m, tm_pref)
    vec = pl.BlockSpec((None, 1, d), lambda i: (layer, 0, 0))
    return pl.pallas_call(
        functools.partial(_outproj_kernel, alpha=alpha),
        grid=(m // tm,),
        in_specs=[pl.BlockSpec((tm, da), lambda i: (i, 0)),
                  pl.BlockSpec((tm, db), lambda i: (i, 0)),
                  pl.BlockSpec((tm, d), lambda i: (i, 0)),
                  pl.BlockSpec((None, da, d), lambda i: (layer, 0, 0)),
                  pl.BlockSpec((None, db, d), lambda i: (layer, 1, 0)),
                  vec, vec],
        out_specs=pl.BlockSpec((tm, d), lambda i: (i, 0)),
        out_shape=jax.ShapeDtypeStruct((m, d), F32),
        compiler_params=_cparams(("parallel",)),
        name="outproj_ln",
    )(ha2, hb2, x2, w_out, w_out, ln_g, ln_b)


def _ffn_kernel(x_ref, wu_ref, wd_ref, g_ref, b_ref, y_ref, *rest, alpha):
    yb_ref = rest[0] if len(rest) == 3 else None
    xb_ref, acc_ref = rest[-2:]
    j = pl.program_id(1)

    @pl.when(j == 0)
    def _():
        xb_ref[...] = x_ref[...].astype(BF16)
        acc_ref[...] = jnp.zeros_like(acc_ref)

    u = jnp.dot(xb_ref[...], wu_ref[...], preferred_element_type=F32)
    u = jnp.maximum(u, 0.0)
    acc_ref[...] += jnp.dot((u * u).astype(BF16), wd_ref[...], preferred_element_type=F32)

    @pl.when(j == pl.num_programs(1) - 1)
    def _():
        y = _layer_norm(alpha * x_ref[...] + acc_ref[...], g_ref[...], b_ref[...])
        y_ref[...] = y
        if yb_ref is not None:
            yb_ref[...] = y.astype(BF16)


def _ffn(x2, w_up, w_down, ln_g, ln_b, layer, alpha, tm_pref, tf_pref, with_bf16):
    m, d = x2.shape
    f = w_up.shape[1]
    tm = _tile(m, tm_pref)
    tf = _tile(f, tf_pref)
    vec = pl.BlockSpec((None, 1, d), lambda i, j: (layer, 0, 0))
    rows = pl.BlockSpec((tm, d), lambda i, j: (i, 0))
    n_out = 2 if with_bf16 else 1
    outs = pl.pallas_call(
        functools.partial(_ffn_kernel, alpha=alpha),
        grid=(m // tm, f // tf),
        in_specs=[rows,
                  pl.BlockSpec((d, tf), lambda i, j: (0, j)),
                  pl.BlockSpec((tf, d), lambda i, j: (j, 0)),
                  vec, vec],
        out_specs=[rows, rows][:n_out],
        out_shape=[jax.ShapeDtypeStruct((m, d), F32), jax.ShapeDtypeStruct((m, d), BF16)][:n_out],
        scratch_shapes=[pltpu.VMEM((tm, d), BF16), pltpu.VMEM((tm, d), F32)],
        compiler_params=_cparams(("parallel", "arbitrary")),
        name="ffn_ln",
    )(x2, w_up, w_down, ln_g, ln_b)
    return (outs[0], outs[1]) if with_bf16 else (outs[0], None)


def _layer(x, xb, layer, last, cache, kv_buf, mlp_w, wts, dims, tiles):
    h_a, dk, dv, h_b, dh = dims
    (w_main, w_gate, col_scale, b_gate, g_head, w_out, ln1_g, ln1_b, ln2_g, ln2_b, alpha) = wts
    b, l, d = x.shape
    m = b * l
    d_b = h_b * dh
    x2 = x.reshape(m, d)

    x_in, tm_in = (x2, tiles["tm_in"]) if xb is None else (xb, tiles["tm_in_bf16"])
    slab, k_all, v_all, gpre = _inproj(x_in, w_main, w_gate, col_scale, layer, d_b, kv_buf, tm_in)
    slab3 = slab.reshape(b, l, slab.shape[1])
    n_rows = 2 * h_a + h_b
    gc, gr, qb_all, kb_all = _gates(gpre.reshape(b, l, GATE_COLS), b_gate, layer, h_a, n_rows, h_b,
                                    _bias_placement(2 * h_a, h_b))
    q_col = (2 * h_a * dk + 2 * h_a * dv) // dh

    if cache is None:
        c0 = jnp.zeros((b, h_a, dv, dk), F32)
        n0 = jnp.zeros((b, h_a, dk), F32)
        m0 = jnp.zeros((b, h_a), F32)
        hb, mlp_w = _fox_prompt(slab3, qb_all, kb_all, h_b, dh, q_col, tiles["tq"], tiles["fox_heads"], mlp_w, layer)
    else:
        ck, cv, clf, c_all, n_all, m_all = cache
        c0, n0, m0 = c_all[layer], n_all[layer], m_all[layer]
        hb = _fox_sample(slab3, ck, cv, clf, layer, gc, h_b, dh, q_col, 2 * h_a)
    ha, c, n, mm = _mlstm(slab3, gc, gr, g_head, layer, c0, n0, m0, h_a, dk, dv, tiles["lc"])

    x1 = _outproj(ha.reshape(m, -1), hb.reshape(m, -1), x2, w_out, ln1_g, ln1_b, layer, alpha, tiles["tm_out"])
    y, yb = _ffn(x1, mlp_w[0], mlp_w[1], ln2_g, ln2_b, layer, alpha, tiles["tm_ffn"], tiles["tf_ffn"], not last)
    lfb = jnp.transpose(gr[:, n_rows:, :], (0, 2, 1))
    return y.reshape(b, l, d), yb, (k_all, v_all), (lfb, c, n, mm), mlp_w


def kernel(x_prompt, x_sample, cache_fox_k, cache_fox_v, cache_fox_logf, state_mlstm_c, state_mlstm_n,
           state_mlstm_m, w_in, b_gates, g_mlstm, w_out, ln1_g, ln1_b, w_up, w_down, ln2_g, ln2_b):
    depth, d_model, _ = w_in.shape
    _, _, _, h_b, dh = cache_fox_k.shape
    _, _, h_a, dv, dk = state_mlstm_c.shape
    d_a = h_a * dv
    d_b = h_b * dh
    dims = (h_a, dk, dv, h_b, dh)
    alpha = (2 * depth) ** 0.25
    widths = (h_a * dk, h_a * dk, d_a, d_a, h_a, h_a, d_b, d_b, d_b, h_b)
    offs = [0]
    for w in widths:
        offs.append(offs[-1] + w)
    n_gates = 2 * h_a + h_b

    tiles = dict(tm_in=1024, tm_in_bf16=1024, tn_in=1024, tq=1024, fox_heads=2, lc=256, tm_out=512, tm_ffn=512, tf_ffn=1024)

    w_main, w_gate = _win_prep(w_in, ((offs[0], offs[4]), (offs[6], offs[9])), ((offs[4], offs[6]), (offs[9], offs[10])),
                               tiles["tn_in"])
    b_gate = jnp.pad(b_gates, ((0, 0), (0, GATE_COLS - n_gates))).reshape(depth, 1, GATE_COLS)
    n_main = w_main.shape[1] * w_main.shape[3]
    col_scale = jnp.concatenate([jnp.full((h_a * dk,), dk ** -0.5, F32),
                                 jnp.ones((offs[4] - offs[1],), F32),
                                 jnp.full((d_b,), dh ** -0.5 * LOG2E, F32),
                                 jnp.ones((2 * d_b,), F32)]).reshape(1, n_main)
    wts = (w_main, w_gate, col_scale, b_gate, g_mlstm.reshape(depth, 1, d_a), w_out.astype(BF16),
           ln1_g.reshape(depth, 1, d_model), ln1_b.reshape(depth, 1, d_model),
           ln2_g.reshape(depth, 1, d_model), ln2_b.reshape(depth, 1, d_model), alpha)
    cache = (cache_fox_k, cache_fox_v, cache_fox_logf, state_mlstm_c, state_mlstm_n, state_mlstm_m)


    yp, ys = x_prompt, x_sample
    ypb, ysb = None, x_sample.reshape(-1, d_model).astype(BF16)
    kv_p = tuple(jnp.zeros((depth, x_prompt.shape[0] * x_prompt.shape[1], d_b), F32) for _ in range(2))
    kv_s = tuple(jnp.zeros((depth, x_sample.shape[0] * x_sample.shape[1], d_b), F32) for _ in range(2))
    small_p, small_s = [], []
    for layer in range(depth):
        last = layer == depth - 1
        yp, ypb, kv_p, st_p, mlp_bf16 = _layer(yp, ypb, layer, last, None, kv_p, (w_up, w_down), wts, dims, tiles)
        ys, ysb, kv_s, st_s, _ = _layer(ys, ysb, layer, last, cache, kv_s, mlp_bf16, wts, dims, tiles)
        small_p.append(st_p)
        small_s.append(st_s)
    stack = lambda states, i: jnp.stack([s[i] for s in states], axis=0)
    bp, lp, _ = x_prompt.shape
    bs, ls, _ = x_sample.shape
    return ((yp, ys)
            + tuple(a.reshape(depth, bp, lp, h_b, dh) for a in kv_p) + tuple(stack(small_p, i) for i in range(4))
            + tuple(a.reshape(depth, bs, ls, h_b, dh) for a in kv_s) + tuple(stack(small_s, i) for i in range(4)))
```
